```python
import math
import jax, jax.numpy as jnp
from jax import lax
import numpy as np

D_MODEL = 1024
BATCH = 16
SEQ = 2048
DEPTH = 4

CHUNK = 64
RET_HEADS = 4
RET_QK_DIM = 128
RET_V_DIM = 256
RET_QK = RET_HEADS * RET_QK_DIM
RET_V = RET_HEADS * RET_V_DIM
ROPE_BASE = 10000.0
SG_BLOCK = 128
SG_GROUPS = 8
SG_WIDTH = D_MODEL
SG_GROUP_DIM = SG_WIDTH // SG_GROUPS
IN_SIZES = [RET_QK, RET_QK, RET_V, RET_V, SG_WIDTH, SG_WIDTH, D_MODEL, D_MODEL]
IN_SPLITS = [int(s) for s in np.cumsum(IN_SIZES)[:-1]]
IN_WIDTH = int(sum(IN_SIZES))
N_EXPERTS = 32
TOP_K = 4
D_EXPERT = D_MODEL
SWIGLU_LIMIT = 7.0
SWIGLU_ALPHA = 1.702
MOE_BLOCK = 128
LN_EPS = 1e-5
DN_ALPHA = (2 * DEPTH) ** 0.25
DN_BETA = (8 * DEPTH) ** -0.25

kernel_name = "retention_sgu_moe_deepnorm_hybrid"


def layer_norm(x, g, b):
    xf = x.astype(jnp.float32)
    mu = jnp.mean(xf, -1, keepdims=True)
    var = jnp.mean(jnp.square(xf - mu), -1, keepdims=True)
    return ((xf - mu) * lax.rsqrt(var + LN_EPS) * g + b).astype(x.dtype)


def head_norm(y):
    yf = y.astype(jnp.float32)
    mu = jnp.mean(yf, -1, keepdims=True)
    var = jnp.mean(jnp.square(yf - mu), -1, keepdims=True)
    return (yf - mu) * lax.rsqrt(var + LN_EPS)


def rotary(x, pos):
    half = x.shape[-1] // 2
    inv = ROPE_BASE ** (-jnp.arange(half, dtype=jnp.float32) / half)
    ang = pos.astype(jnp.float32)[:, None] * inv[None, :]
    cos = jnp.cos(ang)[None, :, None, :]
    sin = jnp.sin(ang)[None, :, None, :]
    x1, x2 = x[..., :half], x[..., half:]
    return jnp.concatenate([x1 * cos - x2 * sin, x1 * sin + x2 * cos], -1)


def retention(q, k, v):
    B, S, H, dk = q.shape
    dv = v.shape[-1]
    n = S // CHUNK
    log_g = jnp.log(1.0 - jnp.exp(jnp.linspace(math.log(1.0 / 32), math.log(1.0 / 512), H)))
    idx = jnp.arange(CHUNK, dtype=jnp.float32)
    diff = idx[:, None] - idx[None, :]
    inner_decay = jnp.where(diff >= 0, jnp.exp(log_g[:, None, None] * jnp.maximum(diff, 0.0)), 0.0)
    q_decay = jnp.exp(log_g[:, None] * (idx[None, :] + 1.0))
    k_decay = jnp.exp(log_g[:, None] * (CHUNK - 1.0 - idx[None, :]))
    chunk_decay = jnp.exp(log_g * CHUNK)
    qc = q.astype(jnp.float32).reshape(B, n, CHUNK, H, dk)
    kc = k.astype(jnp.float32).reshape(B, n, CHUNK, H, dk)
    vc = v.astype(jnp.float32).reshape(B, n, CHUNK, H, dv)
    scores = jnp.einsum('bnihd,bnjhd->bnhij', qc, kc) * inner_decay
    inner = jnp.einsum('bnhij,bnjhe->bnihe', scores, vc)

    def step(state, blk):
        qb, kb, vb = blk
        cross = jnp.einsum('bihd,bhde->bihe', qb, state) * q_decay.T[None, :, :, None]
        state = state * chunk_decay[None, :, None, None] + jnp.einsum(
            'bjhd,hj,bjhe->bhde', kb, k_decay, vb)
        return state, cross

    init = jnp.zeros((B, H, dk, dv), jnp.float32)
    _, cross = lax.scan(step, init, (qc.swapaxes(0, 1), kc.swapaxes(0, 1), vc.swapaxes(0, 1)))
    out = inner + cross.swapaxes(0, 1)
    return out.reshape(B, S, H, dv)


def spatial_gating(u, vs, ln_g, ln_b, w_s, b_s):
    B, S, _ = u.shape
    vn = layer_norm(vs, ln_g, ln_b)
    nb = S // SG_BLOCK
    vb = vn.reshape(B, nb, SG_BLOCK, SG_GROUPS, SG_GROUP_DIM)
    chunk_id = jnp.arange(SG_BLOCK) // CHUNK
    mask = chunk_id[:, None] >= chunk_id[None, :]
    w = jnp.where(mask[None], w_s, 0.0)
    mixed = jnp.einsum('gqp,bnpgc->bnqgc', w, vb) + b_s.T[None, None, :, :, None]
    return u * mixed.reshape(B, S, SG_WIDTH)


def moe(x, router_w, router_b, w1, b1, w2, b2):
    B, S, D = x.shape
    T = B * S
    A = T * TOP_K
    xt = x.reshape(T, D)
    logits = (xt @ router_w + router_b).astype(jnp.float32)
    top_val, top_idx = lax.top_k(logits, TOP_K)
    gates = jax.nn.softmax(top_val, axis=-1)
    e_flat = top_idx.reshape(A)
    tok_flat = jnp.arange(A, dtype=jnp.int32) // TOP_K
    g_flat = gates.reshape(A)
    order = jnp.argsort(e_flat)
    e_sorted = e_flat[order]
    counts = jnp.bincount(e_flat, length=N_EXPERTS)
    padded = (counts + MOE_BLOCK - 1) // MOE_BLOCK * MOE_BLOCK
    start = jnp.cumsum(counts) - counts
    pend = jnp.cumsum(padded)
    pstart = pend - padded
    dest = pstart[e_sorted] + jnp.arange(A) - start[e_sorted]
    P = A + N_EXPERTS * MOE_BLOCK
    n_blocks = P // MOE_BLOCK
    buf_tok = jnp.zeros((P,), jnp.int32).at[dest].set(tok_flat[order])
    buf_gate = jnp.zeros((P,), jnp.float32).at[dest].set(g_flat[order])
    block_start = jnp.arange(n_blocks) * MOE_BLOCK
    block_exp = jnp.minimum(jnp.searchsorted(pend, block_start, side='right'), N_EXPERTS - 1)
    xb = xt[buf_tok].reshape(n_blocks, MOE_BLOCK, D)

    def expert_block(args):
        xblk, e = args
        h = xblk @ w1[e] + b1[e]
        gate = jnp.minimum(h[:, :D_EXPERT], SWIGLU_LIMIT)
        up = jnp.clip(h[:, D_EXPERT:], -SWIGLU_LIMIT, SWIGLU_LIMIT)
        act = gate * jax.nn.sigmoid(SWIGLU_ALPHA * gate) * (up + 1.0)
        return act @ w2[e] + b2[e]

    yb = lax.map(expert_block, (xb, block_exp)).reshape(P, D)
    yw = yb * buf_gate[:, None].astype(yb.dtype)
    y = jnp.zeros((T, D), yw.dtype).at[buf_tok].add(yw)
    return y.reshape(B, S, D)


def mixer(x, pos, w_in, p_ret, sg_ln_g, sg_ln_b, sg_w, sg_b, p_sg, w_o):
    B, S, _ = x.shape
    proj = x @ w_in
    q, k, v, g, u, vs, ga, gb = jnp.split(proj, IN_SPLITS, axis=-1)
    q = rotary(q.reshape(B, S, RET_HEADS, RET_QK_DIM), pos)
    k = rotary(k.reshape(B, S, RET_HEADS, RET_QK_DIM), pos) * (RET_QK_DIM ** -0.5)
    y_ret = retention(q, k, v.reshape(B, S, RET_HEADS, RET_V_DIM))
    y_ret = head_norm(y_ret).reshape(B, S, RET_V) * jax.nn.silu(g)
    y_sg = spatial_gating(jax.nn.gelu(u), jax.nn.gelu(vs), sg_ln_g, sg_ln_b, sg_w, sg_b)
    h = jax.nn.sigmoid(ga) * (y_ret @ p_ret) + jax.nn.sigmoid(gb) * (y_sg @ p_sg)
    return h @ w_o


def setup_inputs(seed: int = 0) -> dict:
    key = jax.random.key(seed)
    ks = jax.random.split(key, 20)
    L, D, E, F = DEPTH, D_MODEL, N_EXPERTS, D_EXPERT
    nrm = lambda k, shape, s: jax.random.normal(k, shape, jnp.float32) * s
    return {
        "x": nrm(ks[0], (BATCH, SEQ, D), 1.0),
        "w_in": nrm(ks[1], (L, D, IN_WIDTH), D ** -0.5),
        "p_ret": nrm(ks[2], (L, RET_V, D), RET_V ** -0.5),
        "sg_ln_g": 1.0 + nrm(ks[3], (L, SG_WIDTH), 0.01),
        "sg_ln_b": nrm(ks[4], (L, SG_WIDTH), 0.01),
        "sg_w": nrm(ks[5], (L, SG_GROUPS, SG_BLOCK, SG_BLOCK), SG_BLOCK ** -0.5),
        "sg_b": 1.0 + nrm(ks[6], (L, SG_GROUPS, SG_BLOCK), 0.01),
        "p_sg": nrm(ks[7], (L, SG_WIDTH, D), SG_WIDTH ** -0.5),
        "w_o": nrm(ks[8], (L, D, D), DN_BETA * D ** -0.5),
        "ln1_g": 1.0 + nrm(ks[9], (L, D), 0.01),
        "ln1_b": nrm(ks[10], (L, D), 0.01),
        "router_w": nrm(ks[11], (L, D, E), D ** -0.5),
        "router_b": nrm(ks[12], (L, E), 0.01),
        "e_w1": nrm(ks[13], (L, E, D, 2 * F), D ** -0.5),
        "e_b1": nrm(ks[14], (L, E, 2 * F), 0.01),
        "e_w2": nrm(ks[15], (L, E, F, D), DN_BETA * F ** -0.5),
        "e_b2": nrm(ks[16], (L, E, D), 0.01),
        "ln2_g": 1.0 + nrm(ks[17], (L, D), 0.01),
        "ln2_b": nrm(ks[18], (L, D), 0.01),
    }


def reference(x, w_in, p_ret, sg_ln_g, sg_ln_b, sg_w, sg_b, p_sg, w_o, ln1_g, ln1_b,
              router_w, router_b, e_w1, e_b1, e_w2, e_b2, ln2_g, ln2_b):
    pos = jnp.arange(x.shape[1], dtype=jnp.int32)
    for l in range(DEPTH):
        mix = mixer(x, pos, w_in[l], p_ret[l], sg_ln_g[l], sg_ln_b[l], sg_w[l], sg_b[l],
                    p_sg[l], w_o[l])
        x = layer_norm(DN_ALPHA * x + mix, ln1_g[l], ln1_b[l])
        ffn = moe(x, router_w[l], router_b[l], e_w1[l], e_b1[l], e_w2[l], e_b2[l])
        x = layer_norm(DN_ALPHA * x + ffn, ln2_g[l], ln2_b[l])
    return x
```

```python
import functools
import math

import jax
import jax.numpy as jnp
import numpy as np
from jax import lax
from jax.experimental import pallas as pl
from jax.experimental.pallas import tpu as pltpu

F32 = jnp.float32
BF16 = jnp.bfloat16

D_MODEL = 1024
DEPTH = 4
RET_HEADS = 4
RET_QK_DIM = 128
RET_V_DIM = 256
RET_QK = RET_HEADS * RET_QK_DIM
RET_V = RET_HEADS * RET_V_DIM
ROPE_BASE = 10000.0
SG_BLOCK = 128
SG_CHUNK = 64
SG_GROUPS = 8
SG_WIDTH = D_MODEL
N_EXPERTS = 32
TOP_K = 4
D_EXPERT = D_MODEL
SWIGLU_LIMIT = 7.0
SWIGLU_ALPHA = 1.702
LN_EPS = 1e-5
DN_ALPHA = (2 * DEPTH) ** 0.25

OFF_Q = 0
OFF_K = OFF_Q + RET_QK
OFF_V = OFF_K + RET_QK
OFF_G = OFF_V + RET_V
OFF_U = OFF_G + RET_V
OFF_VS = OFF_U + SG_WIDTH
OFF_GA = OFF_VS + SG_WIDTH
OFF_GB = OFF_GA + D_MODEL
IN_WIDTH = OFF_GB + D_MODEL

SUBLANES_V7X = 8
TOK_BLOCK = 256
PIECE = SUBLANES_V7X
GROUP_ROWS = TOK_BLOCK * TOP_K + N_EXPERTS * PIECE
GROUP_PIECES = GROUP_ROWS // PIECE
EXP_BLOCK = 256
EXP_PIECES = EXP_BLOCK // PIECE
F_CHUNK = 256
VMEM_LIMIT_V7X = 56 * 1024 * 1024


def _layer_norm(x, g, b):
    mu = jnp.mean(x, axis=-1, keepdims=True)
    xc = x - mu
    var = jnp.mean(xc * xc, axis=-1, keepdims=True)
    return xc * lax.rsqrt(var + LN_EPS) * g + b


def _gelu_tanh(x):
    c = math.sqrt(2.0 / math.pi)
    return 0.5 * x * (1.0 + jnp.tanh(c * (x + 0.044715 * (x * x * x))))


def _dot(a, b):
    return jnp.dot(a, b, preferred_element_type=F32)


def _dot_nt(a, b):
    return lax.dot_general(a, b, (((1,), (1,)), ((), ())), preferred_element_type=F32)


def _dot_tn(a, b):
    return lax.dot_general(a, b, (((0,), (0,)), ((), ())), preferred_element_type=F32)


def _split_bf16(x):
    hi = x.astype(BF16)
    lo = (x - hi.astype(F32)).astype(BF16)
    return hi, lo


def _mixer_kernel(x_ref, w_in_ref, p_ret_ref, p_sg_ref, w_o_ref, sgw_ref, sgbt_ref,
                  sglng_ref, sglnb_ref, ln1g_ref, ln1b_ref, rwt_ref, rbb_ref,
                  cq_ref, sq_ref, ck_ref, sk_ref, dmat_ref, qdec_ref, kdec_ref,
                  x1_ref, pos_ref, gate_ref, cnt_ref, xs_ref,
                  state_ref, ysg_ref, *, chunk_decay):
    ts = TOK_BLOCK

    @pl.when(pl.program_id(1) == 0)
    def _():
        state_ref[...] = jnp.zeros_like(state_ref)

    xb = x_ref[...]
    xbf = xb.astype(BF16)

    def proj(off, width):
        return _dot(xbf, w_in_ref[:, off:off + width])

    cq, sq, ck, sk = cq_ref[...], sq_ref[...], ck_ref[...], sk_ref[...]
    acc_ret = jnp.zeros((ts, D_MODEL), F32)
    for h in range(RET_HEADS):
        q = proj(OFF_Q + h * RET_QK_DIM, RET_QK_DIM)
        k = proj(OFF_K + h * RET_QK_DIM, RET_QK_DIM)
        v = proj(OFF_V + h * RET_V_DIM, RET_V_DIM).astype(BF16)
        qr = q * cq + pltpu.roll(q, RET_QK_DIM // 2, 1) * sq
        kr = k * ck + pltpu.roll(k, RET_QK_DIM // 2, 1) * sk
        scores = _dot_nt(qr.astype(BF16), kr.astype(BF16)) * dmat_ref[h]
        inner = _dot(scores.astype(BF16), v)
        st = state_ref[h]
        cross = _dot((qr * qdec_ref[h]).astype(BF16), st.astype(BF16))
        kd = (kr * kdec_ref[h]).astype(BF16)
        state_ref[h] = st * chunk_decay[h] + _dot_tn(kd, v)
        y = inner + cross
        mu = jnp.mean(y, axis=-1, keepdims=True)
        yc = y - mu
        var = jnp.mean(yc * yc, axis=-1, keepdims=True)
        yn = yc * lax.rsqrt(var + LN_EPS)
        g = proj(OFF_G + h * RET_V_DIM, RET_V_DIM)
        yr = yn * (g * jax.nn.sigmoid(g))
        acc_ret = acc_ret + _dot(yr.astype(BF16), p_ret_ref[h * RET_V_DIM:(h + 1) * RET_V_DIM, :])

    u = _gelu_tanh(proj(OFF_U, SG_WIDTH))
    vn = _layer_norm(_gelu_tanh(proj(OFF_VS, SG_WIDTH)), sglng_ref[...], sglnb_ref[...])
    qi = lax.broadcasted_iota(jnp.int32, (SG_BLOCK, SG_BLOCK), 0) // SG_CHUNK
    pi = lax.broadcasted_iota(jnp.int32, (SG_BLOCK, SG_BLOCK), 1) // SG_CHUNK
    sg_mask = qi >= pi
    for gi in range(SG_GROUPS):
        wm = jnp.where(sg_mask, sgw_ref[gi], 0.0).astype(BF16)
        bias = sgbt_ref[:, gi:gi + 1]
        cols = slice(gi * SG_BLOCK, (gi + 1) * SG_BLOCK)
        for blk in range(ts // SG_BLOCK):
            rows = slice(blk * SG_BLOCK, (blk + 1) * SG_BLOCK)
            mixed = _dot(wm, vn[rows, cols].astype(BF16)) + bias
            ysg_ref[rows, cols] = (u[rows, cols] * mixed).astype(BF16)
    acc_sg = _dot(ysg_ref[...], p_sg_ref[...])

    ga = proj(OFF_GA, D_MODEL)
    gb = proj(OFF_GB, D_MODEL)
    hmix = jax.nn.sigmoid(ga) * acc_ret + jax.nn.sigmoid(gb) * acc_sg
    mix = _dot(hmix.astype(BF16), w_o_ref[...])
    x1 = _layer_norm(DN_ALPHA * xb + mix, ln1g_ref[...], ln1b_ref[...])
    x1_ref[...] = x1

    x_hi, x_lo = _split_bf16(x1)
    w_hi, w_lo = _split_bf16(rwt_ref[...])
    logits = (_dot_nt(w_hi, x_hi) + _dot_nt(w_hi, x_lo) + _dot_nt(w_lo, x_hi)) + rbb_ref[...]
    e_iota = lax.broadcasted_iota(jnp.int32, (N_EXPERTS, ts), 0).astype(F32)
    sels, vals = [], []
    work = logits
    for _ in range(TOP_K):
        m = jnp.max(work, axis=0, keepdims=True)
        idx = jnp.min(jnp.where(work == m, e_iota, float(N_EXPERTS)), axis=0, keepdims=True)
        sel = e_iota == idx
        work = jnp.where(sel, -jnp.inf, work)
        sels.append(sel)
        vals.append(m)
    exps = [jnp.exp(v - vals[0]) for v in vals]
    denom = exps[0] + exps[1] + exps[2] + exps[3]
    gate_ref[...] = jnp.concatenate([e / denom for e in exps], axis=0)

    onehots = [jnp.where(s, 1.0, 0.0) for s in sels]
    oh_sum = onehots[0] + onehots[1] + onehots[2] + onehots[3]
    t_row = lax.broadcasted_iota(jnp.int32, (ts, ts), 0)
    t_col = lax.broadcasted_iota(jnp.int32, (ts, ts), 1)
    upper = jnp.where(t_row < t_col, 1.0, 0.0).astype(BF16)
    before = _dot(oh_sum.astype(BF16), upper)
    cnt = jnp.sum(oh_sum, axis=1, keepdims=True)
    cnt_pad = jnp.floor((cnt + (PIECE - 1)) * (1.0 / PIECE)) * PIECE
    cnt_b = jnp.broadcast_to(cnt_pad, (N_EXPERTS, ts))
    e_row = lax.broadcasted_iota(jnp.int32, (N_EXPERTS, N_EXPERTS), 0)
    e_col = lax.broadcasted_iota(jnp.int32, (N_EXPERTS, N_EXPERTS), 1)
    lower = jnp.where(e_col < e_row, 1.0, 0.0).astype(BF16)
    strip_off = _dot(lower, cnt_b.astype(BF16))
    slot = strip_off + before
    pos = [jnp.sum(oh * slot, axis=0, keepdims=True).astype(jnp.int32) for oh in onehots]
    pos_ref[...] = jnp.concatenate(pos, axis=0)
    cnt_ref[0] = cnt_b[:, :128].astype(jnp.int32)
    r_iota = lax.broadcasted_iota(jnp.int32, (GROUP_ROWS, ts), 0)
    hit = (r_iota == pos[0]) | (r_iota == pos[1]) | (r_iota == pos[2]) | (r_iota == pos[3])
    perm = jnp.where(hit, 1.0, 0.0).astype(BF16)
    xs_ref[...] = _dot(perm, x_hi)


def _mixer_call(x2d, batch, seq, lw, tables):
    t = batch * seq
    ns = seq // TOK_BLOCK
    groups = t // TOK_BLOCK
    const = lambda *shape: pl.BlockSpec(shape, lambda b, j: (0,) * len(shape),
                                        pipeline_mode=pl.Buffered(1))
    tok = lambda width: pl.BlockSpec((TOK_BLOCK, width), lambda b, j: (b * ns + j, 0))
    seqtab = pl.BlockSpec((TOK_BLOCK, RET_QK_DIM), lambda b, j: (j, 0))
    lanes = lambda rows: pl.BlockSpec((rows, TOK_BLOCK), lambda b, j: (0, b * ns + j))
    in_specs = [
        tok(D_MODEL),
        const(D_MODEL, IN_WIDTH), const(RET_V, D_MODEL), const(SG_WIDTH, D_MODEL), const(D_MODEL, D_MODEL),
        const(SG_GROUPS, SG_BLOCK, SG_BLOCK), const(SG_BLOCK, SG_GROUPS),
        const(1, SG_WIDTH), const(1, SG_WIDTH), const(1, D_MODEL), const(1, D_MODEL),
        const(N_EXPERTS, D_MODEL), const(N_EXPERTS, TOK_BLOCK),
        seqtab, seqtab, seqtab, seqtab,
        const(RET_HEADS, TOK_BLOCK, TOK_BLOCK), const(RET_HEADS, TOK_BLOCK, RET_QK_DIM),
        const(RET_HEADS, TOK_BLOCK, RET_QK_DIM),
    ]
    out_shape = [
        jax.ShapeDtypeStruct((t, D_MODEL), F32),
        jax.ShapeDtypeStruct((TOP_K, t), jnp.int32),
        jax.ShapeDtypeStruct((TOP_K, t), F32),
        jax.ShapeDtypeStruct((groups, N_EXPERTS, 128), jnp.int32),
        jax.ShapeDtypeStruct((groups * GROUP_ROWS, D_MODEL), F32),
    ]
    out_specs = [
        tok(D_MODEL), lanes(TOP_K), lanes(TOP_K),
        pl.BlockSpec((1, N_EXPERTS, 128), lambda b, j: (b * ns + j, 0, 0)),
        pl.BlockSpec((GROUP_ROWS, D_MODEL), lambda b, j: (b * ns + j, 0)),
    ]
    kern = functools.partial(_mixer_kernel, chunk_decay=tables["chunk_decay"])
    return pl.pallas_call(
        kern,
        grid=(batch, ns),
        in_specs=in_specs,
        out_specs=out_specs,
        out_shape=out_shape,
        scratch_shapes=[pltpu.VMEM((RET_HEADS, RET_QK_DIM, RET_V_DIM), F32),
                        pltpu.VMEM((TOK_BLOCK, SG_WIDTH), BF16)],
        compiler_params=pltpu.CompilerParams(dimension_semantics=("arbitrary", "arbitrary"),
                                             vmem_limit_bytes=VMEM_LIMIT_V7X),
        name="mixer",
    )(x2d, lw["w_in"], lw["p_ret"], lw["p_sg"], lw["w_o"], lw["sg_w"], lw["sg_bt"],
      lw["sg_ln_g"], lw["sg_ln_b"], lw["ln1_g"], lw["ln1_b"], lw["router_wt"], lw["router_bb"],
      tables["cq"], tables["sq"], tables["ck"], tables["sk"],
      tables["dmat"], tables["qdec"], tables["kdec"])


def _expert_kernel(bexp_ref, src_ref, nvb_ref, xs_hbm, w1_ref, b1_ref, w2_ref, b2_ref,
                   yb_ref, xbuf, sem):
    del bexp_ref
    b = pl.program_id(0)
    nvb = nvb_ref[0]

    def gather(blk, slot):
        for i in range(EXP_PIECES):
            row = pl.multiple_of(src_ref[blk * EXP_PIECES + i] * PIECE, PIECE)
            pltpu.make_async_copy(xs_hbm.at[pl.ds(row, PIECE), :],
                                  xbuf.at[slot, pl.ds(i * PIECE, PIECE), :],
                                  sem.at[slot]).start()

    @pl.when(b == 0)
    def _():
        gather(0, 0)

    @pl.when(b + 1 < nvb)
    def _():
        gather(b + 1, (b + 1) % 2)

    @pl.when(b < nvb)
    def _():
        slot = b % 2
        pltpu.make_async_copy(xs_hbm.at[pl.ds(0, EXP_BLOCK), :], xbuf.at[slot], sem.at[slot]).wait()
        x = xbuf[slot].astype(BF16)
        y = jnp.zeros((EXP_BLOCK, D_MODEL), F32)
        for j in range(D_EXPERT // F_CHUNK):
            c0 = j * F_CHUNK
            hg = _dot(x, w1_ref[0, :, c0:c0 + F_CHUNK]) + b1_ref[0, :, c0:c0 + F_CHUNK]
            hu = (_dot(x, w1_ref[0, :, D_EXPERT + c0:D_EXPERT + c0 + F_CHUNK])
                  + b1_ref[0, :, D_EXPERT + c0:D_EXPERT + c0 + F_CHUNK])
            gate = jnp.minimum(hg, SWIGLU_LIMIT)
            up = jnp.clip(hu, -SWIGLU_LIMIT, SWIGLU_LIMIT)
            act = gate * jax.nn.sigmoid(SWIGLU_ALPHA * gate) * (up + 1.0)
            y = y + _dot(act.astype(BF16), w2_ref[0, c0:c0 + F_CHUNK, :])
        yb_ref[...] = y + b2_ref[0]

    @pl.when(b >= nvb)
    def _():
        yb_ref[...] = jnp.zeros_like(yb_ref)


def _expert_call(xs, block_exp, src_piece, nvb, lw, n_blocks):
    grid_spec = pltpu.PrefetchScalarGridSpec(
        num_scalar_prefetch=3,
        grid=(n_blocks,),
        in_specs=[
            pl.BlockSpec(memory_space=pl.ANY),
            pl.BlockSpec((1, D_MODEL, 2 * D_EXPERT), lambda b, be, sp, nv: (be[b], 0, 0)),
            pl.BlockSpec((1, 1, 2 * D_EXPERT), lambda b, be, sp, nv: (be[b], 0, 0)),
            pl.BlockSpec((1, D_EXPERT, D_MODEL), lambda b, be, sp, nv: (be[b], 0, 0)),
            pl.BlockSpec((1, 1, D_MODEL), lambda b, be, sp, nv: (be[b], 0, 0)),
        ],
        out_specs=pl.BlockSpec((EXP_BLOCK, D_MODEL), lambda b, be, sp, nv: (b, 0)),
        scratch_shapes=[pltpu.VMEM((2, EXP_BLOCK, D_MODEL), F32), pltpu.SemaphoreType.DMA((2,))],
    )
    return pl.pallas_call(
        _expert_kernel,
        grid_spec=grid_spec,
        out_shape=jax.ShapeDtypeStruct((n_blocks * EXP_BLOCK, D_MODEL), F32),
        compiler_params=pltpu.CompilerParams(dimension_semantics=("arbitrary",),
                                             vmem_limit_bytes=VMEM_LIMIT_V7X),
        name="experts",
    )(block_exp, src_piece, nvb, xs, lw["e_w1"], lw["e_b1"], lw["e_w2"], lw["e_b2"])


def _combine_kernel(yp_ref, x1_ref, yb_hbm, post_ref, gatet_ref, g_ref, b_ref, out_ref, ybuf, sem):
    g = pl.program_id(0)
    ng = pl.num_programs(0)

    def gather(grp, slot):
        for i in range(GROUP_PIECES):
            row = pl.multiple_of(yp_ref[grp * GROUP_PIECES + i] * PIECE, PIECE)
            pltpu.make_async_copy(yb_hbm.at[pl.ds(row, PIECE), :],
                                  ybuf.at[slot, pl.ds(i * PIECE, PIECE), :],
                                  sem.at[slot]).start()

    @pl.when(g == 0)
    def _():
        gather(0, 0)

    @pl.when(g + 1 < ng)
    def _():
        gather(g + 1, (g + 1) % 2)

    slot = g % 2
    pltpu.make_async_copy(yb_hbm.at[pl.ds(0, GROUP_ROWS), :], ybuf.at[slot], sem.at[slot]).wait()
    c_iota = lax.broadcasted_iota(jnp.int32, (TOK_BLOCK, GROUP_ROWS), 1)
    post = post_ref[...]
    gatet = gatet_ref[...]
    unsort = jnp.zeros((TOK_BLOCK, GROUP_ROWS), F32)
    for k in range(TOP_K):
        unsort = unsort + jnp.where(c_iota == post[:, k:k + 1], gatet[:, k:k + 1], 0.0)
    ffn = _dot(unsort.astype(BF16), ybuf[slot].astype(BF16))
    out_ref[...] = _layer_norm(DN_ALPHA * x1_ref[...] + ffn, g_ref[...], b_ref[...])


def _combine_call(x1, yb, ypiece, post, gatet, ln_g, ln_b):
    t = x1.shape[0]
    groups = t // TOK_BLOCK
    grid_spec = pltpu.PrefetchScalarGridSpec(
        num_scalar_prefetch=1,
        grid=(groups,),
        in_specs=[
            pl.BlockSpec((TOK_BLOCK, D_MODEL), lambda g, yp: (g, 0)),
            pl.BlockSpec(memory_space=pl.ANY),
            pl.BlockSpec((TOK_BLOCK, TOP_K), lambda g, yp: (g, 0)),
            pl.BlockSpec((TOK_BLOCK, TOP_K), lambda g, yp: (g, 0)),
            pl.BlockSpec((1, D_MODEL), lambda g, yp: (0, 0)),
            pl.BlockSpec((1, D_MODEL), lambda g, yp: (0, 0)),
        ],
        out_specs=pl.BlockSpec((TOK_BLOCK, D_MODEL), lambda g, yp: (g, 0)),
        scratch_shapes=[pltpu.VMEM((2, GROUP_ROWS, D_MODEL), F32), pltpu.SemaphoreType.DMA((2,))],
    )
    return pl.pallas_call(
        _combine_kernel,
        grid_spec=grid_spec,
        out_shape=jax.ShapeDtypeStruct((t, D_MODEL), F32),
        compiler_params=pltpu.CompilerParams(dimension_semantics=("arbitrary",),
                                             vmem_limit_bytes=VMEM_LIMIT_V7X),
        name="combine",
    )(ypiece, x1, yb, post, gatet, ln_g, ln_b)


def _routing_tables(cnt_pad, n_blocks):
    groups = cnt_pad.shape[0]
    npc = cnt_pad // PIECE
    strip_start = jnp.cumsum(npc, axis=1) - npc
    tot = jnp.sum(npc, axis=0)
    tot_pad = (tot + EXP_PIECES - 1) // EXP_PIECES * EXP_PIECES
    exp_end = jnp.cumsum(tot_pad)
    exp_start = exp_end - tot_pad
    dstart = exp_start[None, :] + jnp.cumsum(npc, axis=0) - npc
    nvb = (exp_end[-1] // EXP_PIECES).astype(jnp.int32)

    blocks = jnp.arange(n_blocks, dtype=jnp.int32)
    bexp = jnp.sum(blocks[:, None] * EXP_PIECES >= exp_end[None, :], axis=1).astype(jnp.int32)
    bexp = jnp.minimum(bexp, N_EXPERTS - 1)
    last = jnp.sum((nvb - 1) * EXP_PIECES >= exp_end).astype(jnp.int32)
    bexp = jnp.where(blocks < nvb, bexp, jnp.minimum(last, N_EXPERTS - 1))

    d = jnp.arange(n_blocks * EXP_PIECES, dtype=jnp.int32)
    e_of = jnp.minimum(jnp.sum(d[:, None] >= exp_end[None, :], axis=1), N_EXPERTS - 1)
    ds_e = jnp.take(dstart.T, e_of, axis=0)
    g_of = jnp.maximum(jnp.sum(ds_e <= d[:, None], axis=1) - 1, 0)
    flat = g_of * N_EXPERTS + e_of
    i_of = d - jnp.take(dstart.reshape(-1), flat)
    valid = (i_of >= 0) & (i_of < jnp.take(npc.reshape(-1), flat))
    src = g_of * GROUP_PIECES + jnp.take(strip_start.reshape(-1), flat) + i_of
    src_piece = jnp.where(valid, src, 0).astype(jnp.int32)

    j = jnp.arange(GROUP_PIECES, dtype=jnp.int32)
    e_loc = jnp.maximum(jnp.sum(strip_start[:, None, :] <= j[None, :, None], axis=2) - 1, 0)
    st_loc = jnp.take_along_axis(strip_start, e_loc, axis=1)
    np_loc = jnp.take_along_axis(npc, e_loc, axis=1)
    ds_loc = jnp.take_along_axis(dstart, e_loc, axis=1)
    i_loc = j[None, :] - st_loc
    ypiece = jnp.where(i_loc < np_loc, ds_loc + i_loc, 0).astype(jnp.int32).reshape(groups * GROUP_PIECES)
    return bexp, src_piece, nvb.reshape(1), ypiece


def _tables(seq):
    half = RET_QK_DIM // 2
    inv = ROPE_BASE ** (-jnp.arange(half, dtype=F32) / half)
    ang = jnp.arange(seq, dtype=jnp.int32).astype(F32)[:, None] * inv[None, :]
    cos, sin = jnp.cos(ang), jnp.sin(ang)
    cosf = jnp.concatenate([cos, cos], axis=1)
    sinf = jnp.concatenate([-sin, sin], axis=1)
    kscale = RET_QK_DIM ** -0.5
    log_g = jnp.log(1.0 - jnp.exp(jnp.linspace(math.log(1.0 / 32), math.log(1.0 / 512), RET_HEADS)))
    idx = jnp.arange(TOK_BLOCK, dtype=F32)
    diff = idx[:, None] - idx[None, :]
    dmat = jnp.where(diff >= 0, jnp.exp(log_g[:, None, None] * jnp.maximum(diff, 0.0)), 0.0)
    qdec = jnp.exp(log_g[:, None] * (idx[None, :] + 1.0))
    kdec = jnp.exp(log_g[:, None] * (TOK_BLOCK - 1.0 - idx[None, :]))
    bc = lambda a: jnp.broadcast_to(a[:, :, None], (RET_HEADS, TOK_BLOCK, RET_QK_DIM))
    log_g_host = np.log(1.0 - np.exp(np.linspace(math.log(1.0 / 32), math.log(1.0 / 512), RET_HEADS)))
    chunk_decay = tuple(float(np.float32(np.exp(np.float32(lg) * np.float32(TOK_BLOCK)))) for lg in log_g_host)
    return {"cq": cosf, "sq": sinf, "ck": cosf * kscale, "sk": sinf * kscale,
            "dmat": dmat.astype(F32), "qdec": bc(qdec), "kdec": bc(kdec), "chunk_decay": chunk_decay}


def kernel(x, w_in, p_ret, sg_ln_g, sg_ln_b, sg_w, sg_b, p_sg, w_o, ln1_g, ln1_b,
           router_w, router_b, e_w1, e_b1, e_w2, e_b2, ln2_g, ln2_b):
    batch, seq, d = x.shape
    assert d == D_MODEL and seq % TOK_BLOCK == 0
    depth = w_in.shape[0]
    t = batch * seq
    groups = t // TOK_BLOCK
    n_blocks = (groups * GROUP_PIECES + N_EXPERTS * (EXP_PIECES - 1)) // EXP_PIECES + 1
    tables = _tables(seq)
    xc = x.reshape(t, D_MODEL)
    for l in range(depth):
        lw = {
            "w_in": w_in[l].astype(BF16), "p_ret": p_ret[l].astype(BF16), "p_sg": p_sg[l].astype(BF16),
            "w_o": w_o[l].astype(BF16), "sg_w": sg_w[l], "sg_bt": sg_b[l].T,
            "sg_ln_g": sg_ln_g[l][None], "sg_ln_b": sg_ln_b[l][None],
            "ln1_g": ln1_g[l][None], "ln1_b": ln1_b[l][None],
            "router_wt": router_w[l].T,
            "router_bb": jnp.broadcast_to(router_b[l][:, None], (N_EXPERTS, TOK_BLOCK)),
            "e_w1": e_w1[l].astype(BF16), "e_b1": e_b1[l][:, None, :],
            "e_w2": e_w2[l].astype(BF16), "e_b2": e_b2[l][:, None, :],
        }
        x1, pos, gates, cnt, xs = _mixer_call(xc, batch, seq, lw, tables)
        bexp, src_piece, nvb, ypiece = _routing_tables(cnt[:, :, 0], n_blocks)
        yb = _expert_call(xs, bexp, src_piece, nvb, lw, n_blocks)
        xc = _combine_call(x1, yb, ypiece, pos.T, gates.T, ln2_g[l][None], ln2_b[l][None])
    return xc.reshape(batch, seq, D_MODEL)
```

```python
import functools
import math

import jax
import jax.numpy as jnp
import numpy as np
from jax import lax
from jax.experimental import pallas as pl
from jax.experimental.pallas import tpu as pltpu

F32 = jnp.float32
BF16 = jnp.bfloat16

D_MODEL = 1024
DEPTH = 4
RET_HEADS = 4
RET_QK_DIM = 128
RET_V_DIM = 256
RET_QK = RET_HEADS * RET_QK_DIM
RET_V = RET_HEADS * RET_V_DIM
ROPE_BASE = 10000.0
SG_BLOCK = 128
SG_CHUNK = 64
SG_GROUPS = 8
SG_WIDTH = D_MODEL
N_EXPERTS = 32
TOP_K = 4
D_EXPERT = D_MODEL
SWIGLU_LIMIT = 7.0
SWIGLU_ALPHA = 1.702
LN_EPS = 1e-5
DN_ALPHA = (2 * DEPTH) ** 0.25

OFF_Q = 0
OFF_K = OFF_Q + RET_QK
OFF_V = OFF_K + RET_QK
OFF_G = OFF_V + RET_V
OFF_U = OFF_G + RET_V
OFF_VS = OFF_U + SG_WIDTH
OFF_GA = OFF_VS + SG_WIDTH
OFF_GB = OFF_GA + D_MODEL
IN_WIDTH = OFF_GB + D_MODEL

SUBLANES_V7X = 8
TOK_BLOCK = 256
PIECE = SUBLANES_V7X
GROUP_ROWS = TOK_BLOCK * TOP_K + N_EXPERTS * PIECE
GROUP_PIECES = GROUP_ROWS // PIECE
EXP_BLOCK = 256
EXP_PIECES = EXP_BLOCK // PIECE
F_CHUNK = 256
W_CAST_ROWS = 128
VMEM_LIMIT_V7X = 56 * 1024 * 1024


def _layer_norm(x, g, b):
    mu = jnp.mean(x, axis=-1, keepdims=True)
    xc = x - mu
    var = jnp.mean(xc * xc, axis=-1, keepdims=True)
    return xc * lax.rsqrt(var + LN_EPS) * g + b


def _gelu_tanh(x):
    c = math.sqrt(2.0 / math.pi)
    return 0.5 * x * (1.0 + jnp.tanh(c * (x + 0.044715 * (x * x * x))))


def _dot(a, b):
    return jnp.dot(a, b, preferred_element_type=F32)


def _dot_nt(a, b):
    return lax.dot_general(a, b, (((1,), (1,)), ((), ())), preferred_element_type=F32)


def _dot_tn(a, b):
    return lax.dot_general(a, b, (((0,), (0,)), ((), ())), preferred_element_type=F32)


def _split_bf16(x):
    hi = x.astype(BF16)
    lo = (x - hi.astype(F32)).astype(BF16)
    return hi, lo


def _mixer_kernel(x_ref, w_in_ref, p_ret_ref, p_sg_ref, w_o_ref, sgw_ref, sgbt_ref,
                  sglng_ref, sglnb_ref, ln1g_ref, ln1b_ref, rwt_ref, rbb_ref,
                  cq_ref, sq_ref, ck_ref, sk_ref, dmat_ref, qdec_ref, kdec_ref,
                  x1_ref, pos_ref, gate_ref, cnt_ref, xs_ref,
                  state_ref, ysg_ref, *, chunk_decay):
    ts = TOK_BLOCK

    @pl.when(pl.program_id(1) == 0)
    def _():
        state_ref[...] = jnp.zeros_like(state_ref)

    xb = x_ref[...]
    xbf = xb.astype(BF16)

    def proj(off, width):
        return _dot(xbf, w_in_ref[0, :, off:off + width])

    cq, sq, ck, sk = cq_ref[...], sq_ref[...], ck_ref[...], sk_ref[...]
    acc_ret = jnp.zeros((ts, D_MODEL), F32)
    for h in range(RET_HEADS):
        q = proj(OFF_Q + h * RET_QK_DIM, RET_QK_DIM)
        k = proj(OFF_K + h * RET_QK_DIM, RET_QK_DIM)
        v = proj(OFF_V + h * RET_V_DIM, RET_V_DIM).astype(BF16)
        qr = q * cq + pltpu.roll(q, RET_QK_DIM // 2, 1) * sq
        kr = k * ck + pltpu.roll(k, RET_QK_DIM // 2, 1) * sk
        scores = _dot_nt(qr.astype(BF16), kr.astype(BF16)) * dmat_ref[h]
        inner = _dot(scores.astype(BF16), v)
        st = state_ref[h]
        cross = _dot((qr * qdec_ref[h]).astype(BF16), st.astype(BF16))
        kd = (kr * kdec_ref[h]).astype(BF16)
        state_ref[h] = st * chunk_decay[h] + _dot_tn(kd, v)
        y = inner + cross
        mu = jnp.mean(y, axis=-1, keepdims=True)
        yc = y - mu
        var = jnp.mean(yc * yc, axis=-1, keepdims=True)
        yn = yc * lax.rsqrt(var + LN_EPS)
        g = proj(OFF_G + h * RET_V_DIM, RET_V_DIM)
        yr = yn * (g * jax.nn.sigmoid(g))
        acc_ret = acc_ret + _dot(yr.astype(BF16), p_ret_ref[0, h * RET_V_DIM:(h + 1) * RET_V_DIM, :])

    u = _gelu_tanh(proj(OFF_U, SG_WIDTH))
    vn = _layer_norm(_gelu_tanh(proj(OFF_VS, SG_WIDTH)), sglng_ref[...], sglnb_ref[...])
    qi = lax.broadcasted_iota(jnp.int32, (SG_BLOCK, SG_BLOCK), 0) // SG_CHUNK
    pi = lax.broadcasted_iota(jnp.int32, (SG_BLOCK, SG_BLOCK), 1) // SG_CHUNK
    sg_mask = qi >= pi
    for gi in range(SG_GROUPS):
        wm = jnp.where(sg_mask, sgw_ref[gi], 0.0).astype(BF16)
        bias = sgbt_ref[:, gi:gi + 1]
        cols = slice(gi * SG_BLOCK, (gi + 1) * SG_BLOCK)
        for blk in range(ts // SG_BLOCK):
            rows = slice(blk * SG_BLOCK, (blk + 1) * SG_BLOCK)
            mixed = _dot(wm, vn[rows, cols].astype(BF16)) + bias
            ysg_ref[rows, cols] = (u[rows, cols] * mixed).astype(BF16)
    acc_sg = _dot(ysg_ref[...], p_sg_ref[0])

    ga = proj(OFF_GA, D_MODEL)
    gb = proj(OFF_GB, D_MODEL)
    hmix = jax.nn.sigmoid(ga) * acc_ret + jax.nn.sigmoid(gb) * acc_sg
    mix = _dot(hmix.astype(BF16), w_o_ref[0])
    x1 = _layer_norm(DN_ALPHA * xb + mix, ln1g_ref[...], ln1b_ref[...])
    x1_ref[...] = x1

    x_hi, x_lo = _split_bf16(x1)
    w_hi, w_lo = _split_bf16(rwt_ref[...])
    logits = (_dot_nt(w_hi, x_hi) + _dot_nt(w_hi, x_lo) + _dot_nt(w_lo, x_hi)) + rbb_ref[...]
    e_iota = lax.broadcasted_iota(jnp.int32, (N_EXPERTS, ts), 0).astype(F32)
    sels, vals = [], []
    work = logits
    for _ in range(TOP_K):
        m = jnp.max(work, axis=0, keepdims=True)
        idx = jnp.min(jnp.where(work == m, e_iota, float(N_EXPERTS)), axis=0, keepdims=True)
        sel = e_iota == idx
        work = jnp.where(sel, -jnp.inf, work)
        sels.append(sel)
        vals.append(m)
    exps = [jnp.exp(v - vals[0]) for v in vals]
    denom = exps[0] + exps[1] + exps[2] + exps[3]
    gate_ref[...] = jnp.concatenate([e / denom for e in exps], axis=0)

    onehots = [jnp.where(s, 1.0, 0.0) for s in sels]
    oh_sum = onehots[0] + onehots[1] + onehots[2] + onehots[3]
    t_row = lax.broadcasted_iota(jnp.int32, (ts, ts), 0)
    t_col = lax.broadcasted_iota(jnp.int32, (ts, ts), 1)
    upper = jnp.where(t_row < t_col, 1.0, 0.0).astype(BF16)
    before = _dot(oh_sum.astype(BF16), upper)
    cnt = jnp.sum(oh_sum, axis=1, keepdims=True)
    cnt_pad = jnp.floor((cnt + (PIECE - 1)) * (1.0 / PIECE)) * PIECE
    cnt_b = jnp.broadcast_to(cnt_pad, (N_EXPERTS, ts))
    e_row = lax.broadcasted_iota(jnp.int32, (N_EXPERTS, N_EXPERTS), 0)
    e_col = lax.broadcasted_iota(jnp.int32, (N_EXPERTS, N_EXPERTS), 1)
    lower = jnp.where(e_col < e_row, 1.0, 0.0).astype(BF16)
    strip_off = _dot(lower, cnt_b.astype(BF16))
    slot = strip_off + before
    pos = [jnp.sum(oh * slot, axis=0, keepdims=True).astype(jnp.int32) for oh in onehots]
    pos_ref[...] = jnp.concatenate(pos, axis=0)
    cnt_ref[0] = cnt_b[:, :128].astype(jnp.int32)
    r_iota = lax.broadcasted_iota(jnp.int32, (GROUP_ROWS, ts), 0)
    hit = (r_iota == pos[0]) | (r_iota == pos[1]) | (r_iota == pos[2]) | (r_iota == pos[3])
    perm = jnp.where(hit, 1.0, 0.0).astype(BF16)
    xs_ref[...] = _dot(perm, x_hi)


def _mixer_call(x2d, batch, seq, layer, lw, tables):
    t = batch * seq
    ns = seq // TOK_BLOCK
    groups = t // TOK_BLOCK
    const = lambda *shape: pl.BlockSpec(shape, lambda b, j: (0,) * len(shape),
                                        pipeline_mode=pl.Buffered(1))
    stacked = lambda *shape: pl.BlockSpec((1,) + shape, lambda b, j: (layer, 0, 0),
                                          pipeline_mode=pl.Buffered(1))
    tok = lambda width: pl.BlockSpec((TOK_BLOCK, width), lambda b, j: (b * ns + j, 0))
    seqtab = pl.BlockSpec((TOK_BLOCK, RET_QK_DIM), lambda b, j: (j, 0))
    lanes = lambda rows: pl.BlockSpec((rows, TOK_BLOCK), lambda b, j: (0, b * ns + j))
    in_specs = [
        tok(D_MODEL),
        stacked(D_MODEL, IN_WIDTH), stacked(RET_V, D_MODEL), stacked(SG_WIDTH, D_MODEL),
        stacked(D_MODEL, D_MODEL),
        const(SG_GROUPS, SG_BLOCK, SG_BLOCK), const(SG_BLOCK, SG_GROUPS),
        const(1, SG_WIDTH), const(1, SG_WIDTH), const(1, D_MODEL), const(1, D_MODEL),
        const(N_EXPERTS, D_MODEL), const(N_EXPERTS, TOK_BLOCK),
        seqtab, seqtab, seqtab, seqtab,
        const(RET_HEADS, TOK_BLOCK, TOK_BLOCK), const(RET_HEADS, TOK_BLOCK, RET_QK_DIM),
        const(RET_HEADS, TOK_BLOCK, RET_QK_DIM),
    ]
    out_shape = [
        jax.ShapeDtypeStruct((t, D_MODEL), F32),
        jax.ShapeDtypeStruct((TOP_K, t), jnp.int32),
        jax.ShapeDtypeStruct((TOP_K, t), F32),
        jax.ShapeDtypeStruct((groups, N_EXPERTS, 128), jnp.int32),
        jax.ShapeDtypeStruct((groups * GROUP_ROWS, D_MODEL), F32),
    ]
    out_specs = [
        tok(D_MODEL), lanes(TOP_K), lanes(TOP_K),
        pl.BlockSpec((1, N_EXPERTS, 128), lambda b, j: (b * ns + j, 0, 0)),
        pl.BlockSpec((GROUP_ROWS, D_MODEL), lambda b, j: (b * ns + j, 0)),
    ]
    kern = functools.partial(_mixer_kernel, chunk_decay=tables["chunk_decay"])
    return pl.pallas_call(
        kern,
        grid=(batch, ns),
        in_specs=in_specs,
        out_specs=out_specs,
        out_shape=out_shape,
        scratch_shapes=[pltpu.VMEM((RET_HEADS, RET_QK_DIM, RET_V_DIM), F32),
                        pltpu.VMEM((TOK_BLOCK, SG_WIDTH), BF16)],
        compiler_params=pltpu.CompilerParams(dimension_semantics=("arbitrary", "arbitrary"),
                                             vmem_limit_bytes=VMEM_LIMIT_V7X),
        name="mixer",
    )(x2d, lw["w_in"], lw["p_ret"], lw["p_sg"], lw["w_o"], lw["sg_w"], lw["sg_bt"],
      lw["sg_ln_g"], lw["sg_ln_b"], lw["ln1_g"], lw["ln1_b"], lw["router_wt"], lw["router_bb"],
      tables["cq"], tables["sq"], tables["ck"], tables["sk"],
      tables["dmat"], tables["qdec"], tables["kdec"])


def _expert_kernel(bexp_ref, src_ref, nvb_ref, xs_hbm, w1_ref, b1_ref, w2_ref, b2_ref,
                   yb_ref, xbuf, w1b, w2b, sem):
    b = pl.program_id(0)
    nvb = nvb_ref[0]

    def gather(blk, slot):
        for i in range(EXP_PIECES):
            row = pl.multiple_of(src_ref[blk * EXP_PIECES + i] * PIECE, PIECE)
            pltpu.make_async_copy(xs_hbm.at[pl.ds(row, PIECE), :],
                                  xbuf.at[slot, pl.ds(i * PIECE, PIECE), :],
                                  sem.at[slot]).start()

    def wait(slot):
        pltpu.make_async_copy(xs_hbm.at[pl.ds(0, EXP_BLOCK), :], xbuf.at[slot], sem.at[slot]).wait()

    @pl.when(b == 0)
    def _():
        gather(0, 0)

    new_expert = (b == 0) | (bexp_ref[b] != bexp_ref[jnp.maximum(b - 1, 0)])

    @pl.when(new_expert & (b < nvb))
    def _():
        def cast_rows(i, carry):
            r = pl.multiple_of(i * W_CAST_ROWS, W_CAST_ROWS)
            w1b[pl.ds(r, W_CAST_ROWS), :] = w1_ref[0, 0, pl.ds(r, W_CAST_ROWS), :].astype(BF16)
            w2b[pl.ds(r, W_CAST_ROWS), :] = w2_ref[0, 0, pl.ds(r, W_CAST_ROWS), :].astype(BF16)
            return carry
        lax.fori_loop(0, D_MODEL // W_CAST_ROWS, cast_rows, 0)

    @pl.when(b < nvb)
    def _():
        slot = b % 2
        wait(slot)
        gather(jnp.minimum(b + 1, nvb - 1), 1 - slot)
        x = xbuf[slot].astype(BF16)
        y = jnp.zeros((EXP_BLOCK, D_MODEL), F32)
        for j in range(D_EXPERT // F_CHUNK):
            c0 = j * F_CHUNK
            hg = _dot(x, w1b[:, c0:c0 + F_CHUNK]) + b1_ref[0, 0, :, c0:c0 + F_CHUNK]
            hu = (_dot(x, w1b[:, D_EXPERT + c0:D_EXPERT + c0 + F_CHUNK])
                  + b1_ref[0, 0, :, D_EXPERT + c0:D_EXPERT + c0 + F_CHUNK])
            gate = jnp.minimum(hg, SWIGLU_LIMIT)
            up = jnp.clip(hu, -SWIGLU_LIMIT, SWIGLU_LIMIT)
            act = gate * jax.nn.sigmoid(SWIGLU_ALPHA * gate) * (up + 1.0)
            y = y + _dot(act.astype(BF16), w2b[c0:c0 + F_CHUNK, :])
        yb_ref[...] = y + b2_ref[0, 0]

    @pl.when(b == nvb - 1)
    def _():
        wait(1 - b % 2)

    @pl.when(b >= nvb)
    def _():
        yb_ref[...] = jnp.zeros_like(yb_ref)


def _expert_call(xs, block_exp, src_piece, nvb, layer, e_w1, e_b1, e_w2, e_b2, n_blocks):
    wspec = lambda *shape: pl.BlockSpec((1, 1) + shape, lambda b, be, sp, nv: (layer, be[b], 0, 0))
    grid_spec = pltpu.PrefetchScalarGridSpec(
        num_scalar_prefetch=3,
        grid=(n_blocks,),
        in_specs=[
            pl.BlockSpec(memory_space=pl.ANY),
            wspec(D_MODEL, 2 * D_EXPERT), wspec(1, 2 * D_EXPERT),
            wspec(D_EXPERT, D_MODEL), wspec(1, D_MODEL),
        ],
        out_specs=pl.BlockSpec((EXP_BLOCK, D_MODEL), lambda b, be, sp, nv: (b, 0)),
        scratch_shapes=[pltpu.VMEM((2, EXP_BLOCK, D_MODEL), F32),
                        pltpu.VMEM((D_MODEL, 2 * D_EXPERT), BF16),
                        pltpu.VMEM((D_EXPERT, D_MODEL), BF16),
                        pltpu.SemaphoreType.DMA((2,))],
    )
    return pl.pallas_call(
        _expert_kernel,
        grid_spec=grid_spec,
        out_shape=jax.ShapeDtypeStruct((n_blocks * EXP_BLOCK, D_MODEL), F32),
        compiler_params=pltpu.CompilerParams(dimension_semantics=("arbitrary",),
                                             vmem_limit_bytes=VMEM_LIMIT_V7X),
        name="experts",
    )(block_exp, src_piece, nvb, xs, e_w1, e_b1, e_w2, e_b2)


def _combine_kernel(yp_ref, x1_ref, yb_hbm, post_ref, gatet_ref, g_ref, b_ref, out_ref, ybuf, sem):
    g = pl.program_id(0)
    ng = pl.num_programs(0)

    def gather(grp, slot):
        for i in range(GROUP_PIECES):
            row = pl.multiple_of(yp_ref[grp * GROUP_PIECES + i] * PIECE, PIECE)
            pltpu.make_async_copy(yb_hbm.at[pl.ds(row, PIECE), :],
                                  ybuf.at[slot, pl.ds(i * PIECE, PIECE), :],
                                  sem.at[slot]).start()

    @pl.when(g == 0)
    def _():
        gather(0, 0)

    @pl.when(g + 1 < ng)
    def _():
        gather(g + 1, (g + 1) % 2)

    slot = g % 2
    pltpu.make_async_copy(yb_hbm.at[pl.ds(0, GROUP_ROWS), :], ybuf.at[slot], sem.at[slot]).wait()
    c_iota = lax.broadcasted_iota(jnp.int32, (TOK_BLOCK, GROUP_ROWS), 1)
    post = post_ref[...]
    gatet = gatet_ref[...]
    unsort = jnp.zeros((TOK_BLOCK, GROUP_ROWS), F32)
    for k in range(TOP_K):
        unsort = unsort + jnp.where(c_iota == post[:, k:k + 1], gatet[:, k:k + 1], 0.0)
    ffn = _dot(unsort.astype(BF16), ybuf[slot].astype(BF16))
    out_ref[...] = _layer_norm(DN_ALPHA * x1_ref[...] + ffn, g_ref[...], b_ref[...])


def _combine_call(x1, yb, ypiece, post, gatet, ln_g, ln_b):
    t = x1.shape[0]
    groups = t // TOK_BLOCK
    grid_spec = pltpu.PrefetchScalarGridSpec(
        num_scalar_prefetch=1,
        grid=(groups,),
        in_specs=[
            pl.BlockSpec((TOK_BLOCK, D_MODEL), lambda g, yp: (g, 0)),
            pl.BlockSpec(memory_space=pl.ANY),
            pl.BlockSpec((TOK_BLOCK, TOP_K), lambda g, yp: (g, 0)),
            pl.BlockSpec((TOK_BLOCK, TOP_K), lambda g, yp: (g, 0)),
            pl.BlockSpec((1, D_MODEL), lambda g, yp: (0, 0)),
            pl.BlockSpec((1, D_MODEL), lambda g, yp: (0, 0)),
        ],
        out_specs=pl.BlockSpec((TOK_BLOCK, D_MODEL), lambda g, yp: (g, 0)),
        scratch_shapes=[pltpu.VMEM((2, GROUP_ROWS, D_MODEL), F32), pltpu.SemaphoreType.DMA((2,))],
    )
    return pl.pallas_call(
        _combine_kernel,
        grid_spec=grid_spec,
        out_shape=jax.ShapeDtypeStruct((t, D_MODEL), F32),
        compiler_params=pltpu.CompilerParams(dimension_semantics=("arbitrary",),
                                             vmem_limit_bytes=VMEM_LIMIT_V7X),
        name="combine",
    )(ypiece, x1, yb, post, gatet, ln_g, ln_b)


def _routing_tables(cnt_pad, n_blocks):
    groups = cnt_pad.shape[0]
    npc = cnt_pad // PIECE
    strip_start = jnp.cumsum(npc, axis=1) - npc
    tot = jnp.sum(npc, axis=0)
    tot_pad = (tot + EXP_PIECES - 1) // EXP_PIECES * EXP_PIECES
    exp_end = jnp.cumsum(tot_pad)
    exp_start = exp_end - tot_pad
    dstart = exp_start[None, :] + jnp.cumsum(npc, axis=0) - npc
    nvb = (exp_end[-1] // EXP_PIECES).astype(jnp.int32)

    blocks = jnp.arange(n_blocks, dtype=jnp.int32)
    bexp = jnp.sum(blocks[:, None] * EXP_PIECES >= exp_end[None, :], axis=1).astype(jnp.int32)
    bexp = jnp.minimum(bexp, N_EXPERTS - 1)
    last = jnp.sum((nvb - 1) * EXP_PIECES >= exp_end).astype(jnp.int32)
    bexp = jnp.where(blocks < nvb, bexp, jnp.minimum(last, N_EXPERTS - 1))

    d = jnp.arange(n_blocks * EXP_PIECES, dtype=jnp.int32)
    e_of = jnp.minimum(jnp.sum(d[:, None] >= exp_end[None, :], axis=1), N_EXPERTS - 1)
    oh_e = (e_of[:, None] == jnp.arange(N_EXPERTS, dtype=jnp.int32)[None, :]).astype(F32)
    row_of = lambda tab: jnp.dot(oh_e, tab.T.astype(F32), precision=lax.Precision.HIGHEST).astype(jnp.int32)
    ds_e = row_of(dstart)
    g_of = jnp.maximum(jnp.sum(ds_e <= d[:, None], axis=1) - 1, 0)
    at_g = g_of[:, None] == jnp.arange(groups, dtype=jnp.int32)[None, :]
    pick = lambda rows: jnp.sum(jnp.where(at_g, rows, 0), axis=1)
    i_of = d - pick(ds_e)
    valid = (i_of >= 0) & (i_of < pick(row_of(npc)))
    src = g_of * GROUP_PIECES + pick(row_of(strip_start)) + i_of
    src_piece = jnp.where(valid, src, 0).astype(jnp.int32)

    j = jnp.arange(GROUP_PIECES, dtype=jnp.int32)
    e_loc = jnp.maximum(jnp.sum(strip_start[:, None, :] <= j[None, :, None], axis=2) - 1, 0)
    at_e = e_loc[:, :, None] == jnp.arange(N_EXPERTS, dtype=jnp.int32)[None, None, :]
    pick_e = lambda tab: jnp.sum(jnp.where(at_e, tab[:, None, :], 0), axis=2)
    st_loc = pick_e(strip_start)
    np_loc = pick_e(npc)
    ds_loc = pick_e(dstart)
    i_loc = j[None, :] - st_loc
    ypiece = jnp.where(i_loc < np_loc, ds_loc + i_loc, 0).astype(jnp.int32).reshape(groups * GROUP_PIECES)
    return bexp, src_piece, nvb.reshape(1), ypiece


def _tables(seq):
    half = RET_QK_DIM // 2
    inv = ROPE_BASE ** (-jnp.arange(half, dtype=F32) / half)
    ang = jnp.arange(seq, dtype=jnp.int32).astype(F32)[:, None] * inv[None, :]
    cos, sin = jnp.cos(ang), jnp.sin(ang)
    cosf = jnp.concatenate([cos, cos], axis=1)
    sinf = jnp.concatenate([-sin, sin], axis=1)
    kscale = RET_QK_DIM ** -0.5
    log_g = jnp.log(1.0 - jnp.exp(jnp.linspace(math.log(1.0 / 32), math.log(1.0 / 512), RET_HEADS)))
    idx = jnp.arange(TOK_BLOCK, dtype=F32)
    diff = idx[:, None] - idx[None, :]
    dmat = jnp.where(diff >= 0, jnp.exp(log_g[:, None, None] * jnp.maximum(diff, 0.0)), 0.0)
    qdec = jnp.exp(log_g[:, None] * (idx[None, :] + 1.0))
    kdec = jnp.exp(log_g[:, None] * (TOK_BLOCK - 1.0 - idx[None, :]))
    bc = lambda a: jnp.broadcast_to(a[:, :, None], (RET_HEADS, TOK_BLOCK, RET_QK_DIM))
    log_g_host = np.log(1.0 - np.exp(np.linspace(math.log(1.0 / 32), math.log(1.0 / 512), RET_HEADS)))
    chunk_decay = tuple(float(np.float32(np.exp(np.float32(lg) * np.float32(TOK_BLOCK)))) for lg in log_g_host)
    return {"cq": cosf, "sq": sinf, "ck": cosf * kscale, "sk": sinf * kscale,
            "dmat": dmat.astype(F32), "qdec": bc(qdec), "kdec": bc(kdec), "chunk_decay": chunk_decay}


def kernel(x, w_in, p_ret, sg_ln_g, sg_ln_b, sg_w, sg_b, p_sg, w_o, ln1_g, ln1_b,
           router_w, router_b, e_w1, e_b1, e_w2, e_b2, ln2_g, ln2_b):
    batch, seq, d = x.shape
    assert d == D_MODEL and seq % TOK_BLOCK == 0
    depth = w_in.shape[0]
    t = batch * seq
    groups = t // TOK_BLOCK
    n_blocks = (groups * GROUP_PIECES + N_EXPERTS * (EXP_PIECES - 1)) // EXP_PIECES + 1
    tables = _tables(seq)
    xc = x.reshape(t, D_MODEL)
    w_in_b, p_ret_b, p_sg_b, w_o_b = (w.astype(BF16) for w in (w_in, p_ret, p_sg, w_o))
    e_b1r = e_b1.reshape(depth, N_EXPERTS, 1, 2 * D_EXPERT)
    e_b2r = e_b2.reshape(depth, N_EXPERTS, 1, D_MODEL)
    for l in range(depth):
        lw = {
            "w_in": w_in_b, "p_ret": p_ret_b, "p_sg": p_sg_b,
            "w_o": w_o_b, "sg_w": sg_w[l], "sg_bt": sg_b[l].T,
            "sg_ln_g": sg_ln_g[l][None], "sg_ln_b": sg_ln_b[l][None],
            "ln1_g": ln1_g[l][None], "ln1_b": ln1_b[l][None],
            "router_wt": router_w[l].T,
            "router_bb": jnp.broadcast_to(router_b[l][:, None], (N_EXPERTS, TOK_BLOCK)),
        }
        x1, pos, gates, cnt, xs = _mixer_call(xc, batch, seq, l, lw, tables)
        bexp, src_piece, nvb, ypiece = _routing_tables(cnt[:, :, 0], n_blocks)
        yb = _expert_call(xs, bexp, src_piece, nvb, l, e_w1, e_b1r, e_w2, e_b2r, n_blocks)
        xc = _combine_call(x1, yb, ypiece, pos.T, gates.T, ln2_g[l][None], ln2_b[l][None])
    return xc.reshape(batch, seq, D_MODEL)
```

```python
import functools
import math

import jax
import jax.numpy as jnp
import numpy as np
from jax import lax
from jax.experimental import pallas as pl
from jax.experimental.pallas import tpu as pltpu

F32 = jnp.float32
BF16 = jnp.bfloat16

D_MODEL = 1024
DEPTH = 4
RET_HEADS = 4
RET_QK_DIM = 128
RET_V_DIM = 256
RET_QK = RET_HEADS * RET_QK_DIM
RET_V = RET_HEADS * RET_V_DIM
ROPE_BASE = 10000.0
SG_BLOCK = 128
SG_CHUNK = 64
SG_GROUPS = 8
SG_WIDTH = D_MODEL
N_EXPERTS = 32
TOP_K = 4
D_EXPERT = D_MODEL
SWIGLU_LIMIT = 7.0
SWIGLU_ALPHA = 1.702
LN_EPS = 1e-5
DN_ALPHA = (2 * DEPTH) ** 0.25

OFF_Q = 0
OFF_K = OFF_Q + RET_QK
OFF_V = OFF_K + RET_QK
OFF_G = OFF_V + RET_V
OFF_U = OFF_G + RET_V
OFF_VS = OFF_U + SG_WIDTH
OFF_GA = OFF_VS + SG_WIDTH
OFF_GB = OFF_GA + D_MODEL
IN_WIDTH = OFF_GB + D_MODEL

SUBLANES_V7X = 8
TOK_BLOCK = 256
PIECE = SUBLANES_V7X
GROUP_ROWS = TOK_BLOCK * TOP_K + N_EXPERTS * PIECE
GROUP_PIECES = GROUP_ROWS // PIECE
EXP_BLOCK = 512
EXP_PIECES = EXP_BLOCK // PIECE
F_CHUNK = 512
W_CAST_ROWS = 128
VMEM_LIMIT_V7X = 56 * 1024 * 1024


def _layer_norm(x, g, b):
    mu = jnp.mean(x, axis=-1, keepdims=True)
    xc = x - mu
    var = jnp.mean(xc * xc, axis=-1, keepdims=True)
    return xc * lax.rsqrt(var + LN_EPS) * g + b


def _gelu_tanh(x):
    c = math.sqrt(2.0 / math.pi)
    return 0.5 * x * (1.0 + jnp.tanh(c * (x + 0.044715 * (x * x * x))))


def _dot(a, b):
    return jnp.dot(a, b, preferred_element_type=F32)


def _dot_nt(a, b):
    return lax.dot_general(a, b, (((1,), (1,)), ((), ())), preferred_element_type=F32)


def _dot_tn(a, b):
    return lax.dot_general(a, b, (((0,), (0,)), ((), ())), preferred_element_type=F32)


def _split_bf16(x):
    hi = x.astype(BF16)
    lo = (x - hi.astype(F32)).astype(BF16)
    return hi, lo


def _mixer_kernel(x_ref, w_in_ref, p_ret_ref, p_sg_ref, w_o_ref, sgw_ref, sgbt_ref,
                  sglng_ref, sglnb_ref, ln1g_ref, ln1b_ref, rwt_ref, rbb_ref,
                  cq_ref, sq_ref, ck_ref, sk_ref, dmat_ref, qdec_ref, kdec_ref,
                  x1_ref, pos_ref, gate_ref, cnt_ref, xs_ref,
                  state_ref, ysg_ref, *, chunk_decay):
    ts = TOK_BLOCK

    @pl.when(pl.program_id(1) == 0)
    def _():
        state_ref[...] = jnp.zeros_like(state_ref)

    xb = x_ref[...]
    xbf = xb.astype(BF16)

    def proj(off, width):
        return _dot(xbf, w_in_ref[0, :, off:off + width])

    cq, sq, ck, sk = cq_ref[...], sq_ref[...], ck_ref[...], sk_ref[...]
    acc_ret = jnp.zeros((ts, D_MODEL), F32)
    for h in range(RET_HEADS):
        if h % 2 == 0:
            q2 = proj(OFF_Q + h * RET_QK_DIM, 2 * RET_QK_DIM)
            k2 = proj(OFF_K + h * RET_QK_DIM, 2 * RET_QK_DIM)
        lanes = slice((h % 2) * RET_QK_DIM, (h % 2 + 1) * RET_QK_DIM)
        q, k = q2[:, lanes], k2[:, lanes]
        v = proj(OFF_V + h * RET_V_DIM, RET_V_DIM).astype(BF16)
        qr = q * cq + pltpu.roll(q, RET_QK_DIM // 2, 1) * sq
        kr = k * ck + pltpu.roll(k, RET_QK_DIM // 2, 1) * sk
        scores = _dot_nt(qr.astype(BF16), kr.astype(BF16)) * dmat_ref[h]
        inner = _dot(scores.astype(BF16), v)
        st = state_ref[h]
        cross = _dot((qr * qdec_ref[h]).astype(BF16), st.astype(BF16))
        kd = (kr * kdec_ref[h]).astype(BF16)
        state_ref[h] = st * chunk_decay[h] + _dot_tn(kd, v)
        y = inner + cross
        mu = jnp.mean(y, axis=-1, keepdims=True)
        yc = y - mu
        var = jnp.mean(yc * yc, axis=-1, keepdims=True)
        yn = yc * lax.rsqrt(var + LN_EPS)
        g = proj(OFF_G + h * RET_V_DIM, RET_V_DIM)
        yr = yn * (g * jax.nn.sigmoid(g))
        acc_ret = acc_ret + _dot(yr.astype(BF16), p_ret_ref[0, h * RET_V_DIM:(h + 1) * RET_V_DIM, :])

    u = _gelu_tanh(proj(OFF_U, SG_WIDTH))
    vn = _layer_norm(_gelu_tanh(proj(OFF_VS, SG_WIDTH)), sglng_ref[...], sglnb_ref[...])
    qi = lax.broadcasted_iota(jnp.int32, (SG_BLOCK, SG_BLOCK), 0) // SG_CHUNK
    pi = lax.broadcasted_iota(jnp.int32, (SG_BLOCK, SG_BLOCK), 1) // SG_CHUNK
    sg_mask = qi >= pi
    for gi in range(SG_GROUPS):
        wm = jnp.where(sg_mask, sgw_ref[gi], 0.0).astype(BF16)
        bias = sgbt_ref[:, gi:gi + 1]
        cols = slice(gi * SG_BLOCK, (gi + 1) * SG_BLOCK)
        blocks = [slice(blk * SG_BLOCK, (blk + 1) * SG_BLOCK) for blk in range(ts // SG_BLOCK)]
        vcat = jnp.concatenate([vn[rows, cols] for rows in blocks], axis=1).astype(BF16)
        mixed = _dot(wm, vcat) + bias
        for blk, rows in enumerate(blocks):
            ysg_ref[rows, cols] = (u[rows, cols] * mixed[:, blk * SG_BLOCK:(blk + 1) * SG_BLOCK]).astype(BF16)
    acc_sg = _dot(ysg_ref[...], p_sg_ref[0])

    ga = proj(OFF_GA, D_MODEL)
    gb = proj(OFF_GB, D_MODEL)
    hmix = jax.nn.sigmoid(ga) * acc_ret + jax.nn.sigmoid(gb) * acc_sg
    mix = _dot(hmix.astype(BF16), w_o_ref[0])
    x1 = _layer_norm(DN_ALPHA * xb + mix, ln1g_ref[...], ln1b_ref[...])
    x1_ref[...] = x1

    x_hi, x_lo = _split_bf16(x1)
    w_hi, w_lo = _split_bf16(rwt_ref[...])
    logits = (_dot_nt(w_hi, x_hi) + _dot_nt(w_hi, x_lo) + _dot_nt(w_lo, x_hi)) + rbb_ref[...]
    e_iota = lax.broadcasted_iota(jnp.int32, (N_EXPERTS, ts), 0).astype(F32)
    sels, vals = [], []
    work = logits
    for _ in range(TOP_K):
        m = jnp.max(work, axis=0, keepdims=True)
        idx = jnp.min(jnp.where(work == m, e_iota, float(N_EXPERTS)), axis=0, keepdims=True)
        sel = e_iota == idx
        work = jnp.where(sel, -jnp.inf, work)
        sels.append(sel)
        vals.append(m)
    exps = [jnp.exp(v - vals[0]) for v in vals]
    denom = exps[0] + exps[1] + exps[2] + exps[3]
    gate_ref[...] = jnp.concatenate([e / denom for e in exps], axis=0)

    onehots = [jnp.where(s, 1.0, 0.0) for s in sels]
    oh_sum = onehots[0] + onehots[1] + onehots[2] + onehots[3]
    t_row = lax.broadcasted_iota(jnp.int32, (ts, ts), 0)
    t_col = lax.broadcasted_iota(jnp.int32, (ts, ts), 1)
    upper = jnp.where(t_row < t_col, 1.0, 0.0).astype(BF16)
    before = _dot(oh_sum.astype(BF16), upper)
    cnt = jnp.sum(oh_sum, axis=1, keepdims=True)
    cnt_pad = jnp.floor((cnt + (PIECE - 1)) * (1.0 / PIECE)) * PIECE
    cnt_b = jnp.broadcast_to(cnt_pad, (N_EXPERTS, ts))
    e_row = lax.broadcasted_iota(jnp.int32, (N_EXPERTS, N_EXPERTS), 0)
    e_col = lax.broadcasted_iota(jnp.int32, (N_EXPERTS, N_EXPERTS), 1)
    lower = jnp.where(e_col < e_row, 1.0, 0.0).astype(BF16)
    strip_off = _dot(lower, cnt_b.astype(BF16))
    slot = strip_off + before
    pos = [jnp.sum(oh * slot, axis=0, keepdims=True).astype(jnp.int32) for oh in onehots]
    pos_ref[...] = jnp.concatenate(pos, axis=0)
    cnt_ref[0] = cnt_b[:, :128].astype(jnp.int32)
    r_iota = lax.broadcasted_iota(jnp.int32, (GROUP_ROWS, ts), 0)
    hit = (r_iota == pos[0]) | (r_iota == pos[1]) | (r_iota == pos[2]) | (r_iota == pos[3])
    perm = jnp.where(hit, 1.0, 0.0).astype(BF16)
    xs_ref[...] = _dot(perm, x_hi)


def _mixer_call(x2d, batch, seq, layer, lw, tables):
    t = batch * seq
    ns = seq // TOK_BLOCK
    groups = t // TOK_BLOCK
    const = lambda *shape: pl.BlockSpec(shape, lambda b, j: (0,) * len(shape),
                                        pipeline_mode=pl.Buffered(1))
    stacked = lambda *shape: pl.BlockSpec((1,) + shape, lambda b, j: (layer, 0, 0),
                                          pipeline_mode=pl.Buffered(1))
    tok = lambda width: pl.BlockSpec((TOK_BLOCK, width), lambda b, j: (b * ns + j, 0))
    seqtab = pl.BlockSpec((TOK_BLOCK, RET_QK_DIM), lambda b, j: (j, 0))
    lanes = lambda rows: pl.BlockSpec((rows, TOK_BLOCK), lambda b, j: (0, b * ns + j))
    in_specs = [
        tok(D_MODEL),
        stacked(D_MODEL, IN_WIDTH), stacked(RET_V, D_MODEL), stacked(SG_WIDTH, D_MODEL),
        stacked(D_MODEL, D_MODEL),
        const(SG_GROUPS, SG_BLOCK, SG_BLOCK), const(SG_BLOCK, SG_GROUPS),
        const(1, SG_WIDTH), const(1, SG_WIDTH), const(1, D_MODEL), const(1, D_MODEL),
        const(N_EXPERTS, D_MODEL), const(N_EXPERTS, TOK_BLOCK),
        seqtab, seqtab, seqtab, seqtab,
        const(RET_HEADS, TOK_BLOCK, TOK_BLOCK), const(RET_HEADS, TOK_BLOCK, RET_QK_DIM),
        const(RET_HEADS, TOK_BLOCK, RET_QK_DIM),
    ]
    out_shape = [
        jax.ShapeDtypeStruct((t, D_MODEL), F32),
        jax.ShapeDtypeStruct((TOP_K, t), jnp.int32),
        jax.ShapeDtypeStruct((TOP_K, t), F32),
        jax.ShapeDtypeStruct((groups, N_EXPERTS, 128), jnp.int32),
        jax.ShapeDtypeStruct((groups * GROUP_ROWS, D_MODEL), F32),
    ]
    out_specs = [
        tok(D_MODEL), lanes(TOP_K), lanes(TOP_K),
        pl.BlockSpec((1, N_EXPERTS, 128), lambda b, j: (b * ns + j, 0, 0)),
        pl.BlockSpec((GROUP_ROWS, D_MODEL), lambda b, j: (b * ns + j, 0)),
    ]
    kern = functools.partial(_mixer_kernel, chunk_decay=tables["chunk_decay"])
    return pl.pallas_call(
        kern,
        grid=(batch, ns),
        in_specs=in_specs,
        out_specs=out_specs,
        out_shape=out_shape,
        scratch_shapes=[pltpu.VMEM((RET_HEADS, RET_QK_DIM, RET_V_DIM), F32),
                        pltpu.VMEM((TOK_BLOCK, SG_WIDTH), BF16)],
        compiler_params=pltpu.CompilerParams(dimension_semantics=("arbitrary", "arbitrary"),
                                             vmem_limit_bytes=VMEM_LIMIT_V7X),
        name="mixer",
    )(x2d, lw["w_in"], lw["p_ret"], lw["p_sg"], lw["w_o"], lw["sg_w"], lw["sg_bt"],
      lw["sg_ln_g"], lw["sg_ln_b"], lw["ln1_g"], lw["ln1_b"], lw["router_wt"], lw["router_bb"],
      tables["cq"], tables["sq"], tables["ck"], tables["sk"],
      tables["dmat"], tables["qdec"], tables["kdec"])


def _expert_kernel(bexp_ref, src_ref, nvb_ref, xs_hbm, w1_ref, b1_ref, w2_ref, b2_ref,
                   yb_ref, xbuf, w1b, w2b, sem):
    b = pl.program_id(0)
    nvb = nvb_ref[0]

    def gather(blk, slot):
        for i in range(EXP_PIECES):
            row = pl.multiple_of(src_ref[blk * EXP_PIECES + i] * PIECE, PIECE)
            pltpu.make_async_copy(xs_hbm.at[pl.ds(row, PIECE), :],
                                  xbuf.at[slot, pl.ds(i * PIECE, PIECE), :],
                                  sem.at[slot]).start()

    def wait(slot):
        pltpu.make_async_copy(xs_hbm.at[pl.ds(0, EXP_BLOCK), :], xbuf.at[slot], sem.at[slot]).wait()

    @pl.when(b == 0)
    def _():
        gather(0, 0)

    new_expert = (b == 0) | (bexp_ref[b] != bexp_ref[jnp.maximum(b - 1, 0)])

    @pl.when(new_expert & (b < nvb))
    def _():
        def cast_rows(i, carry):
            r = pl.multiple_of(i * W_CAST_ROWS, W_CAST_ROWS)
            w1b[pl.ds(r, W_CAST_ROWS), :] = w1_ref[0, 0, pl.ds(r, W_CAST_ROWS), :].astype(BF16)
            w2b[pl.ds(r, W_CAST_ROWS), :] = w2_ref[0, 0, pl.ds(r, W_CAST_ROWS), :].astype(BF16)
            return carry
        lax.fori_loop(0, D_MODEL // W_CAST_ROWS, cast_rows, 0)

    @pl.when(b < nvb)
    def _():
        slot = b % 2
        wait(slot)
        gather(jnp.minimum(b + 1, nvb - 1), 1 - slot)
        x = xbuf[slot].astype(BF16)
        y = jnp.zeros((EXP_BLOCK, D_MODEL), F32)
        for j in range(D_EXPERT // F_CHUNK):
            c0 = j * F_CHUNK
            hg = _dot(x, w1b[:, c0:c0 + F_CHUNK]) + b1_ref[0, 0, :, c0:c0 + F_CHUNK]
            hu = (_dot(x, w1b[:, D_EXPERT + c0:D_EXPERT + c0 + F_CHUNK])
                  + b1_ref[0, 0, :, D_EXPERT + c0:D_EXPERT + c0 + F_CHUNK])
            gate = jnp.minimum(hg, SWIGLU_LIMIT)
            up = jnp.clip(hu, -SWIGLU_LIMIT, SWIGLU_LIMIT)
            act = gate * jax.nn.sigmoid(SWIGLU_ALPHA * gate) * (up + 1.0)
            y = y + _dot(act.astype(BF16), w2b[c0:c0 + F_CHUNK, :])
        yb_ref[...] = y + b2_ref[0, 0]

    @pl.when(b == nvb - 1)
    def _():
        wait(1 - b % 2)

    @pl.when(b >= nvb)
    def _():
        yb_ref[...] = jnp.zeros_like(yb_ref)


def _expert_call(xs, block_exp, src_piece, nvb, layer, e_w1, e_b1, e_w2, e_b2, n_blocks):
    wspec = lambda *shape: pl.BlockSpec((1, 1) + shape, lambda b, be, sp, nv: (layer, be[b], 0, 0))
    grid_spec = pltpu.PrefetchScalarGridSpec(
        num_scalar_prefetch=3,
        grid=(n_blocks,),
        in_specs=[
            pl.BlockSpec(memory_space=pl.ANY),
            wspec(D_MODEL, 2 * D_EXPERT), wspec(1, 2 * D_EXPERT),
            wspec(D_EXPERT, D_MODEL), wspec(1, D_MODEL),
        ],
        out_specs=pl.BlockSpec((EXP_BLOCK, D_MODEL), lambda b, be, sp, nv: (b, 0)),
        scratch_shapes=[pltpu.VMEM((2, EXP_BLOCK, D_MODEL), F32),
                        pltpu.VMEM((D_MODEL, 2 * D_EXPERT), BF16),
                        pltpu.VMEM((D_EXPERT, D_MODEL), BF16),
                        pltpu.SemaphoreType.DMA((2,))],
    )
    return pl.pallas_call(
        _expert_kernel,
        grid_spec=grid_spec,
        out_shape=jax.ShapeDtypeStruct((n_blocks * EXP_BLOCK, D_MODEL), F32),
        compiler_params=pltpu.CompilerParams(dimension_semantics=("arbitrary",),
                                             vmem_limit_bytes=VMEM_LIMIT_V7X),
        name="experts",
    )(block_exp, src_piece, nvb, xs, e_w1, e_b1, e_w2, e_b2)


def _combine_kernel(yp_ref, x1_ref, yb_hbm, post_ref, gatet_ref, g_ref, b_ref, out_ref, ybuf, sem):
    g = pl.program_id(0)
    ng = pl.num_programs(0)

    def gather(grp, slot):
        for i in range(GROUP_PIECES):
            row = pl.multiple_of(yp_ref[grp * GROUP_PIECES + i] * PIECE, PIECE)
            pltpu.make_async_copy(yb_hbm.at[pl.ds(row, PIECE), :],
                                  ybuf.at[slot, pl.ds(i * PIECE, PIECE), :],
                                  sem.at[slot]).start()

    @pl.when(g == 0)
    def _():
        gather(0, 0)

    @pl.when(g + 1 < ng)
    def _():
        gather(g + 1, (g + 1) % 2)

    slot = g % 2
    pltpu.make_async_copy(yb_hbm.at[pl.ds(0, GROUP_ROWS), :], ybuf.at[slot], sem.at[slot]).wait()
    c_iota = lax.broadcasted_iota(jnp.int32, (TOK_BLOCK, GROUP_ROWS), 1)
    post = post_ref[...]
    gatet = gatet_ref[...]
    unsort = jnp.zeros((TOK_BLOCK, GROUP_ROWS), F32)
    for k in range(TOP_K):
        unsort = unsort + jnp.where(c_iota == post[:, k:k + 1], gatet[:, k:k + 1], 0.0)
    ffn = _dot(unsort.astype(BF16), ybuf[slot].astype(BF16))
    out_ref[...] = _layer_norm(DN_ALPHA * x1_ref[...] + ffn, g_ref[...], b_ref[...])


def _combine_call(x1, yb, ypiece, post, gatet, ln_g, ln_b):
    t = x1.shape[0]
    groups = t // TOK_BLOCK
    grid_spec = pltpu.PrefetchScalarGridSpec(
        num_scalar_prefetch=1,
        grid=(groups,),
        in_specs=[
            pl.BlockSpec((TOK_BLOCK, D_MODEL), lambda g, yp: (g, 0)),
            pl.BlockSpec(memory_space=pl.ANY),
            pl.BlockSpec((TOK_BLOCK, TOP_K), lambda g, yp: (g, 0)),
            pl.BlockSpec((TOK_BLOCK, TOP_K), lambda g, yp: (g, 0)),
            pl.BlockSpec((1, D_MODEL), lambda g, yp: (0, 0)),
            pl.BlockSpec((1, D_MODEL), lambda g, yp: (0, 0)),
        ],
        out_specs=pl.BlockSpec((TOK_BLOCK, D_MODEL), lambda g, yp: (g, 0)),
        scratch_shapes=[pltpu.VMEM((2, GROUP_ROWS, D_MODEL), F32), pltpu.SemaphoreType.DMA((2,))],
    )
    return pl.pallas_call(
        _combine_kernel,
        grid_spec=grid_spec,
        out_shape=jax.ShapeDtypeStruct((t, D_MODEL), F32),
        compiler_params=pltpu.CompilerParams(dimension_semantics=("arbitrary",),
                                             vmem_limit_bytes=VMEM_LIMIT_V7X),
        name="combine",
    )(ypiece, x1, yb, post, gatet, ln_g, ln_b)


def _routing_tables(cnt_pad, n_blocks):
    groups = cnt_pad.shape[0]
    npc = cnt_pad // PIECE
    strip_start = jnp.cumsum(npc, axis=1) - npc
    tot = jnp.sum(npc, axis=0)
    tot_pad = (tot + EXP_PIECES - 1) // EXP_PIECES * EXP_PIECES
    exp_end = jnp.cumsum(tot_pad)
    exp_start = exp_end - tot_pad
    dstart = exp_start[None, :] + jnp.cumsum(npc, axis=0) - npc
    nvb = (exp_end[-1] // EXP_PIECES).astype(jnp.int32)

    blocks = jnp.arange(n_blocks, dtype=jnp.int32)
    bexp = jnp.sum(blocks[:, None] * EXP_PIECES >= exp_end[None, :], axis=1).astype(jnp.int32)
    bexp = jnp.minimum(bexp, N_EXPERTS - 1)
    last = jnp.sum((nvb - 1) * EXP_PIECES >= exp_end).astype(jnp.int32)
    bexp = jnp.where(blocks < nvb, bexp, jnp.minimum(last, N_EXPERTS - 1))

    d = jnp.arange(n_blocks * EXP_PIECES, dtype=jnp.int32)
    e_of = jnp.minimum(jnp.sum(d[:, None] >= exp_end[None, :], axis=1), N_EXPERTS - 1)
    oh_e = (e_of[:, None] == jnp.arange(N_EXPERTS, dtype=jnp.int32)[None, :]).astype(F32)
    row_of = lambda tab: jnp.dot(oh_e, tab.T.astype(F32), precision=lax.Precision.HIGHEST).astype(jnp.int32)
    ds_e = row_of(dstart)
    g_of = jnp.maximum(jnp.sum(ds_e <= d[:, None], axis=1) - 1, 0)
    at_g = g_of[:, None] == jnp.arange(groups, dtype=jnp.int32)[None, :]
    pick = lambda rows: jnp.sum(jnp.where(at_g, rows, 0), axis=1)
    i_of = d - pick(ds_e)
    valid = (i_of >= 0) & (i_of < pick(row_of(npc)))
    src = g_of * GROUP_PIECES + pick(row_of(strip_start)) + i_of
    src_piece = jnp.where(valid, src, 0).astype(jnp.int32)

    j = jnp.arange(GROUP_PIECES, dtype=jnp.int32)
    e_loc = jnp.maximum(jnp.sum(strip_start[:, None, :] <= j[None, :, None], axis=2) - 1, 0)
    at_e = e_loc[:, :, None] == jnp.arange(N_EXPERTS, dtype=jnp.int32)[None, None, :]
    pick_e = lambda tab: jnp.sum(jnp.where(at_e, tab[:, None, :], 0), axis=2)
    st_loc = pick_e(strip_start)
    np_loc = pick_e(npc)
    ds_loc = pick_e(dstart)
    i_loc = j[None, :] - st_loc
    ypiece = jnp.where(i_loc < np_loc, ds_loc + i_loc, 0).astype(jnp.int32).reshape(groups * GROUP_PIECES)
    return bexp, src_piece, nvb.reshape(1), ypiece


def _tables(seq):
    half = RET_QK_DIM // 2
    inv = ROPE_BASE ** (-jnp.arange(half, dtype=F32) / half)
    ang = jnp.arange(seq, dtype=jnp.int32).astype(F32)[:, None] * inv[None, :]
    cos, sin = jnp.cos(ang), jnp.sin(ang)
    cosf = jnp.concatenate([cos, cos], axis=1)
    sinf = jnp.concatenate([-sin, sin], axis=1)
    kscale = RET_QK_DIM ** -0.5
    log_g = jnp.log(1.0 - jnp.exp(jnp.linspace(math.log(1.0 / 32), math.log(1.0 / 512), RET_HEADS)))
    idx = jnp.arange(TOK_BLOCK, dtype=F32)
    diff = idx[:, None] - idx[None, :]
    dmat = jnp.where(diff >= 0, jnp.exp(log_g[:, None, None] * jnp.maximum(diff, 0.0)), 0.0)
    qdec = jnp.exp(log_g[:, None] * (idx[None, :] + 1.0))
    kdec = jnp.exp(log_g[:, None] * (TOK_BLOCK - 1.0 - idx[None, :]))
    bc = lambda a: jnp.broadcast_to(a[:, :, None], (RET_HEADS, TOK_BLOCK, RET_QK_DIM))
    log_g_host = np.log(1.0 - np.exp(np.linspace(math.log(1.0 / 32), math.log(1.0 / 512), RET_HEADS)))
    chunk_decay = tuple(float(np.float32(np.exp(np.float32(lg) * np.float32(TOK_BLOCK)))) for lg in log_g_host)
    return {"cq": cosf, "sq": sinf, "ck": cosf * kscale, "sk": sinf * kscale,
            "dmat": dmat.astype(F32), "qdec": bc(qdec), "kdec": bc(kdec), "chunk_decay": chunk_decay}


def kernel(x, w_in, p_ret, sg_ln_g, sg_ln_b, sg_w, sg_b, p_sg, w_o, ln1_g, ln1_b,
           router_w, router_b, e_w1, e_b1, e_w2, e_b2, ln2_g, ln2_b):
    batch, seq, d = x.shape
    assert d == D_MODEL and seq % TOK_BLOCK == 0
    depth = w_in.shape[0]
    t = batch * seq
    groups = t // TOK_BLOCK
    n_blocks = (groups * GROUP_PIECES + N_EXPERTS * (EXP_PIECES - 1)) // EXP_PIECES + 1
    tables = _tables(seq)
    xc = x.reshape(t, D_MODEL)
    w_in_b, p_ret_b, p_sg_b, w_o_b = (w.astype(BF16) for w in (w_in, p_ret, p_sg, w_o))
    e_b1r = e_b1.reshape(depth, N_EXPERTS, 1, 2 * D_EXPERT)
    e_b2r = e_b2.reshape(depth, N_EXPERTS, 1, D_MODEL)
    for l in range(depth):
        lw = {
            "w_in": w_in_b, "p_ret": p_ret_b, "p_sg": p_sg_b,
            "w_o": w_o_b, "sg_w": sg_w[l], "sg_bt": sg_b[l].T,
            "sg_ln_g": sg_ln_g[l][None], "sg_ln_b": sg_ln_b[l][None],
            "ln1_g": ln1_g[l][None], "ln1_b": ln1_b[l][None],
            "router_wt": router_w[l].T,
            "router_bb": jnp.broadcast_to(router_b[l][:, None], (N_EXPERTS, TOK_BLOCK)),
        }
        x1, pos, gates, cnt, xs = _mixer_call(xc, batch, seq, l, lw, tables)
        bexp, src_piece, nvb, ypiece = _routing_tables(cnt[:, :, 0], n_blocks)
        yb = _expert_call(xs, bexp, src_piece, nvb, l, e_w1, e_b1r, e_w2, e_b2r, n_blocks)
        xc = _combine_call(x1, yb, ypiece, pos.T, gates.T, ln2_g[l][None], ln2_b[l][None])
    return xc.reshape(batch, seq, D_MODEL)
```

```python
import functools
import math

import jax
import jax.numpy as jnp
import numpy as np
from jax import lax
from jax.experimental import pallas as pl
from jax.experimental.pallas import tpu as pltpu

F32 = jnp.float32
BF16 = jnp.bfloat16

D_MODEL = 1024
DEPTH = 4
RET_HEADS = 4
RET_QK_DIM = 128
RET_V_DIM = 256
RET_QK = RET_HEADS * RET_QK_DIM
RET_V = RET_HEADS * RET_V_DIM
ROPE_BASE = 10000.0
SG_BLOCK = 128
SG_CHUNK = 64
SG_GROUPS = 8
SG_WIDTH = D_MODEL
N_EXPERTS = 32
TOP_K = 4
D_EXPERT = D_MODEL
SWIGLU_LIMIT = 7.0
SWIGLU_ALPHA = 1.702
LN_EPS = 1e-5
DN_ALPHA = (2 * DEPTH) ** 0.25

OFF_Q = 0
OFF_K = OFF_Q + RET_QK
OFF_V = OFF_K + RET_QK
OFF_G = OFF_V + RET_V
OFF_U = OFF_G + RET_V
OFF_VS = OFF_U + SG_WIDTH
OFF_GA = OFF_VS + SG_WIDTH
OFF_GB = OFF_GA + D_MODEL
IN_WIDTH = OFF_GB + D_MODEL

SUBLANES_V7X = 8
TOK_BLOCK = 256
PIECE = SUBLANES_V7X
GROUP_ROWS = TOK_BLOCK * TOP_K + N_EXPERTS * PIECE
GROUP_PIECES = GROUP_ROWS // PIECE
SORT_ROWS = 256
EXP_BLOCK = 512
EXP_PIECES = EXP_BLOCK // PIECE
F_CHUNK = 512
W_CAST_ROWS = 128
VMEM_LIMIT_V7X = 56 * 1024 * 1024


def _layer_norm(x, g, b):
    mu = jnp.mean(x, axis=-1, keepdims=True)
    xc = x - mu
    var = jnp.mean(xc * xc, axis=-1, keepdims=True)
    return xc * lax.rsqrt(var + LN_EPS) * g + b


def _gelu_tanh(x):
    c = math.sqrt(2.0 / math.pi)
    return 0.5 * x * (1.0 + jnp.tanh(c * (x + 0.044715 * (x * x * x))))


def _dot(a, b):
    return jnp.dot(a, b, preferred_element_type=F32)


def _dot_nt(a, b):
    return lax.dot_general(a, b, (((1,), (1,)), ((), ())), preferred_element_type=F32)


def _dot_tn(a, b):
    return lax.dot_general(a, b, (((0,), (0,)), ((), ())), preferred_element_type=F32)


def _split_bf16(x):
    hi = x.astype(BF16)
    lo = (x - hi.astype(F32)).astype(BF16)
    return hi, lo


def _mixer_kernel(x_ref, w_in_ref, p_ret_ref, p_sg_ref, w_o_ref, sgw_ref, sgbt_ref,
                  sglng_ref, sglnb_ref, ln1g_ref, ln1b_ref, rwt_ref, rbb_ref,
                  cq_ref, sq_ref, ck_ref, sk_ref, dmat_ref, qdec_ref, kdec_ref,
                  x1_ref, pos_ref, gate_ref, cnt_ref, xs_ref,
                  state_ref, ysg_ref, x1_prev_ref, *, chunk_decay, steps_per_seq):
    ts = TOK_BLOCK
    step = pl.program_id(0)

    @pl.when(step == 0)
    def _():
        x1_prev_ref[...] = jnp.zeros_like(x1_prev_ref)

    @pl.when(step % steps_per_seq == 0)
    def _():
        state_ref[...] = jnp.zeros_like(state_ref)

    routing = _route_and_sort(x1_prev_ref[...], rwt_ref, rbb_ref, pos_ref, gate_ref, cnt_ref, xs_ref)
    routing_stage = lambda: next(routing, None)

    xb = x_ref[...]
    xbf = xb.astype(BF16)

    def proj(off, width):
        return _dot(xbf, w_in_ref[0, :, off:off + width])

    cq, sq, ck, sk = cq_ref[...], sq_ref[...], ck_ref[...], sk_ref[...]
    acc_ret = jnp.zeros((ts, D_MODEL), F32)
    for h in range(RET_HEADS):
        if h % 2 == 0:
            q2 = proj(OFF_Q + h * RET_QK_DIM, 2 * RET_QK_DIM)
            k2 = proj(OFF_K + h * RET_QK_DIM, 2 * RET_QK_DIM)
        lanes = slice((h % 2) * RET_QK_DIM, (h % 2 + 1) * RET_QK_DIM)
        q, k = q2[:, lanes], k2[:, lanes]
        routing_stage()
        v = proj(OFF_V + h * RET_V_DIM, RET_V_DIM).astype(BF16)
        qr = q * cq + pltpu.roll(q, RET_QK_DIM // 2, 1) * sq
        kr = k * ck + pltpu.roll(k, RET_QK_DIM // 2, 1) * sk
        scores = _dot_nt(qr.astype(BF16), kr.astype(BF16)) * dmat_ref[h]
        inner = _dot(scores.astype(BF16), v)
        st = state_ref[h]
        cross = _dot((qr * qdec_ref[h]).astype(BF16), st.astype(BF16))
        kd = (kr * kdec_ref[h]).astype(BF16)
        state_ref[h] = st * chunk_decay[h] + _dot_tn(kd, v)
        routing_stage()
        y = inner + cross
        mu = jnp.mean(y, axis=-1, keepdims=True)
        yc = y - mu
        var = jnp.mean(yc * yc, axis=-1, keepdims=True)
        yn = yc * lax.rsqrt(var + LN_EPS)
        g = proj(OFF_G + h * RET_V_DIM, RET_V_DIM)
        yr = yn * (g * jax.nn.sigmoid(g))
        acc_ret = acc_ret + _dot(yr.astype(BF16), p_ret_ref[0, h * RET_V_DIM:(h + 1) * RET_V_DIM, :])

    pre_u = proj(OFF_U, SG_WIDTH)
    pre_vs = proj(OFF_VS, SG_WIDTH)
    ga = proj(OFF_GA, D_MODEL)
    u = _gelu_tanh(pre_u)
    routing_stage()
    gb = proj(OFF_GB, D_MODEL)
    vn = _layer_norm(_gelu_tanh(pre_vs), sglng_ref[...], sglnb_ref[...])
    qi = lax.broadcasted_iota(jnp.int32, (SG_BLOCK, SG_BLOCK), 0) // SG_CHUNK
    pi = lax.broadcasted_iota(jnp.int32, (SG_BLOCK, SG_BLOCK), 1) // SG_CHUNK
    sg_mask = qi >= pi
    for gi in range(SG_GROUPS):
        wm = jnp.where(sg_mask, sgw_ref[gi], 0.0).astype(BF16)
        bias = sgbt_ref[:, gi:gi + 1]
        cols = slice(gi * SG_BLOCK, (gi + 1) * SG_BLOCK)
        blocks = [slice(blk * SG_BLOCK, (blk + 1) * SG_BLOCK) for blk in range(ts // SG_BLOCK)]
        vcat = jnp.concatenate([vn[rows, cols] for rows in blocks], axis=1).astype(BF16)
        mixed = _dot(wm, vcat) + bias
        for blk, rows in enumerate(blocks):
            ysg_ref[rows, cols] = (u[rows, cols] * mixed[:, blk * SG_BLOCK:(blk + 1) * SG_BLOCK]).astype(BF16)
    routing_stage()
    acc_sg = _dot(ysg_ref[...], p_sg_ref[0])

    hmix = jax.nn.sigmoid(ga) * acc_ret + jax.nn.sigmoid(gb) * acc_sg
    mix = _dot(hmix.astype(BF16), w_o_ref[0])
    for _ in routing:
        pass
    x1 = _layer_norm(DN_ALPHA * xb + mix, ln1g_ref[...], ln1b_ref[...])
    x1_ref[...] = x1
    x1_prev_ref[...] = x1


def _route_and_sort(x1, rwt_ref, rbb_ref, pos_ref, gate_ref, cnt_ref, xs_ref):
    ts = TOK_BLOCK
    x_hi, x_lo = _split_bf16(x1)
    w_hi, w_lo = _split_bf16(rwt_ref[...])
    logits = (_dot_nt(w_hi, x_hi) + _dot_nt(w_hi, x_lo) + _dot_nt(w_lo, x_hi)) + rbb_ref[...]
    yield
    e_iota = lax.broadcasted_iota(jnp.int32, (N_EXPERTS, ts), 0).astype(F32)
    sels, vals = [], []
    work = logits
    for _ in range(TOP_K):
        m = jnp.max(work, axis=0, keepdims=True)
        idx = jnp.min(jnp.where(work == m, e_iota, float(N_EXPERTS)), axis=0, keepdims=True)
        sel = e_iota == idx
        work = jnp.where(sel, -jnp.inf, work)
        sels.append(sel)
        vals.append(m)
        yield
    exps = [jnp.exp(v - vals[0]) for v in vals]
    denom = exps[0] + exps[1] + exps[2] + exps[3]
    gate_ref[...] = jnp.concatenate([e / denom for e in exps], axis=0)

    onehots = [jnp.where(s, 1.0, 0.0) for s in sels]
    oh_sum = onehots[0] + onehots[1] + onehots[2] + onehots[3]
    t_row = lax.broadcasted_iota(jnp.int32, (ts, ts), 0)
    t_col = lax.broadcasted_iota(jnp.int32, (ts, ts), 1)
    upper = jnp.where(t_row < t_col, 1.0, 0.0).astype(BF16)
    before = _dot(oh_sum.astype(BF16), upper)
    cnt = jnp.sum(oh_sum, axis=1, keepdims=True)
    cnt_pad = jnp.floor((cnt + (PIECE - 1)) * (1.0 / PIECE)) * PIECE
    cnt_b = jnp.broadcast_to(cnt_pad, (N_EXPERTS, ts))
    e_row = lax.broadcasted_iota(jnp.int32, (N_EXPERTS, N_EXPERTS), 0)
    e_col = lax.broadcasted_iota(jnp.int32, (N_EXPERTS, N_EXPERTS), 1)
    lower = jnp.where(e_col < e_row, 1.0, 0.0).astype(BF16)
    strip_off = _dot(lower, cnt_b.astype(BF16))
    slot = strip_off + before
    pos = [jnp.sum(oh * slot, axis=0, keepdims=True).astype(jnp.int32) for oh in onehots]
    pos_ref[...] = jnp.concatenate(pos, axis=0)
    cnt_ref[0] = cnt_b[:, :128].astype(jnp.int32)
    yield
    for r0 in range(0, GROUP_ROWS, SORT_ROWS):
        r_iota = lax.broadcasted_iota(jnp.int32, (SORT_ROWS, ts), 0) + r0
        hit = (r_iota == pos[0]) | (r_iota == pos[1]) | (r_iota == pos[2]) | (r_iota == pos[3])
        perm = jnp.where(hit, 1.0, 0.0).astype(BF16)
        xs_ref[r0:r0 + SORT_ROWS, :] = _dot(perm, x_hi)
        yield


def _mixer_call(x2d, batch, seq, layer, lw, tables):
    t = batch * seq
    ns = seq // TOK_BLOCK
    groups = t // TOK_BLOCK
    const = lambda *shape: pl.BlockSpec(shape, lambda i: (0,) * len(shape),
                                        pipeline_mode=pl.Buffered(1))
    stacked = lambda *shape: pl.BlockSpec((1,) + shape, lambda i: (layer, 0, 0),
                                          pipeline_mode=pl.Buffered(1))
    cur = lambda i: jnp.minimum(i, groups - 1)
    prev = lambda i: jnp.maximum(i - 1, 0)
    seqtab = pl.BlockSpec((TOK_BLOCK, RET_QK_DIM), lambda i: (cur(i) % ns, 0))
    lanes = lambda rows: pl.BlockSpec((rows, TOK_BLOCK), lambda i: (0, prev(i)))
    in_specs = [
        pl.BlockSpec((TOK_BLOCK, D_MODEL), lambda i: (cur(i), 0)),
        stacked(D_MODEL, IN_WIDTH), stacked(RET_V, D_MODEL), stacked(SG_WIDTH, D_MODEL),
        stacked(D_MODEL, D_MODEL),
        const(SG_GROUPS, SG_BLOCK, SG_BLOCK), const(SG_BLOCK, SG_GROUPS),
        const(1, SG_WIDTH), const(1, SG_WIDTH), const(1, D_MODEL), const(1, D_MODEL),
        const(N_EXPERTS, D_MODEL), const(N_EXPERTS, TOK_BLOCK),
        seqtab, seqtab, seqtab, seqtab,
        const(RET_HEADS, TOK_BLOCK, TOK_BLOCK), const(RET_HEADS, TOK_BLOCK, RET_QK_DIM),
        const(RET_HEADS, TOK_BLOCK, RET_QK_DIM),
    ]
    out_shape = [
        jax.ShapeDtypeStruct((t + TOK_BLOCK, D_MODEL), F32),
        jax.ShapeDtypeStruct((TOP_K, t), jnp.int32),
        jax.ShapeDtypeStruct((TOP_K, t), F32),
        jax.ShapeDtypeStruct((groups, N_EXPERTS, 128), jnp.int32),
        jax.ShapeDtypeStruct((groups * GROUP_ROWS, D_MODEL), F32),
    ]
    out_specs = [
        pl.BlockSpec((TOK_BLOCK, D_MODEL), lambda i: (i, 0)), lanes(TOP_K), lanes(TOP_K),
        pl.BlockSpec((1, N_EXPERTS, 128), lambda i: (prev(i), 0, 0)),
        pl.BlockSpec((GROUP_ROWS, D_MODEL), lambda i: (prev(i), 0)),
    ]
    kern = functools.partial(_mixer_kernel, chunk_decay=tables["chunk_decay"], steps_per_seq=ns)
    return pl.pallas_call(
        kern,
        grid=(groups + 1,),
        in_specs=in_specs,
        out_specs=out_specs,
        out_shape=out_shape,
        scratch_shapes=[pltpu.VMEM((RET_HEADS, RET_QK_DIM, RET_V_DIM), F32),
                        pltpu.VMEM((TOK_BLOCK, SG_WIDTH), BF16),
                        pltpu.VMEM((TOK_BLOCK, D_MODEL), F32)],
        compiler_params=pltpu.CompilerParams(dimension_semantics=("arbitrary",),
                                             vmem_limit_bytes=VMEM_LIMIT_V7X),
        name="mixer",
    )(x2d, lw["w_in"], lw["p_ret"], lw["p_sg"], lw["w_o"], lw["sg_w"], lw["sg_bt"],
      lw["sg_ln_g"], lw["sg_ln_b"], lw["ln1_g"], lw["ln1_b"], lw["router_wt"], lw["router_bb"],
      tables["cq"], tables["sq"], tables["ck"], tables["sk"],
      tables["dmat"], tables["qdec"], tables["kdec"])


def _expert_kernel(bexp_ref, src_ref, nvb_ref, xs_hbm, w1_ref, b1_ref, w2_ref, b2_ref,
                   yb_ref, xbuf, w1b, w2b, sem):
    b = pl.program_id(0)
    nvb = nvb_ref[0]

    def gather(blk, slot):
        for i in range(EXP_PIECES):
            row = pl.multiple_of(src_ref[blk * EXP_PIECES + i] * PIECE, PIECE)
            pltpu.make_async_copy(xs_hbm.at[pl.ds(row, PIECE), :],
                                  xbuf.at[slot, pl.ds(i * PIECE, PIECE), :],
                                  sem.at[slot]).start()

    def wait(slot):
        pltpu.make_async_copy(xs_hbm.at[pl.ds(0, EXP_BLOCK), :], xbuf.at[slot], sem.at[slot]).wait()

    @pl.when(b == 0)
    def _():
        gather(0, 0)

    new_expert = (b == 0) | (bexp_ref[b] != bexp_ref[jnp.maximum(b - 1, 0)])

    @pl.when(new_expert & (b < nvb))
    def _():
        def cast_rows(i, carry):
            r = pl.multiple_of(i * W_CAST_ROWS, W_CAST_ROWS)
            w1b[pl.ds(r, W_CAST_ROWS), :] = w1_ref[0, 0, pl.ds(r, W_CAST_ROWS), :].astype(BF16)
            w2b[pl.ds(r, W_CAST_ROWS), :] = w2_ref[0, 0, pl.ds(r, W_CAST_ROWS), :].astype(BF16)
            return carry
        lax.fori_loop(0, D_MODEL // W_CAST_ROWS, cast_rows, 0)

    @pl.when(b < nvb)
    def _():
        slot = b % 2
        wait(slot)
        gather(jnp.minimum(b + 1, nvb - 1), 1 - slot)
        x = xbuf[slot].astype(BF16)
        y = jnp.zeros((EXP_BLOCK, D_MODEL), F32)
        for j in range(D_EXPERT // F_CHUNK):
            c0 = j * F_CHUNK
            hg = _dot(x, w1b[:, c0:c0 + F_CHUNK]) + b1_ref[0, 0, :, c0:c0 + F_CHUNK]
            hu = (_dot(x, w1b[:, D_EXPERT + c0:D_EXPERT + c0 + F_CHUNK])
                  + b1_ref[0, 0, :, D_EXPERT + c0:D_EXPERT + c0 + F_CHUNK])
            gate = jnp.minimum(hg, SWIGLU_LIMIT)
            up = jnp.clip(hu, -SWIGLU_LIMIT, SWIGLU_LIMIT)
            act = gate * jax.nn.sigmoid(SWIGLU_ALPHA * gate) * (up + 1.0)
            y = y + _dot(act.astype(BF16), w2b[c0:c0 + F_CHUNK, :])
        yb_ref[...] = y + b2_ref[0, 0]

    @pl.when(b == nvb - 1)
    def _():
        wait(1 - b % 2)

    @pl.when(b >= nvb)
    def _():
        yb_ref[...] = jnp.zeros_like(yb_ref)


def _expert_call(xs, block_exp, src_piece, nvb, layer, e_w1, e_b1, e_w2, e_b2, n_blocks):
    wspec = lambda *shape: pl.BlockSpec((1, 1) + shape, lambda b, be, sp, nv: (layer, be[b], 0, 0))
    grid_spec = pltpu.PrefetchScalarGridSpec(
        num_scalar_prefetch=3,
        grid=(n_blocks,),
        in_specs=[
            pl.BlockSpec(memory_space=pl.ANY),
            wspec(D_MODEL, 2 * D_EXPERT), wspec(1, 2 * D_EXPERT),
            wspec(D_EXPERT, D_MODEL), wspec(1, D_MODEL),
        ],
        out_specs=pl.BlockSpec((EXP_BLOCK, D_MODEL), lambda b, be, sp, nv: (b, 0)),
        scratch_shapes=[pltpu.VMEM((2, EXP_BLOCK, D_MODEL), F32),
                        pltpu.VMEM((D_MODEL, 2 * D_EXPERT), BF16),
                        pltpu.VMEM((D_EXPERT, D_MODEL), BF16),
                        pltpu.SemaphoreType.DMA((2,))],
    )
    return pl.pallas_call(
        _expert_kernel,
        grid_spec=grid_spec,
        out_shape=jax.ShapeDtypeStruct((n_blocks * EXP_BLOCK, D_MODEL), F32),
        compiler_params=pltpu.CompilerParams(dimension_semantics=("arbitrary",),
                                             vmem_limit_bytes=VMEM_LIMIT_V7X),
        name="experts",
    )(block_exp, src_piece, nvb, xs, e_w1, e_b1, e_w2, e_b2)


def _combine_kernel(yp_ref, x1_ref, yb_hbm, post_ref, gatet_ref, g_ref, b_ref, out_ref, ybuf, sem):
    g = pl.program_id(0)
    ng = pl.num_programs(0)

    def gather(grp, slot):
        for i in range(GROUP_PIECES):
            row = pl.multiple_of(yp_ref[grp * GROUP_PIECES + i] * PIECE, PIECE)
            pltpu.make_async_copy(yb_hbm.at[pl.ds(row, PIECE), :],
                                  ybuf.at[slot, pl.ds(i * PIECE, PIECE), :],
                                  sem.at[slot]).start()

    @pl.when(g == 0)
    def _():
        gather(0, 0)

    @pl.when(g + 1 < ng)
    def _():
        gather(g + 1, (g + 1) % 2)

    slot = g % 2
    pltpu.make_async_copy(yb_hbm.at[pl.ds(0, GROUP_ROWS), :], ybuf.at[slot], sem.at[slot]).wait()
    c_iota = lax.broadcasted_iota(jnp.int32, (TOK_BLOCK, GROUP_ROWS), 1)
    post = post_ref[...]
    gatet = gatet_ref[...]
    unsort = jnp.zeros((TOK_BLOCK, GROUP_ROWS), F32)
    for k in range(TOP_K):
        unsort = unsort + jnp.where(c_iota == post[:, k:k + 1], gatet[:, k:k + 1], 0.0)
    ffn = _dot(unsort.astype(BF16), ybuf[slot].astype(BF16))
    out_ref[...] = _layer_norm(DN_ALPHA * x1_ref[...] + ffn, g_ref[...], b_ref[...])


def _combine_call(x1, yb, ypiece, post, gatet, ln_g, ln_b):
    t = post.shape[0]
    groups = t // TOK_BLOCK
    grid_spec = pltpu.PrefetchScalarGridSpec(
        num_scalar_prefetch=1,
        grid=(groups,),
        in_specs=[
            pl.BlockSpec((TOK_BLOCK, D_MODEL), lambda g, yp: (g, 0)),
            pl.BlockSpec(memory_space=pl.ANY),
            pl.BlockSpec((TOK_BLOCK, TOP_K), lambda g, yp: (g, 0)),
            pl.BlockSpec((TOK_BLOCK, TOP_K), lambda g, yp: (g, 0)),
            pl.BlockSpec((1, D_MODEL), lambda g, yp: (0, 0)),
            pl.BlockSpec((1, D_MODEL), lambda g, yp: (0, 0)),
        ],
        out_specs=pl.BlockSpec((TOK_BLOCK, D_MODEL), lambda g, yp: (g, 0)),
        scratch_shapes=[pltpu.VMEM((2, GROUP_ROWS, D_MODEL), F32), pltpu.SemaphoreType.DMA((2,))],
    )
    return pl.pallas_call(
        _combine_kernel,
        grid_spec=grid_spec,
        out_shape=jax.ShapeDtypeStruct((t, D_MODEL), F32),
        compiler_params=pltpu.CompilerParams(dimension_semantics=("arbitrary",),
                                             vmem_limit_bytes=VMEM_LIMIT_V7X),
        name="combine",
    )(ypiece, x1, yb, post, gatet, ln_g, ln_b)


def _routing_tables(cnt_pad, n_blocks):
    groups = cnt_pad.shape[0]
    npc = cnt_pad // PIECE
    strip_start = jnp.cumsum(npc, axis=1) - npc
    tot = jnp.sum(npc, axis=0)
    tot_pad = (tot + EXP_PIECES - 1) // EXP_PIECES * EXP_PIECES
    exp_end = jnp.cumsum(tot_pad)
    exp_start = exp_end - tot_pad
    dstart = exp_start[None, :] + jnp.cumsum(npc, axis=0) - npc
    nvb = (exp_end[-1] // EXP_PIECES).astype(jnp.int32)

    blocks = jnp.arange(n_blocks, dtype=jnp.int32)
    bexp = jnp.sum(blocks[:, None] * EXP_PIECES >= exp_end[None, :], axis=1).astype(jnp.int32)
    bexp = jnp.minimum(bexp, N_EXPERTS - 1)
    last = jnp.sum((nvb - 1) * EXP_PIECES >= exp_end).astype(jnp.int32)
    bexp = jnp.where(blocks < nvb, bexp, jnp.minimum(last, N_EXPERTS - 1))

    d = jnp.arange(n_blocks * EXP_PIECES, dtype=jnp.int32)
    e_of = jnp.minimum(jnp.sum(d[:, None] >= exp_end[None, :], axis=1), N_EXPERTS - 1)
    oh_e = (e_of[:, None] == jnp.arange(N_EXPERTS, dtype=jnp.int32)[None, :]).astype(F32)
    row_of = lambda tab: jnp.dot(oh_e, tab.T.astype(F32), precision=lax.Precision.HIGHEST).astype(jnp.int32)
    ds_e = row_of(dstart)
    g_of = jnp.maximum(jnp.sum(ds_e <= d[:, None], axis=1) - 1, 0)
    at_g = g_of[:, None] == jnp.arange(groups, dtype=jnp.int32)[None, :]
    pick = lambda rows: jnp.sum(jnp.where(at_g, rows, 0), axis=1)
    i_of = d - pick(ds_e)
    valid = (i_of >= 0) & (i_of < pick(row_of(npc)))
    src = g_of * GROUP_PIECES + pick(row_of(strip_start)) + i_of
    src_piece = jnp.where(valid, src, 0).astype(jnp.int32)

    j = jnp.arange(GROUP_PIECES, dtype=jnp.int32)
    e_loc = jnp.maximum(jnp.sum(strip_start[:, None, :] <= j[None, :, None], axis=2) - 1, 0)
    at_e = e_loc[:, :, None] == jnp.arange(N_EXPERTS, dtype=jnp.int32)[None, None, :]
    pick_e = lambda tab: jnp.sum(jnp.where(at_e, tab[:, None, :], 0), axis=2)
    st_loc = pick_e(strip_start)
    np_loc = pick_e(npc)
    ds_loc = pick_e(dstart)
    i_loc = j[None, :] - st_loc
    ypiece = jnp.where(i_loc < np_loc, ds_loc + i_loc, 0).astype(jnp.int32).reshape(groups * GROUP_PIECES)
    return bexp, src_piece, nvb.reshape(1), ypiece


def _tables(seq):
    half = RET_QK_DIM // 2
    inv = ROPE_BASE ** (-jnp.arange(half, dtype=F32) / half)
    ang = jnp.arange(seq, dtype=jnp.int32).astype(F32)[:, None] * inv[None, :]
    cos, sin = jnp.cos(ang), jnp.sin(ang)
    cosf = jnp.concatenate([cos, cos], axis=1)
    sinf = jnp.concatenate([-sin, sin], axis=1)
    kscale = RET_QK_DIM ** -0.5
    log_g = jnp.log(1.0 - jnp.exp(jnp.linspace(math.log(1.0 / 32), math.log(1.0 / 512), RET_HEADS)))
    idx = jnp.arange(TOK_BLOCK, dtype=F32)
    diff = idx[:, None] - idx[None, :]
    dmat = jnp.where(diff >= 0, jnp.exp(log_g[:, None, None] * jnp.maximum(diff, 0.0)), 0.0)
    qdec = jnp.exp(log_g[:, None] * (idx[None, :] + 1.0))
    kdec = jnp.exp(log_g[:, None] * (TOK_BLOCK - 1.0 - idx[None, :]))
    bc = lambda a: jnp.broadcast_to(a[:, :, None], (RET_HEADS, TOK_BLOCK, RET_QK_DIM))
    log_g_host = np.log(1.0 - np.exp(np.linspace(math.log(1.0 / 32), math.log(1.0 / 512), RET_HEADS)))
    chunk_decay = tuple(float(np.float32(np.exp(np.float32(lg) * np.float32(TOK_BLOCK)))) for lg in log_g_host)
    return {"cq": cosf, "sq": sinf, "ck": cosf * kscale, "sk": sinf * kscale,
            "dmat": dmat.astype(F32), "qdec": bc(qdec), "kdec": bc(kdec), "chunk_decay": chunk_decay}


def kernel(x, w_in, p_ret, sg_ln_g, sg_ln_b, sg_w, sg_b, p_sg, w_o, ln1_g, ln1_b,
           router_w, router_b, e_w1, e_b1, e_w2, e_b2, ln2_g, ln2_b):
    batch, seq, d = x.shape
    assert d == D_MODEL and seq % TOK_BLOCK == 0
    depth = w_in.shape[0]
    t = batch * seq
    groups = t // TOK_BLOCK
    n_blocks = (groups * GROUP_PIECES + N_EXPERTS * (EXP_PIECES - 1)) // EXP_PIECES + 1
    tables = _tables(seq)
    xc = x.reshape(t, D_MODEL)
    w_in_b, p_ret_b, p_sg_b, w_o_b = (w.astype(BF16) for w in (w_in, p_ret, p_sg, w_o))
    e_b1r = e_b1.reshape(depth, N_EXPERTS, 1, 2 * D_EXPERT)
    e_b2r = e_b2.reshape(depth, N_EXPERTS, 1, D_MODEL)
    for l in range(depth):
        lw = {
            "w_in": w_in_b, "p_ret": p_ret_b, "p_sg": p_sg_b,
            "w_o": w_o_b, "sg_w": sg_w[l], "sg_bt": sg_b[l].T,
            "sg_ln_g": sg_ln_g[l][None], "sg_ln_b": sg_ln_b[l][None],
            "ln1_g": ln1_g[l][None], "ln1_b": ln1_b[l][None],
            "router_wt": router_w[l].T,
            "router_bb": jnp.broadcast_to(router_b[l][:, None], (N_EXPERTS, TOK_BLOCK)),
        }
        x1, pos, gates, cnt, xs = _mixer_call(xc, batch, seq, l, lw, tables)
        bexp, src_piece, nvb, ypiece = _routing_tables(cnt[:, :, 0], n_blocks)
        yb = _expert_call(xs, bexp, src_piece, nvb, l, e_w1, e_b1r, e_w2, e_b2r, n_blocks)
        xc = _combine_call(x1, yb, ypiece, pos.T, gates.T, ln2_g[l][None], ln2_b[l][None])
    return xc.reshape(batch, seq, D_MODEL)
```

```python
import functools
import math

import jax
import jax.numpy as jnp
import numpy as np
from jax import lax
from jax.experimental import pallas as pl
from jax.experimental.pallas import tpu as pltpu

F32 = jnp.float32
BF16 = jnp.bfloat16

D_MODEL = 1024
DEPTH = 4
RET_HEADS = 4
RET_QK_DIM = 128
RET_V_DIM = 256
RET_QK = RET_HEADS * RET_QK_DIM
RET_V = RET_HEADS * RET_V_DIM
ROPE_BASE = 10000.0
SG_BLOCK = 128
SG_CHUNK = 64
SG_GROUPS = 8
SG_WIDTH = D_MODEL
N_EXPERTS = 32
TOP_K = 4
D_EXPERT = D_MODEL
SWIGLU_LIMIT = 7.0
SWIGLU_ALPHA = 1.702
LN_EPS = 1e-5
DN_ALPHA = (2 * DEPTH) ** 0.25

OFF_Q = 0
OFF_K = OFF_Q + RET_QK
OFF_V = OFF_K + RET_QK
OFF_G = OFF_V + RET_V
OFF_U = OFF_G + RET_V
OFF_VS = OFF_U + SG_WIDTH
OFF_GA = OFF_VS + SG_WIDTH
OFF_GB = OFF_GA + D_MODEL
IN_WIDTH = OFF_GB + D_MODEL

SUBLANES_V7X = 8
TOK_BLOCK = 256
PIECE = SUBLANES_V7X
GROUP_ROWS = TOK_BLOCK * TOP_K + N_EXPERTS * PIECE
GROUP_PIECES = GROUP_ROWS // PIECE
SORT_ROWS = 256
EXP_BLOCK = 512
EXP_PIECES = EXP_BLOCK // PIECE
F_CHUNK = 512
W_CAST_ROWS = 128
VMEM_LIMIT_V7X = 56 * 1024 * 1024


def _layer_norm(x, g, b):
    mu = jnp.mean(x, axis=-1, keepdims=True)
    xc = x - mu
    var = jnp.mean(xc * xc, axis=-1, keepdims=True)
    return xc * lax.rsqrt(var + LN_EPS) * g + b


def _gelu_tanh(x):
    c = math.sqrt(2.0 / math.pi)
    return 0.5 * x * (1.0 + jnp.tanh(c * (x + 0.044715 * (x * x * x))))


def _dot(a, b):
    return jnp.dot(a, b, preferred_element_type=F32)


def _dot_nt(a, b):
    return lax.dot_general(a, b, (((1,), (1,)), ((), ())), preferred_element_type=F32)


def _dot_tn(a, b):
    return lax.dot_general(a, b, (((0,), (0,)), ((), ())), preferred_element_type=F32)


def _split_bf16(x):
    hi = x.astype(BF16)
    lo = (x - hi.astype(F32)).astype(BF16)
    return hi, lo


def _mixer_kernel(x_ref, w_in_ref, p_ret_ref, p_sg_ref, w_o_ref, sgw_ref, sgbt_ref,
                  sglng_ref, sglnb_ref, ln1g_ref, ln1b_ref, rwt_ref, rbb_ref,
                  cq_ref, sq_ref, ck_ref, sk_ref, dmat_ref, qdec_ref, kdec_ref,
                  x1_ref, pos_ref, gate_ref, cnt_ref, xs_ref,
                  state_ref, ysg_ref, x1_prev_ref, *, chunk_decay, steps_per_seq):
    ts = TOK_BLOCK
    step = pl.program_id(0)

    @pl.when(step == 0)
    def _():
        x1_prev_ref[...] = jnp.zeros_like(x1_prev_ref)

    @pl.when(step % steps_per_seq == 0)
    def _():
        state_ref[...] = jnp.zeros_like(state_ref)

    routing = _route_and_sort(x1_prev_ref[...], rwt_ref, rbb_ref, pos_ref, gate_ref, cnt_ref, xs_ref)
    routing_stage = lambda: next(routing, None)

    xb = x_ref[...]
    xbf = xb.astype(BF16)

    def proj(off, width):
        return _dot(xbf, w_in_ref[0, :, off:off + width])

    cq, sq, ck, sk = cq_ref[...], sq_ref[...], ck_ref[...], sk_ref[...]
    acc_ret = jnp.zeros((ts, D_MODEL), F32)
    for h in range(RET_HEADS):
        if h % 2 == 0:
            q2 = proj(OFF_Q + h * RET_QK_DIM, 2 * RET_QK_DIM)
            k2 = proj(OFF_K + h * RET_QK_DIM, 2 * RET_QK_DIM)
        lanes = slice((h % 2) * RET_QK_DIM, (h % 2 + 1) * RET_QK_DIM)
        q, k = q2[:, lanes], k2[:, lanes]
        routing_stage()
        v = proj(OFF_V + h * RET_V_DIM, RET_V_DIM).astype(BF16)
        qr = q * cq + pltpu.roll(q, RET_QK_DIM // 2, 1) * sq
        kr = k * ck + pltpu.roll(k, RET_QK_DIM // 2, 1) * sk
        scores = _dot_nt(qr.astype(BF16), kr.astype(BF16)) * dmat_ref[h]
        inner = _dot(scores.astype(BF16), v)
        st = state_ref[h]
        cross = _dot((qr * qdec_ref[h]).astype(BF16), st.astype(BF16))
        kd = (kr * kdec_ref[h]).astype(BF16)
        state_ref[h] = st * chunk_decay[h] + _dot_tn(kd, v)
        routing_stage()
        y = inner + cross
        mu = jnp.mean(y, axis=-1, keepdims=True)
        yc = y - mu
        var = jnp.mean(yc * yc, axis=-1, keepdims=True)
        yn = yc * lax.rsqrt(var + LN_EPS)
        g = proj(OFF_G + h * RET_V_DIM, RET_V_DIM)
        yr = yn * (g * jax.nn.sigmoid(g))
        acc_ret = acc_ret + _dot(yr.astype(BF16), p_ret_ref[0, h * RET_V_DIM:(h + 1) * RET_V_DIM, :])

    pre_u = proj(OFF_U, SG_WIDTH)
    pre_vs = proj(OFF_VS, SG_WIDTH)
    ga = proj(OFF_GA, D_MODEL)
    u = _gelu_tanh(pre_u)
    routing_stage()
    gb = proj(OFF_GB, D_MODEL)
    vn = _layer_norm(_gelu_tanh(pre_vs), sglng_ref[...], sglnb_ref[...])
    qi = lax.broadcasted_iota(jnp.int32, (SG_BLOCK, SG_BLOCK), 0) // SG_CHUNK
    pi = lax.broadcasted_iota(jnp.int32, (SG_BLOCK, SG_BLOCK), 1) // SG_CHUNK
    sg_mask = qi >= pi
    for gi in range(SG_GROUPS):
        wm = jnp.where(sg_mask, sgw_ref[gi], 0.0).astype(BF16)
        bias = sgbt_ref[:, gi:gi + 1]
        cols = slice(gi * SG_BLOCK, (gi + 1) * SG_BLOCK)
        blocks = [slice(blk * SG_BLOCK, (blk + 1) * SG_BLOCK) for blk in range(ts // SG_BLOCK)]
        vcat = jnp.concatenate([vn[rows, cols] for rows in blocks], axis=1).astype(BF16)
        mixed = _dot(wm, vcat) + bias
        for blk, rows in enumerate(blocks):
            ysg_ref[rows, cols] = (u[rows, cols] * mixed[:, blk * SG_BLOCK:(blk + 1) * SG_BLOCK]).astype(BF16)
    routing_stage()
    acc_sg = _dot(ysg_ref[...], p_sg_ref[0])

    hmix = jax.nn.sigmoid(ga) * acc_ret + jax.nn.sigmoid(gb) * acc_sg
    mix = _dot(hmix.astype(BF16), w_o_ref[0])
    for _ in routing:
        pass
    x1 = _layer_norm(DN_ALPHA * xb + mix, ln1g_ref[...], ln1b_ref[...])
    x1_ref[...] = x1
    x1_prev_ref[...] = x1


def _route_and_sort(x1, rwt_ref, rbb_ref, pos_ref, gate_ref, cnt_ref, xs_ref):
    ts = TOK_BLOCK
    x_hi, x_lo = _split_bf16(x1)
    w_hi, w_lo = _split_bf16(rwt_ref[...])
    logits = (_dot_nt(w_hi, x_hi) + _dot_nt(w_hi, x_lo) + _dot_nt(w_lo, x_hi)) + rbb_ref[...]
    yield
    e_iota = lax.broadcasted_iota(jnp.int32, (N_EXPERTS, ts), 0).astype(F32)
    sels, vals = [], []
    work = logits
    for _ in range(TOP_K):
        m = jnp.max(work, axis=0, keepdims=True)
        idx = jnp.min(jnp.where(work == m, e_iota, float(N_EXPERTS)), axis=0, keepdims=True)
        sel = e_iota == idx
        work = jnp.where(sel, -jnp.inf, work)
        sels.append(sel)
        vals.append(m)
        yield
    exps = [jnp.exp(v - vals[0]) for v in vals]
    denom = exps[0] + exps[1] + exps[2] + exps[3]
    gate_ref[...] = jnp.concatenate([e / denom for e in exps], axis=0)

    onehots = [jnp.where(s, 1.0, 0.0) for s in sels]
    oh_sum = onehots[0] + onehots[1] + onehots[2] + onehots[3]
    t_row = lax.broadcasted_iota(jnp.int32, (ts, ts), 0)
    t_col = lax.broadcasted_iota(jnp.int32, (ts, ts), 1)
    upper = jnp.where(t_row < t_col, 1.0, 0.0).astype(BF16)
    before = _dot(oh_sum.astype(BF16), upper)
    cnt = jnp.sum(oh_sum, axis=1, keepdims=True)
    cnt_pad = jnp.floor((cnt + (PIECE - 1)) * (1.0 / PIECE)) * PIECE
    cnt_b = jnp.broadcast_to(cnt_pad, (N_EXPERTS, ts))
    e_row = lax.broadcasted_iota(jnp.int32, (N_EXPERTS, N_EXPERTS), 0)
    e_col = lax.broadcasted_iota(jnp.int32, (N_EXPERTS, N_EXPERTS), 1)
    lower = jnp.where(e_col < e_row, 1.0, 0.0).astype(BF16)
    strip_off = _dot(lower, cnt_b.astype(BF16))
    slot = strip_off + before
    pos = [jnp.sum(oh * slot, axis=0, keepdims=True).astype(jnp.int32) for oh in onehots]
    pos_ref[...] = jnp.concatenate(pos, axis=0)
    cnt_ref[0] = cnt_b[:, :128].astype(jnp.int32)
    yield
    for r0 in range(0, GROUP_ROWS, SORT_ROWS):
        r_iota = lax.broadcasted_iota(jnp.int32, (SORT_ROWS, ts), 0) + r0
        hit = (r_iota == pos[0]) | (r_iota == pos[1]) | (r_iota == pos[2]) | (r_iota == pos[3])
        perm = jnp.where(hit, 1.0, 0.0).astype(BF16)
        xs_ref[r0:r0 + SORT_ROWS, :] = _dot(perm, x_hi)
        yield


def _mixer_call(x2d, batch, seq, layer, lw, tables):
    t = batch * seq
    ns = seq // TOK_BLOCK
    groups = t // TOK_BLOCK
    const = lambda *shape: pl.BlockSpec(shape, lambda i: (0,) * len(shape),
                                        pipeline_mode=pl.Buffered(1))
    stacked = lambda *shape: pl.BlockSpec((1,) + shape, lambda i: (layer, 0, 0),
                                          pipeline_mode=pl.Buffered(1))
    cur = lambda i: jnp.minimum(i, groups - 1)
    prev = lambda i: jnp.maximum(i - 1, 0)
    seqtab = pl.BlockSpec((TOK_BLOCK, RET_QK_DIM), lambda i: (cur(i) % ns, 0))
    lanes = lambda rows: pl.BlockSpec((rows, TOK_BLOCK), lambda i: (0, prev(i)))
    in_specs = [
        pl.BlockSpec((TOK_BLOCK, D_MODEL), lambda i: (cur(i), 0)),
        stacked(D_MODEL, IN_WIDTH), stacked(RET_V, D_MODEL), stacked(SG_WIDTH, D_MODEL),
        stacked(D_MODEL, D_MODEL),
        const(SG_GROUPS, SG_BLOCK, SG_BLOCK), const(SG_BLOCK, SG_GROUPS),
        const(1, SG_WIDTH), const(1, SG_WIDTH), const(1, D_MODEL), const(1, D_MODEL),
        const(N_EXPERTS, D_MODEL), const(N_EXPERTS, TOK_BLOCK),
        seqtab, seqtab, seqtab, seqtab,
        const(RET_HEADS, TOK_BLOCK, TOK_BLOCK), const(RET_HEADS, TOK_BLOCK, RET_QK_DIM),
        const(RET_HEADS, TOK_BLOCK, RET_QK_DIM),
    ]
    out_shape = [
        jax.ShapeDtypeStruct((t + TOK_BLOCK, D_MODEL), F32),
        jax.ShapeDtypeStruct((TOP_K, t), jnp.int32),
        jax.ShapeDtypeStruct((TOP_K, t), F32),
        jax.ShapeDtypeStruct((groups, N_EXPERTS, 128), jnp.int32),
        jax.ShapeDtypeStruct((groups * GROUP_ROWS, D_MODEL), F32),
    ]
    out_specs = [
        pl.BlockSpec((TOK_BLOCK, D_MODEL), lambda i: (i, 0)), lanes(TOP_K), lanes(TOP_K),
        pl.BlockSpec((1, N_EXPERTS, 128), lambda i: (prev(i), 0, 0)),
        pl.BlockSpec((GROUP_ROWS, D_MODEL), lambda i: (prev(i), 0)),
    ]
    kern = functools.partial(_mixer_kernel, chunk_decay=tables["chunk_decay"], steps_per_seq=ns)
    return pl.pallas_call(
        kern,
        grid=(groups + 1,),
        in_specs=in_specs,
        out_specs=out_specs,
        out_shape=out_shape,
        scratch_shapes=[pltpu.VMEM((RET_HEADS, RET_QK_DIM, RET_V_DIM), F32),
                        pltpu.VMEM((TOK_BLOCK, SG_WIDTH), BF16),
                        pltpu.VMEM((TOK_BLOCK, D_MODEL), F32)],
        compiler_params=pltpu.CompilerParams(dimension_semantics=("arbitrary",),
                                             vmem_limit_bytes=VMEM_LIMIT_V7X),
        name="mixer",
    )(x2d, lw["w_in"], lw["p_ret"], lw["p_sg"], lw["w_o"], lw["sg_w"], lw["sg_bt"],
      lw["sg_ln_g"], lw["sg_ln_b"], lw["ln1_g"], lw["ln1_b"], lw["router_wt"], lw["router_bb"],
      tables["cq"], tables["sq"], tables["ck"], tables["sk"],
      tables["dmat"], tables["qdec"], tables["kdec"])


def _expert_kernel(bexp_ref, src_ref, nvb_ref, xs_hbm, w1_ref, b1_ref, w2_ref, b2_ref,
                   yb_ref, xbuf0, xbuf1, w1b, w2b, sem):
    b = pl.program_id(0)
    nvb = nvb_ref[0]
    xbufs = (xbuf0, xbuf1)

    def gather(blk, slot, pieces=range(EXP_PIECES)):
        for i in pieces:
            row = pl.multiple_of(src_ref[blk * EXP_PIECES + i] * PIECE, PIECE)
            pltpu.make_async_copy(xs_hbm.at[pl.ds(row, PIECE), :],
                                  xbufs[slot].at[pl.ds(i * PIECE, PIECE), :],
                                  sem.at[slot]).start()

    def wait(slot):
        pltpu.make_async_copy(xs_hbm.at[pl.ds(0, EXP_BLOCK), :], xbufs[slot], sem.at[slot]).wait()

    @pl.when(b == 0)
    def _():
        gather(0, 0)

    new_expert = (b == 0) | (bexp_ref[b] != bexp_ref[jnp.maximum(b - 1, 0)])

    @pl.when(new_expert & (b < nvb))
    def _():
        def cast_rows(i, carry):
            r = pl.multiple_of(i * W_CAST_ROWS, W_CAST_ROWS)
            w1b[pl.ds(r, W_CAST_ROWS), :] = w1_ref[0, 0, pl.ds(r, W_CAST_ROWS), :].astype(BF16)
            w2b[pl.ds(r, W_CAST_ROWS), :] = w2_ref[0, 0, pl.ds(r, W_CAST_ROWS), :].astype(BF16)
            return carry
        lax.fori_loop(0, D_MODEL // W_CAST_ROWS, cast_rows, 0)

    def block(cur):
        nxt = 1 - cur
        wait(cur)
        nblk = jnp.minimum(b + 1, nvb - 1)
        n_chunks = D_EXPERT // F_CHUNK
        per_batch = -(-EXP_PIECES // (3 * n_chunks))
        batches = [range(s, min(s + per_batch, EXP_PIECES)) for s in range(0, EXP_PIECES, per_batch)]
        next_batch = lambda: gather(nblk, nxt, batches.pop(0)) if batches else None
        x = xbufs[cur][...].astype(BF16)
        y = jnp.zeros((EXP_BLOCK, D_MODEL), F32)
        for j in range(n_chunks):
            c0 = j * F_CHUNK
            next_batch()
            hg = _dot(x, w1b[:, c0:c0 + F_CHUNK]) + b1_ref[0, 0, :, c0:c0 + F_CHUNK]
            next_batch()
            hu = (_dot(x, w1b[:, D_EXPERT + c0:D_EXPERT + c0 + F_CHUNK])
                  + b1_ref[0, 0, :, D_EXPERT + c0:D_EXPERT + c0 + F_CHUNK])
            gate = jnp.minimum(hg, SWIGLU_LIMIT)
            up = jnp.clip(hu, -SWIGLU_LIMIT, SWIGLU_LIMIT)
            act = gate * jax.nn.sigmoid(SWIGLU_ALPHA * gate) * (up + 1.0)
            next_batch()
            y = y + _dot(act.astype(BF16), w2b[c0:c0 + F_CHUNK, :])
        while batches:
            next_batch()
        yb_ref[...] = y + b2_ref[0, 0]

        @pl.when(b == nvb - 1)
        def _():
            wait(nxt)

    for cur in range(2):
        pl.when((b < nvb) & (b % 2 == cur))(functools.partial(block, cur))

    @pl.when(b >= nvb)
    def _():
        yb_ref[...] = jnp.zeros_like(yb_ref)


def _expert_call(xs, block_exp, src_piece, nvb, layer, e_w1, e_b1, e_w2, e_b2, n_blocks):
    wspec = lambda *shape: pl.BlockSpec((1, 1) + shape, lambda b, be, sp, nv: (layer, be[b], 0, 0))
    grid_spec = pltpu.PrefetchScalarGridSpec(
        num_scalar_prefetch=3,
        grid=(n_blocks,),
        in_specs=[
            pl.BlockSpec(memory_space=pl.ANY),
            wspec(D_MODEL, 2 * D_EXPERT), wspec(1, 2 * D_EXPERT),
            wspec(D_EXPERT, D_MODEL), wspec(1, D_MODEL),
        ],
        out_specs=pl.BlockSpec((EXP_BLOCK, D_MODEL), lambda b, be, sp, nv: (b, 0)),
        scratch_shapes=[pltpu.VMEM((EXP_BLOCK, D_MODEL), F32),
                        pltpu.VMEM((EXP_BLOCK, D_MODEL), F32),
                        pltpu.VMEM((D_MODEL, 2 * D_EXPERT), BF16),
                        pltpu.VMEM((D_EXPERT, D_MODEL), BF16),
                        pltpu.SemaphoreType.DMA((2,))],
    )
    return pl.pallas_call(
        _expert_kernel,
        grid_spec=grid_spec,
        out_shape=jax.ShapeDtypeStruct((n_blocks * EXP_BLOCK, D_MODEL), F32),
        compiler_params=pltpu.CompilerParams(dimension_semantics=("arbitrary",),
                                             vmem_limit_bytes=VMEM_LIMIT_V7X),
        name="experts",
    )(block_exp, src_piece, nvb, xs, e_w1, e_b1, e_w2, e_b2)


def _combine_kernel(yp_ref, x1_ref, yb_hbm, post_ref, gatet_ref, g_ref, b_ref, out_ref,
                    ybuf0, ybuf1, sem):
    g = pl.program_id(0)
    ng = pl.num_programs(0)
    ybufs = (ybuf0, ybuf1)

    def gather(grp, slot, pieces=range(GROUP_PIECES)):
        for i in pieces:
            row = pl.multiple_of(yp_ref[grp * GROUP_PIECES + i] * PIECE, PIECE)
            pltpu.make_async_copy(yb_hbm.at[pl.ds(row, PIECE), :],
                                  ybufs[slot].at[pl.ds(i * PIECE, PIECE), :],
                                  sem.at[slot]).start()

    def wait(slot):
        pltpu.make_async_copy(yb_hbm.at[pl.ds(0, GROUP_ROWS), :], ybufs[slot], sem.at[slot]).wait()

    @pl.when(g == 0)
    def _():
        gather(0, 0)

    def group(cur):
        nxt = 1 - cur
        wait(cur)
        ngrp = jnp.minimum(g + 1, ng - 1)
        post = post_ref[...]
        gatet = gatet_ref[...]
        ffn = jnp.zeros((TOK_BLOCK, D_MODEL), F32)
        for r0 in range(0, GROUP_ROWS, SORT_ROWS):
            gather(ngrp, nxt, range(r0 // PIECE, (r0 + SORT_ROWS) // PIECE))
            c_iota = lax.broadcasted_iota(jnp.int32, (TOK_BLOCK, SORT_ROWS), 1) + r0
            unsort = jnp.zeros((TOK_BLOCK, SORT_ROWS), F32)
            for k in range(TOP_K):
                unsort = unsort + jnp.where(c_iota == post[:, k:k + 1], gatet[:, k:k + 1], 0.0)
            ffn = ffn + _dot(unsort.astype(BF16), ybufs[cur][r0:r0 + SORT_ROWS, :].astype(BF16))
        out_ref[...] = _layer_norm(DN_ALPHA * x1_ref[...] + ffn, g_ref[...], b_ref[...])

        @pl.when(g == ng - 1)
        def _():
            wait(nxt)

    for cur in range(2):
        pl.when(g % 2 == cur)(functools.partial(group, cur))


def _combine_call(x1, yb, ypiece, post, gatet, ln_g, ln_b):
    t = post.shape[0]
    groups = t // TOK_BLOCK
    grid_spec = pltpu.PrefetchScalarGridSpec(
        num_scalar_prefetch=1,
        grid=(groups,),
        in_specs=[
            pl.BlockSpec((TOK_BLOCK, D_MODEL), lambda g, yp: (g, 0)),
            pl.BlockSpec(memory_space=pl.ANY),
            pl.BlockSpec((TOK_BLOCK, TOP_K), lambda g, yp: (g, 0)),
            pl.BlockSpec((TOK_BLOCK, TOP_K), lambda g, yp: (g, 0)),
            pl.BlockSpec((1, D_MODEL), lambda g, yp: (0, 0)),
            pl.BlockSpec((1, D_MODEL), lambda g, yp: (0, 0)),
        ],
        out_specs=pl.BlockSpec((TOK_BLOCK, D_MODEL), lambda g, yp: (g, 0)),
        scratch_shapes=[pltpu.VMEM((GROUP_ROWS, D_MODEL), F32), pltpu.VMEM((GROUP_ROWS, D_MODEL), F32),
                        pltpu.SemaphoreType.DMA((2,))],
    )
    return pl.pallas_call(
        _combine_kernel,
        grid_spec=grid_spec,
        out_shape=jax.ShapeDtypeStruct((t, D_MODEL), F32),
        compiler_params=pltpu.CompilerParams(dimension_semantics=("arbitrary",),
                                             vmem_limit_bytes=VMEM_LIMIT_V7X),
        name="combine",
    )(ypiece, x1, yb, post, gatet, ln_g, ln_b)


def _routing_tables(cnt_pad, n_blocks):
    groups = cnt_pad.shape[0]
    npc = cnt_pad // PIECE
    strip_start = jnp.cumsum(npc, axis=1) - npc
    tot = jnp.sum(npc, axis=0)
    tot_pad = (tot + EXP_PIECES - 1) // EXP_PIECES * EXP_PIECES
    exp_end = jnp.cumsum(tot_pad)
    exp_start = exp_end - tot_pad
    dstart = exp_start[None, :] + jnp.cumsum(npc, axis=0) - npc
    nvb = (exp_end[-1] // EXP_PIECES).astype(jnp.int32)

    blocks = jnp.arange(n_blocks, dtype=jnp.int32)
    bexp = jnp.sum(blocks[:, None] * EXP_PIECES >= exp_end[None, :], axis=1).astype(jnp.int32)
    bexp = jnp.minimum(bexp, N_EXPERTS - 1)
    last = jnp.sum((nvb - 1) * EXP_PIECES >= exp_end).astype(jnp.int32)
    bexp = jnp.where(blocks < nvb, bexp, jnp.minimum(last, N_EXPERTS - 1))

    d = jnp.arange(n_blocks * EXP_PIECES, dtype=jnp.int32)
    e_of = jnp.minimum(jnp.sum(d[:, None] >= exp_end[None, :], axis=1), N_EXPERTS - 1)
    oh_e = (e_of[:, None] == jnp.arange(N_EXPERTS, dtype=jnp.int32)[None, :]).astype(F32)
    row_of = lambda tab: jnp.dot(oh_e, tab.T.astype(F32), precision=lax.Precision.HIGHEST).astype(jnp.int32)
    ds_e = row_of(dstart)
    g_of = jnp.maximum(jnp.sum(ds_e <= d[:, None], axis=1) - 1, 0)
    at_g = g_of[:, None] == jnp.arange(groups, dtype=jnp.int32)[None, :]
    pick = lambda rows: jnp.sum(jnp.where(at_g, rows, 0), axis=1)
    i_of = d - pick(ds_e)
    valid = (i_of >= 0) & (i_of < pick(row_of(npc)))
    src = g_of * GROUP_PIECES + pick(row_of(strip_start)) + i_of
    src_piece = jnp.where(valid, src, 0).astype(jnp.int32)

    j = jnp.arange(GROUP_PIECES, dtype=jnp.int32)
    e_loc = jnp.maximum(jnp.sum(strip_start[:, None, :] <= j[None, :, None], axis=2) - 1, 0)
    at_e = e_loc[:, :, None] == jnp.arange(N_EXPERTS, dtype=jnp.int32)[None, None, :]
    pick_e = lambda tab: jnp.sum(jnp.where(at_e, tab[:, None, :], 0), axis=2)
    st_loc = pick_e(strip_start)
    np_loc = pick_e(npc)
    ds_loc = pick_e(dstart)
    i_loc = j[None, :] - st_loc
    ypiece = jnp.where(i_loc < np_loc, ds_loc + i_loc, 0).astype(jnp.int32).reshape(groups * GROUP_PIECES)
    return bexp, src_piece, nvb.reshape(1), ypiece


def _tables(seq):
    half = RET_QK_DIM // 2
    inv = ROPE_BASE ** (-jnp.arange(half, dtype=F32) / half)
    ang = jnp.arange(seq, dtype=jnp.int32).astype(F32)[:, None] * inv[None, :]
    cos, sin = jnp.cos(ang), jnp.sin(ang)
    cosf = jnp.concatenate([cos, cos], axis=1)
    sinf = jnp.concatenate([-sin, sin], axis=1)
    kscale = RET_QK_DIM ** -0.5
    log_g = jnp.log(1.0 - jnp.exp(jnp.linspace(math.log(1.0 / 32), math.log(1.0 / 512), RET_HEADS)))
    idx = jnp.arange(TOK_BLOCK, dtype=F32)
    diff = idx[:, None] - idx[None, :]
    dmat = jnp.where(diff >= 0, jnp.exp(log_g[:, None, None] * jnp.maximum(diff, 0.0)), 0.0)
    qdec = jnp.exp(log_g[:, None] * (idx[None, :] + 1.0))
    kdec = jnp.exp(log_g[:, None] * (TOK_BLOCK - 1.0 - idx[None, :]))
    bc = lambda a: jnp.broadcast_to(a[:, :, None], (RET_HEADS, TOK_BLOCK, RET_QK_DIM))
    log_g_host = np.log(1.0 - np.exp(np.linspace(math.log(1.0 / 32), math.log(1.0 / 512), RET_HEADS)))
    chunk_decay = tuple(float(np.float32(np.exp(np.float32(lg) * np.float32(TOK_BLOCK)))) for lg in log_g_host)
    return {"cq": cosf, "sq": sinf, "ck": cosf * kscale, "sk": sinf * kscale,
            "dmat": dmat.astype(F32), "qdec": bc(qdec), "kdec": bc(kdec), "chunk_decay": chunk_decay}


def kernel(x, w_in, p_ret, sg_ln_g, sg_ln_b, sg_w, sg_b, p_sg, w_o, ln1_g, ln1_b,
           router_w, router_b, e_w1, e_b1, e_w2, e_b2, ln2_g, ln2_b):
    batch, seq, d = x.shape
    assert d == D_MODEL and seq % TOK_BLOCK == 0
    depth = w_in.shape[0]
    t = batch * seq
    groups = t // TOK_BLOCK
    n_blocks = (groups * GROUP_PIECES + N_EXPERTS * (EXP_PIECES - 1)) // EXP_PIECES + 1
    tables = _tables(seq)
    xc = x.reshape(t, D_MODEL)
    w_in_b, p_ret_b, p_sg_b, w_o_b = (w.astype(BF16) for w in (w_in, p_ret, p_sg, w_o))
    e_b1r = e_b1.reshape(depth, N_EXPERTS, 1, 2 * D_EXPERT)
    e_b2r = e_b2.reshape(depth, N_EXPERTS, 1, D_MODEL)
    for l in range(depth):
        lw = {
            "w_in": w_in_b, "p_ret": p_ret_b, "p_sg": p_sg_b,
            "w_o": w_o_b, "sg_w": sg_w[l], "sg_bt": sg_b[l].T,
            "sg_ln_g": sg_ln_g[l][None], "sg_ln_b": sg_ln_b[l][None],
            "ln1_g": ln1_g[l][None], "ln1_b": ln1_b[l][None],
            "router_wt": router_w[l].T,
            "router_bb": jnp.broadcast_to(router_b[l][:, None], (N_EXPERTS, TOK_BLOCK)),
        }
        x1, pos, gates, cnt, xs = _mixer_call(xc, batch, seq, l, lw, tables)
        bexp, src_piece, nvb, ypiece = _routing_tables(cnt[:, :, 0], n_blocks)
        yb = _expert_call(xs, bexp, src_piece, nvb, l, e_w1, e_b1r, e_w2, e_b2r, n_blocks)
        xc = _combine_call(x1, yb, ypiece, pos.T, gates.T, ln2_g[l][None], ln2_b[l][None])
    return xc.reshape(batch, seq, D_MODEL)
```

```python
import functools
import math

import jax
import jax.numpy as jnp
import numpy as np
from jax import lax
from jax.experimental import pallas as pl
from jax.experimental.pallas import tpu as pltpu

F32 = jnp.float32
BF16 = jnp.bfloat16

D_MODEL = 1024
DEPTH = 4
RET_HEADS = 4
RET_QK_DIM = 128
RET_V_DIM = 256
RET_QK = RET_HEADS * RET_QK_DIM
RET_V = RET_HEADS * RET_V_DIM
ROPE_BASE = 10000.0
SG_BLOCK = 128
SG_CHUNK = 64
SG_GROUPS = 8
SG_WIDTH = D_MODEL
N_EXPERTS = 32
TOP_K = 4
D_EXPERT = D_MODEL
SWIGLU_LIMIT = 7.0
SWIGLU_ALPHA = 1.702
LN_EPS = 1e-5
DN_ALPHA = (2 * DEPTH) ** 0.25

OFF_Q = 0
OFF_K = OFF_Q + RET_QK
OFF_V = OFF_K + RET_QK
OFF_G = OFF_V + RET_V
OFF_U = OFF_G + RET_V
OFF_VS = OFF_U + SG_WIDTH
OFF_GA = OFF_VS + SG_WIDTH
OFF_GB = OFF_GA + D_MODEL
IN_WIDTH = OFF_GB + D_MODEL

SUBLANES_V7X = 8
TOK_BLOCK = 256
PIECE = SUBLANES_V7X
GROUP_ROWS = TOK_BLOCK * TOP_K + N_EXPERTS * PIECE
GROUP_PIECES = GROUP_ROWS // PIECE
SORT_ROWS = 256
EXP_BLOCK = 512
EXP_PIECES = EXP_BLOCK // PIECE
F_CHUNK = 512
W_CAST_ROWS = 128
VMEM_LIMIT_V7X = 56 * 1024 * 1024


def _layer_norm(x, g, b):
    mu = jnp.mean(x, axis=-1, keepdims=True)
    xc = x - mu
    var = jnp.mean(xc * xc, axis=-1, keepdims=True)
    return xc * lax.rsqrt(var + LN_EPS) * g + b


def _gelu_tanh(x):
    c = math.sqrt(2.0 / math.pi)
    return 0.5 * x * (1.0 + jnp.tanh(c * (x + 0.044715 * (x * x * x))))


def _dot(a, b):
    return jnp.dot(a, b, preferred_element_type=F32)


def _dot_nt(a, b):
    return lax.dot_general(a, b, (((1,), (1,)), ((), ())), preferred_element_type=F32)


def _dot_tn(a, b):
    return lax.dot_general(a, b, (((0,), (0,)), ((), ())), preferred_element_type=F32)


def _split_bf16(x):
    hi = x.astype(BF16)
    lo = (x - hi.astype(F32)).astype(BF16)
    return hi, lo


def _mixer_kernel(x_ref, w_in_ref, p_ret_ref, p_sg_ref, w_o_ref, sgw_ref, sgbt_ref,
                  sglng_ref, sglnb_ref, ln1g_ref, ln1b_ref, rwt_ref, rbb_ref,
                  cq_ref, sq_ref, ck_ref, sk_ref, dmat_ref, qdec_ref, kdec_ref,
                  x1_ref, pos_ref, gate_ref, cnt_ref, xs_ref,
                  state_ref, ysg_ref, x1_prev_ref, *, chunk_decay, steps_per_seq):
    ts = TOK_BLOCK
    step = pl.program_id(0)

    @pl.when(step == 0)
    def _():
        x1_prev_ref[...] = jnp.zeros_like(x1_prev_ref)

    @pl.when(step % steps_per_seq == 0)
    def _():
        state_ref[...] = jnp.zeros_like(state_ref)

    routing = _route_and_sort(x1_prev_ref[...], rwt_ref, rbb_ref, pos_ref, gate_ref, cnt_ref, xs_ref)
    routing_stage = lambda: next(routing, None)

    xb = x_ref[...]
    xbf = xb.astype(BF16)

    def proj(off, width):
        return _dot(xbf, w_in_ref[0, :, off:off + width])

    cq, sq, ck, sk = cq_ref[...], sq_ref[...], ck_ref[...], sk_ref[...]
    acc_ret = jnp.zeros((ts, D_MODEL), F32)
    for h in range(RET_HEADS):
        if h % 2 == 0:
            q2 = proj(OFF_Q + h * RET_QK_DIM, 2 * RET_QK_DIM)
            k2 = proj(OFF_K + h * RET_QK_DIM, 2 * RET_QK_DIM)
        lanes = slice((h % 2) * RET_QK_DIM, (h % 2 + 1) * RET_QK_DIM)
        q, k = q2[:, lanes], k2[:, lanes]
        routing_stage()
        v = proj(OFF_V + h * RET_V_DIM, RET_V_DIM).astype(BF16)
        qr = q * cq + pltpu.roll(q, RET_QK_DIM // 2, 1) * sq
        kr = k * ck + pltpu.roll(k, RET_QK_DIM // 2, 1) * sk
        scores = _dot_nt(qr.astype(BF16), kr.astype(BF16)) * dmat_ref[h]
        inner = _dot(scores.astype(BF16), v)
        st = state_ref[h]
        cross = _dot((qr * qdec_ref[h]).astype(BF16), st.astype(BF16))
        kd = (kr * kdec_ref[h]).astype(BF16)
        state_ref[h] = st * chunk_decay[h] + _dot_tn(kd, v)
        routing_stage()
        y = inner + cross
        mu = jnp.mean(y, axis=-1, keepdims=True)
        yc = y - mu
        var = jnp.mean(yc * yc, axis=-1, keepdims=True)
        yn = yc * lax.rsqrt(var + LN_EPS)
        g = proj(OFF_G + h * RET_V_DIM, RET_V_DIM)
        yr = yn * (g * jax.nn.sigmoid(g))
        acc_ret = acc_ret + _dot(yr.astype(BF16), p_ret_ref[0, h * RET_V_DIM:(h + 1) * RET_V_DIM, :])

    pre_u = proj(OFF_U, SG_WIDTH)
    pre_vs = proj(OFF_VS, SG_WIDTH)
    ga = proj(OFF_GA, D_MODEL)
    u = _gelu_tanh(pre_u)
    routing_stage()
    gb = proj(OFF_GB, D_MODEL)
    vn = _layer_norm(_gelu_tanh(pre_vs), sglng_ref[...], sglnb_ref[...])
    qi = lax.broadcasted_iota(jnp.int32, (SG_BLOCK, SG_BLOCK), 0) // SG_CHUNK
    pi = lax.broadcasted_iota(jnp.int32, (SG_BLOCK, SG_BLOCK), 1) // SG_CHUNK
    sg_mask = qi >= pi
    for gi in range(SG_GROUPS):
        wm = jnp.where(sg_mask, sgw_ref[gi], 0.0).astype(BF16)
        bias = sgbt_ref[:, gi:gi + 1]
        cols = slice(gi * SG_BLOCK, (gi + 1) * SG_BLOCK)
        blocks = [slice(blk * SG_BLOCK, (blk + 1) * SG_BLOCK) for blk in range(ts // SG_BLOCK)]
        vcat = jnp.concatenate([vn[rows, cols] for rows in blocks], axis=1).astype(BF16)
        mixed = _dot(wm, vcat) + bias
        for blk, rows in enumerate(blocks):
            ysg_ref[rows, cols] = (u[rows, cols] * mixed[:, blk * SG_BLOCK:(blk + 1) * SG_BLOCK]).astype(BF16)
    routing_stage()
    acc_sg = _dot(ysg_ref[...], p_sg_ref[0])

    hmix = jax.nn.sigmoid(ga) * acc_ret + jax.nn.sigmoid(gb) * acc_sg
    mix = _dot(hmix.astype(BF16), w_o_ref[0])
    for _ in routing:
        pass
    x1 = _layer_norm(DN_ALPHA * xb + mix, ln1g_ref[...], ln1b_ref[...])
    x1_ref[...] = x1
    x1_prev_ref[...] = x1


def _route_and_sort(x1, rwt_ref, rbb_ref, pos_ref, gate_ref, cnt_ref, xs_ref):
    ts = TOK_BLOCK
    x_hi, x_lo = _split_bf16(x1)
    w_hi, w_lo = _split_bf16(rwt_ref[...])
    logits = (_dot_nt(w_hi, x_hi) + _dot_nt(w_hi, x_lo) + _dot_nt(w_lo, x_hi)) + rbb_ref[...]
    yield
    e_iota = lax.broadcasted_iota(jnp.int32, (N_EXPERTS, ts), 0).astype(F32)
    sels, vals = [], []
    work = logits
    for _ in range(TOP_K):
        m = jnp.max(work, axis=0, keepdims=True)
        idx = jnp.min(jnp.where(work == m, e_iota, float(N_EXPERTS)), axis=0, keepdims=True)
        sel = e_iota == idx
        work = jnp.where(sel, -jnp.inf, work)
        sels.append(sel)
        vals.append(m)
        yield
    exps = [jnp.exp(v - vals[0]) for v in vals]
    denom = exps[0] + exps[1] + exps[2] + exps[3]
    gate_ref[...] = jnp.concatenate([e / denom for e in exps], axis=0)

    onehots = [jnp.where(s, 1.0, 0.0) for s in sels]
    oh_sum = onehots[0] + onehots[1] + onehots[2] + onehots[3]
    t_row = lax.broadcasted_iota(jnp.int32, (ts, ts), 0)
    t_col = lax.broadcasted_iota(jnp.int32, (ts, ts), 1)
    upper = jnp.where(t_row < t_col, 1.0, 0.0).astype(BF16)
    before = _dot(oh_sum.astype(BF16), upper)
    cnt = jnp.sum(oh_sum, axis=1, keepdims=True)
    cnt_pad = jnp.floor((cnt + (PIECE - 1)) * (1.0 / PIECE)) * PIECE
    cnt_b = jnp.broadcast_to(cnt_pad, (N_EXPERTS, ts))
    e_row = lax.broadcasted_iota(jnp.int32, (N_EXPERTS, N_EXPERTS), 0)
    e_col = lax.broadcasted_iota(jnp.int32, (N_EXPERTS, N_EXPERTS), 1)
    lower = jnp.where(e_col < e_row, 1.0, 0.0).astype(BF16)
    strip_off = _dot(lower, cnt_b.astype(BF16))
    slot = strip_off + before
    pos = [jnp.sum(oh * slot, axis=0, keepdims=True).astype(jnp.int32) for oh in onehots]
    pos_ref[...] = jnp.concatenate(pos, axis=0)
    cnt_ref[0] = cnt_b[:, :128].astype(jnp.int32)
    yield
    for r0 in range(0, GROUP_ROWS, SORT_ROWS):
        r_iota = lax.broadcasted_iota(jnp.int32, (SORT_ROWS, ts), 0) + r0
        hit = (r_iota == pos[0]) | (r_iota == pos[1]) | (r_iota == pos[2]) | (r_iota == pos[3])
        perm = jnp.where(hit, 1.0, 0.0).astype(BF16)
        xs_ref[r0:r0 + SORT_ROWS, :] = _dot(perm, x_hi)
        yield


def _mixer_call(x2d, batch, seq, layer, lw, tables):
    t = batch * seq
    ns = seq // TOK_BLOCK
    groups = t // TOK_BLOCK
    const = lambda *shape: pl.BlockSpec(shape, lambda i: (0,) * len(shape),
                                        pipeline_mode=pl.Buffered(1))
    stacked = lambda *shape: pl.BlockSpec((1,) + shape, lambda i: (layer, 0, 0),
                                          pipeline_mode=pl.Buffered(1))
    cur = lambda i: jnp.minimum(i, groups - 1)
    prev = lambda i: jnp.maximum(i - 1, 0)
    seqtab = pl.BlockSpec((TOK_BLOCK, RET_QK_DIM), lambda i: (cur(i) % ns, 0))
    lanes = lambda rows: pl.BlockSpec((rows, TOK_BLOCK), lambda i: (0, prev(i)))
    in_specs = [
        pl.BlockSpec((TOK_BLOCK, D_MODEL), lambda i: (cur(i), 0)),
        stacked(D_MODEL, IN_WIDTH), stacked(RET_V, D_MODEL), stacked(SG_WIDTH, D_MODEL),
        stacked(D_MODEL, D_MODEL),
        const(SG_GROUPS, SG_BLOCK, SG_BLOCK), const(SG_BLOCK, SG_GROUPS),
        const(1, SG_WIDTH), const(1, SG_WIDTH), const(1, D_MODEL), const(1, D_MODEL),
        const(N_EXPERTS, D_MODEL), const(N_EXPERTS, TOK_BLOCK),
        seqtab, seqtab, seqtab, seqtab,
        const(RET_HEADS, TOK_BLOCK, TOK_BLOCK), const(RET_HEADS, TOK_BLOCK, RET_QK_DIM),
        const(RET_HEADS, TOK_BLOCK, RET_QK_DIM),
    ]
    out_shape = [
        jax.ShapeDtypeStruct((t + TOK_BLOCK, D_MODEL), F32),
        jax.ShapeDtypeStruct((TOP_K, t), jnp.int32),
        jax.ShapeDtypeStruct((TOP_K, t), F32),
        jax.ShapeDtypeStruct((groups, N_EXPERTS, 128), jnp.int32),
        jax.ShapeDtypeStruct((groups * GROUP_ROWS, D_MODEL), F32),
    ]
    out_specs = [
        pl.BlockSpec((TOK_BLOCK, D_MODEL), lambda i: (i, 0)), lanes(TOP_K), lanes(TOP_K),
        pl.BlockSpec((1, N_EXPERTS, 128), lambda i: (prev(i), 0, 0)),
        pl.BlockSpec((GROUP_ROWS, D_MODEL), lambda i: (prev(i), 0)),
    ]
    kern = functools.partial(_mixer_kernel, chunk_decay=tables["chunk_decay"], steps_per_seq=ns)
    return pl.pallas_call(
        kern,
        grid=(groups + 1,),
        in_specs=in_specs,
        out_specs=out_specs,
        out_shape=out_shape,
        scratch_shapes=[pltpu.VMEM((RET_HEADS, RET_QK_DIM, RET_V_DIM), F32),
                        pltpu.VMEM((TOK_BLOCK, SG_WIDTH), BF16),
                        pltpu.VMEM((TOK_BLOCK, D_MODEL), F32)],
        compiler_params=pltpu.CompilerParams(dimension_semantics=("arbitrary",),
                                             vmem_limit_bytes=VMEM_LIMIT_V7X),
        name="mixer",
    )(x2d, lw["w_in"], lw["p_ret"], lw["p_sg"], lw["w_o"], lw["sg_w"], lw["sg_bt"],
      lw["sg_ln_g"], lw["sg_ln_b"], lw["ln1_g"], lw["ln1_b"], lw["router_wt"], lw["router_bb"],
      tables["cq"], tables["sq"], tables["ck"], tables["sk"],
      tables["dmat"], tables["qdec"], tables["kdec"])


def _expert_kernel(bexp_ref, src_ref, nvb_ref, xs_hbm, w1_ref, b1_ref, w2_ref, b2_ref,
                   yb_ref, xbuf0, xbuf1, w1b, w2b, sem):
    b = pl.program_id(0)
    nvb = nvb_ref[0]
    xbufs = (xbuf0, xbuf1)

    def gather(blk, slot, pieces=range(EXP_PIECES)):
        for i in pieces:
            row = pl.multiple_of(src_ref[blk * EXP_PIECES + i] * PIECE, PIECE)
            pltpu.make_async_copy(xs_hbm.at[pl.ds(row, PIECE), :],
                                  xbufs[slot].at[pl.ds(i * PIECE, PIECE), :],
                                  sem.at[slot]).start()

    def wait(slot):
        pltpu.make_async_copy(xs_hbm.at[pl.ds(0, EXP_BLOCK), :], xbufs[slot], sem.at[slot]).wait()

    @pl.when(b == 0)
    def _():
        gather(0, 0)

    new_expert = (b == 0) | (bexp_ref[b] != bexp_ref[jnp.maximum(b - 1, 0)])

    @pl.when(new_expert & (b < nvb))
    def _():
        def cast_rows(i, carry):
            r = pl.multiple_of(i * W_CAST_ROWS, W_CAST_ROWS)
            w1b[pl.ds(r, W_CAST_ROWS), :] = w1_ref[0, 0, pl.ds(r, W_CAST_ROWS), :].astype(BF16)
            w2b[pl.ds(r, W_CAST_ROWS), :] = w2_ref[0, 0, pl.ds(r, W_CAST_ROWS), :].astype(BF16)
            return carry
        lax.fori_loop(0, D_MODEL // W_CAST_ROWS, cast_rows, 0)

    def block(cur):
        nxt = 1 - cur
        gather(jnp.minimum(b + 1, nvb - 1), nxt)
        wait(cur)
        x = xbufs[cur][...].astype(BF16)
        y = jnp.zeros((EXP_BLOCK, D_MODEL), F32)
        for j in range(D_EXPERT // F_CHUNK):
            c0 = j * F_CHUNK
            hg = _dot(x, w1b[:, c0:c0 + F_CHUNK]) + b1_ref[0, 0, :, c0:c0 + F_CHUNK]
            hu = (_dot(x, w1b[:, D_EXPERT + c0:D_EXPERT + c0 + F_CHUNK])
                  + b1_ref[0, 0, :, D_EXPERT + c0:D_EXPERT + c0 + F_CHUNK])
            gate = jnp.minimum(hg, SWIGLU_LIMIT)
            up = jnp.clip(hu, -SWIGLU_LIMIT, SWIGLU_LIMIT)
            act = gate * jax.nn.sigmoid(SWIGLU_ALPHA * gate) * (up + 1.0)
            y = y + _dot(act.astype(BF16), w2b[c0:c0 + F_CHUNK, :])
        yb_ref[...] = y + b2_ref[0, 0]

        @pl.when(b == nvb - 1)
        def _():
            wait(nxt)

    for cur in range(2):
        pl.when((b < nvb) & (b % 2 == cur))(functools.partial(block, cur))

    @pl.when(b >= nvb)
    def _():
        yb_ref[...] = jnp.zeros_like(yb_ref)


def _expert_call(xs, block_exp, src_piece, nvb, layer, e_w1, e_b1, e_w2, e_b2, n_blocks):
    wspec = lambda *shape: pl.BlockSpec((1, 1) + shape, lambda b, be, sp, nv: (layer, be[b], 0, 0))
    grid_spec = pltpu.PrefetchScalarGridSpec(
        num_scalar_prefetch=3,
        grid=(n_blocks,),
        in_specs=[
            pl.BlockSpec(memory_space=pl.ANY),
            wspec(D_MODEL, 2 * D_EXPERT), wspec(1, 2 * D_EXPERT),
            wspec(D_EXPERT, D_MODEL), wspec(1, D_MODEL),
        ],
        out_specs=pl.BlockSpec((EXP_BLOCK, D_MODEL), lambda b, be, sp, nv: (b, 0)),
        scratch_shapes=[pltpu.VMEM((EXP_BLOCK, D_MODEL), F32),
                        pltpu.VMEM((EXP_BLOCK, D_MODEL), F32),
                        pltpu.VMEM((D_MODEL, 2 * D_EXPERT), BF16),
                        pltpu.VMEM((D_EXPERT, D_MODEL), BF16),
                        pltpu.SemaphoreType.DMA((2,))],
    )
    return pl.pallas_call(
        _expert_kernel,
        grid_spec=grid_spec,
        out_shape=jax.ShapeDtypeStruct((n_blocks * EXP_BLOCK, D_MODEL), F32),
        compiler_params=pltpu.CompilerParams(dimension_semantics=("arbitrary",),
                                             vmem_limit_bytes=VMEM_LIMIT_V7X),
        name="experts",
    )(block_exp, src_piece, nvb, xs, e_w1, e_b1, e_w2, e_b2)


def _combine_kernel(yp_ref, x1_ref, yb_hbm, post_ref, gatet_ref, g_ref, b_ref, out_ref,
                    ybuf0, ybuf1, sem):
    g = pl.program_id(0)
    ng = pl.num_programs(0)
    ybufs = (ybuf0, ybuf1)

    def gather(grp, slot, pieces=range(GROUP_PIECES)):
        for i in pieces:
            row = pl.multiple_of(yp_ref[grp * GROUP_PIECES + i] * PIECE, PIECE)
            pltpu.make_async_copy(yb_hbm.at[pl.ds(row, PIECE), :],
                                  ybufs[slot].at[pl.ds(i * PIECE, PIECE), :],
                                  sem.at[slot]).start()

    def wait(slot):
        pltpu.make_async_copy(yb_hbm.at[pl.ds(0, GROUP_ROWS), :], ybufs[slot], sem.at[slot]).wait()

    @pl.when(g == 0)
    def _():
        gather(0, 0)

    def group(cur):
        nxt = 1 - cur
        gather(jnp.minimum(g + 1, ng - 1), nxt)
        post = post_ref[...]
        gatet = gatet_ref[...]
        slabs = []
        for r0 in range(0, GROUP_ROWS, SORT_ROWS):
            c_iota = lax.broadcasted_iota(jnp.int32, (TOK_BLOCK, SORT_ROWS), 1) + r0
            unsort = jnp.zeros((TOK_BLOCK, SORT_ROWS), F32)
            for k in range(TOP_K):
                unsort = unsort + jnp.where(c_iota == post[:, k:k + 1], gatet[:, k:k + 1], 0.0)
            slabs.append(unsort.astype(BF16))
        wait(cur)
        ffn = jnp.zeros((TOK_BLOCK, D_MODEL), F32)
        for i, r0 in enumerate(range(0, GROUP_ROWS, SORT_ROWS)):
            ffn = ffn + _dot(slabs[i], ybufs[cur][r0:r0 + SORT_ROWS, :].astype(BF16))
        out_ref[...] = _layer_norm(DN_ALPHA * x1_ref[...] + ffn, g_ref[...], b_ref[...])

        @pl.when(g == ng - 1)
        def _():
            wait(nxt)

    for cur in range(2):
        pl.when(g % 2 == cur)(functools.partial(group, cur))


def _combine_call(x1, yb, ypiece, post, gatet, ln_g, ln_b):
    t = post.shape[0]
    groups = t // TOK_BLOCK
    grid_spec = pltpu.PrefetchScalarGridSpec(
        num_scalar_prefetch=1,
        grid=(groups,),
        in_specs=[
            pl.BlockSpec((TOK_BLOCK, D_MODEL), lambda g, yp: (g, 0)),
            pl.BlockSpec(memory_space=pl.ANY),
            pl.BlockSpec((TOK_BLOCK, TOP_K), lambda g, yp: (g, 0)),
            pl.BlockSpec((TOK_BLOCK, TOP_K), lambda g, yp: (g, 0)),
            pl.BlockSpec((1, D_MODEL), lambda g, yp: (0, 0)),
            pl.BlockSpec((1, D_MODEL), lambda g, yp: (0, 0)),
        ],
        out_specs=pl.BlockSpec((TOK_BLOCK, D_MODEL), lambda g, yp: (g, 0)),
        scratch_shapes=[pltpu.VMEM((GROUP_ROWS, D_MODEL), F32), pltpu.VMEM((GROUP_ROWS, D_MODEL), F32),
                        pltpu.SemaphoreType.DMA((2,))],
    )
    return pl.pallas_call(
        _combine_kernel,
        grid_spec=grid_spec,
        out_shape=jax.ShapeDtypeStruct((t, D_MODEL), F32),
        compiler_params=pltpu.CompilerParams(dimension_semantics=("arbitrary",),
                                             vmem_limit_bytes=VMEM_LIMIT_V7X),
        name="combine",
    )(ypiece, x1, yb, post, gatet, ln_g, ln_b)


def _routing_tables(cnt_pad, n_blocks):
    groups = cnt_pad.shape[0]
    npc = cnt_pad // PIECE
    strip_start = jnp.cumsum(npc, axis=1) - npc
    tot = jnp.sum(npc, axis=0)
    tot_pad = (tot + EXP_PIECES - 1) // EXP_PIECES * EXP_PIECES
    exp_end = jnp.cumsum(tot_pad)
    exp_start = exp_end - tot_pad
    dstart = exp_start[None, :] + jnp.cumsum(npc, axis=0) - npc
    nvb = (exp_end[-1] // EXP_PIECES).astype(jnp.int32)

    blocks = jnp.arange(n_blocks, dtype=jnp.int32)
    bexp = jnp.sum(blocks[:, None] * EXP_PIECES >= exp_end[None, :], axis=1).astype(jnp.int32)
    bexp = jnp.minimum(bexp, N_EXPERTS - 1)
    last = jnp.sum((nvb - 1) * EXP_PIECES >= exp_end).astype(jnp.int32)
    bexp = jnp.where(blocks < nvb, bexp, jnp.minimum(last, N_EXPERTS - 1))

    d = jnp.arange(n_blocks * EXP_PIECES, dtype=jnp.int32)
    e_of = jnp.minimum(jnp.sum(d[:, None] >= exp_end[None, :], axis=1), N_EXPERTS - 1)
    oh_e = (e_of[:, None] == jnp.arange(N_EXPERTS, dtype=jnp.int32)[None, :]).astype(F32)
    row_of = lambda tab: jnp.dot(oh_e, tab.T.astype(F32), precision=lax.Precision.HIGHEST).astype(jnp.int32)
    ds_e = row_of(dstart)
    g_of = jnp.maximum(jnp.sum(ds_e <= d[:, None], axis=1) - 1, 0)
    at_g = g_of[:, None] == jnp.arange(groups, dtype=jnp.int32)[None, :]
    pick = lambda rows: jnp.sum(jnp.where(at_g, rows, 0), axis=1)
    i_of = d - pick(ds_e)
    valid = (i_of >= 0) & (i_of < pick(row_of(npc)))
    src = g_of * GROUP_PIECES + pick(row_of(strip_start)) + i_of
    src_piece = jnp.where(valid, src, 0).astype(jnp.int32)

    j = jnp.arange(GROUP_PIECES, dtype=jnp.int32)
    e_loc = jnp.maximum(jnp.sum(strip_start[:, None, :] <= j[None, :, None], axis=2) - 1, 0)
    at_e = e_loc[:, :, None] == jnp.arange(N_EXPERTS, dtype=jnp.int32)[None, None, :]
    pick_e = lambda tab: jnp.sum(jnp.where(at_e, tab[:, None, :], 0), axis=2)
    st_loc = pick_e(strip_start)
    np_loc = pick_e(npc)
    ds_loc = pick_e(dstart)
    i_loc = j[None, :] - st_loc
    ypiece = jnp.where(i_loc < np_loc, ds_loc + i_loc, 0).astype(jnp.int32).reshape(groups * GROUP_PIECES)
    return bexp, src_piece, nvb.reshape(1), ypiece


def _tables(seq):
    half = RET_QK_DIM // 2
    inv = ROPE_BASE ** (-jnp.arange(half, dtype=F32) / half)
    ang = jnp.arange(seq, dtype=jnp.int32).astype(F32)[:, None] * inv[None, :]
    cos, sin = jnp.cos(ang), jnp.sin(ang)
    cosf = jnp.concatenate([cos, cos], axis=1)
    sinf = jnp.concatenate([-sin, sin], axis=1)
    kscale = RET_QK_DIM ** -0.5
    log_g = jnp.log(1.0 - jnp.exp(jnp.linspace(math.log(1.0 / 32), math.log(1.0 / 512), RET_HEADS)))
    idx = jnp.arange(TOK_BLOCK, dtype=F32)
    diff = idx[:, None] - idx[None, :]
    dmat = jnp.where(diff >= 0, jnp.exp(log_g[:, None, None] * jnp.maximum(diff, 0.0)), 0.0)
    qdec = jnp.exp(log_g[:, None] * (idx[None, :] + 1.0))
    kdec = jnp.exp(log_g[:, None] * (TOK_BLOCK - 1.0 - idx[None, :]))
    bc = lambda a: jnp.broadcast_to(a[:, :, None], (RET_HEADS, TOK_BLOCK, RET_QK_DIM))
    log_g_host = np.log(1.0 - np.exp(np.linspace(math.log(1.0 / 32), math.log(1.0 / 512), RET_HEADS)))
    chunk_decay = tuple(float(np.float32(np.exp(np.float32(lg) * np.float32(TOK_BLOCK)))) for lg in log_g_host)
    return {"cq": cosf, "sq": sinf, "ck": cosf * kscale, "sk": sinf * kscale,
            "dmat": dmat.astype(F32), "qdec": bc(qdec), "kdec": bc(kdec), "chunk_decay": chunk_decay}


def kernel(x, w_in, p_ret, sg_ln_g, sg_ln_b, sg_w, sg_b, p_sg, w_o, ln1_g, ln1_b,
           router_w, router_b, e_w1, e_b1, e_w2, e_b2, ln2_g, ln2_b):
    batch, seq, d = x.shape
    assert d == D_MODEL and seq % TOK_BLOCK == 0
    depth = w_in.shape[0]
    t = batch * seq
    groups = t // TOK_BLOCK
    n_blocks = (groups * GROUP_PIECES + N_EXPERTS * (EXP_PIECES - 1)) // EXP_PIECES + 1
    tables = _tables(seq)
    xc = x.reshape(t, D_MODEL)
    w_in_b, p_ret_b, p_sg_b, w_o_b = (w.astype(BF16) for w in (w_in, p_ret, p_sg, w_o))
    e_b1r = e_b1.reshape(depth, N_EXPERTS, 1, 2 * D_EXPERT)
    e_b2r = e_b2.reshape(depth, N_EXPERTS, 1, D_MODEL)
    for l in range(depth):
        lw = {
            "w_in": w_in_b, "p_ret": p_ret_b, "p_sg": p_sg_b,
            "w_o": w_o_b, "sg_w": sg_w[l], "sg_bt": sg_b[l].T,
            "sg_ln_g": sg_ln_g[l][None], "sg_ln_b": sg_ln_b[l][None],
            "ln1_g": ln1_g[l][None], "ln1_b": ln1_b[l][None],
            "router_wt": router_w[l].T,
            "router_bb": jnp.broadcast_to(router_b[l][:, None], (N_EXPERTS, TOK_BLOCK)),
        }
        x1, pos, gates, cnt, xs = _mixer_call(xc, batch, seq, l, lw, tables)
        bexp, src_piece, nvb, ypiece = _routing_tables(cnt[:, :, 0], n_blocks)
        yb = _expert_call(xs, bexp, src_piece, nvb, l, e_w1, e_b1r, e_w2, e_b2r, n_blocks)
        xc = _combine_call(x1, yb, ypiece, pos.T, gates.T, ln2_g[l][None], ln2_b[l][None])
    return xc.reshape(batch, seq, D_MODEL)
```

```python
import functools
import math

import jax
import jax.numpy as jnp
import numpy as np
from jax import lax
from jax.experimental import pallas as pl
from jax.experimental.pallas import tpu as pltpu

F32 = jnp.float32
BF16 = jnp.bfloat16

D_MODEL = 1024
DEPTH = 4
RET_HEADS = 4
RET_QK_DIM = 128
RET_V_DIM = 256
RET_QK = RET_HEADS * RET_QK_DIM
RET_V = RET_HEADS * RET_V_DIM
ROPE_BASE = 10000.0
SG_BLOCK = 128
SG_CHUNK = 64
SG_GROUPS = 8
SG_WIDTH = D_MODEL
N_EXPERTS = 32
TOP_K = 4
D_EXPERT = D_MODEL
SWIGLU_LIMIT = 7.0
SWIGLU_ALPHA = 1.702
LN_EPS = 1e-5
DN_ALPHA = (2 * DEPTH) ** 0.25

OFF_Q = 0
OFF_K = OFF_Q + RET_QK
OFF_V = OFF_K + RET_QK
OFF_G = OFF_V + RET_V
OFF_U = OFF_G + RET_V
OFF_VS = OFF_U + SG_WIDTH
OFF_GA = OFF_VS + SG_WIDTH
OFF_GB = OFF_GA + D_MODEL
IN_WIDTH = OFF_GB + D_MODEL

SUBLANES_V7X = 8
TOK_BLOCK = 256
PIECE = SUBLANES_V7X
GROUP_ROWS = TOK_BLOCK * TOP_K + N_EXPERTS * PIECE
GROUP_PIECES = GROUP_ROWS // PIECE
SORT_ROWS = 256
EXP_BLOCK = 512
EXP_PIECES = EXP_BLOCK // PIECE
F_CHUNK = 512
W_CAST_ROWS = 128
VMEM_LIMIT_V7X = 56 * 1024 * 1024


def _layer_norm(x, g, b):
    mu = jnp.mean(x, axis=-1, keepdims=True)
    xc = x - mu
    var = jnp.mean(xc * xc, axis=-1, keepdims=True)
    return xc * lax.rsqrt(var + LN_EPS) * g + b


def _gelu_tanh(x):
    c = math.sqrt(2.0 / math.pi)
    return 0.5 * x * (1.0 + jnp.tanh(c * (x + 0.044715 * (x * x * x))))


def _dot(a, b):
    return jnp.dot(a, b, preferred_element_type=F32)


def _dot_nt(a, b):
    return lax.dot_general(a, b, (((1,), (1,)), ((), ())), preferred_element_type=F32)


def _dot_tn(a, b):
    return lax.dot_general(a, b, (((0,), (0,)), ((), ())), preferred_element_type=F32)


def _split_bf16(x):
    hi = x.astype(BF16)
    lo = (x - hi.astype(F32)).astype(BF16)
    return hi, lo


def _mixer_kernel(x_ref, w_in_ref, p_ret_ref, p_sg_ref, w_o_ref, sgw_ref, sgbt_ref,
                  sglng_ref, sglnb_ref, ln1g_ref, ln1b_ref, rwt_ref, rbb_ref,
                  cq_ref, sq_ref, ck_ref, sk_ref, dmat_ref, qdec_ref, kdec_ref,
                  x1_ref, pos_ref, gate_ref, cnt_ref, xs_ref,
                  state_ref, ysg_ref, x1_prev_ref, *, chunk_decay, steps_per_seq):
    ts = TOK_BLOCK
    step = pl.program_id(0)

    @pl.when(step == 0)
    def _():
        x1_prev_ref[...] = jnp.zeros_like(x1_prev_ref)

    @pl.when(step % steps_per_seq == 0)
    def _():
        state_ref[...] = jnp.zeros_like(state_ref)

    routing = _route_and_sort(x1_prev_ref[...], rwt_ref, rbb_ref, pos_ref, gate_ref, cnt_ref, xs_ref)
    routing_stage = lambda: next(routing, None)

    xb = x_ref[...]
    xbf = xb.astype(BF16)

    def proj(off, width):
        return _dot(xbf, w_in_ref[0, :, off:off + width])

    cq, sq, ck, sk = cq_ref[...], sq_ref[...], ck_ref[...], sk_ref[...]
    acc_ret = jnp.zeros((ts, D_MODEL), F32)
    pair, head_in = {}, {}

    def project_head(h):
        if h % 2 == 0:
            pair["q"] = proj(OFF_Q + h * RET_QK_DIM, 2 * RET_QK_DIM)
            yield
            pair["k"] = proj(OFF_K + h * RET_QK_DIM, 2 * RET_QK_DIM)
            yield
        lanes = slice((h % 2) * RET_QK_DIM, (h % 2 + 1) * RET_QK_DIM)
        v = proj(OFF_V + h * RET_V_DIM, RET_V_DIM).astype(BF16)
        yield
        g = proj(OFF_G + h * RET_V_DIM, RET_V_DIM)
        head_in[h] = (pair["q"][:, lanes], pair["k"][:, lanes], v, g)
        yield

    for _ in project_head(0):
        pass
    for h in range(RET_HEADS):
        upcoming = project_head(h + 1) if h + 1 < RET_HEADS else iter(())
        q, k, v, g = head_in.pop(h)
        routing_stage()
        next(upcoming, None)
        qr = q * cq + pltpu.roll(q, RET_QK_DIM // 2, 1) * sq
        kr = k * ck + pltpu.roll(k, RET_QK_DIM // 2, 1) * sk
        scores = _dot_nt(qr.astype(BF16), kr.astype(BF16)) * dmat_ref[h]
        next(upcoming, None)
        inner = _dot(scores.astype(BF16), v)
        st = state_ref[h]
        cross = _dot((qr * qdec_ref[h]).astype(BF16), st.astype(BF16))
        kd = (kr * kdec_ref[h]).astype(BF16)
        state_ref[h] = st * chunk_decay[h] + _dot_tn(kd, v)
        routing_stage()
        next(upcoming, None)
        y = inner + cross
        mu = jnp.mean(y, axis=-1, keepdims=True)
        yc = y - mu
        var = jnp.mean(yc * yc, axis=-1, keepdims=True)
        yn = yc * lax.rsqrt(var + LN_EPS)
        next(upcoming, None)
        yr = yn * (g * jax.nn.sigmoid(g))
        acc_ret = acc_ret + _dot(yr.astype(BF16), p_ret_ref[0, h * RET_V_DIM:(h + 1) * RET_V_DIM, :])
        for _ in upcoming:
            pass

    pre_u = proj(OFF_U, SG_WIDTH)
    pre_vs = proj(OFF_VS, SG_WIDTH)
    ga = proj(OFF_GA, D_MODEL)
    u = _gelu_tanh(pre_u)
    routing_stage()
    gb = proj(OFF_GB, D_MODEL)
    vn = _layer_norm(_gelu_tanh(pre_vs), sglng_ref[...], sglnb_ref[...])
    qi = lax.broadcasted_iota(jnp.int32, (SG_BLOCK, SG_BLOCK), 0) // SG_CHUNK
    pi = lax.broadcasted_iota(jnp.int32, (SG_BLOCK, SG_BLOCK), 1) // SG_CHUNK
    sg_mask = qi >= pi
    for gi in range(SG_GROUPS):
        wm = jnp.where(sg_mask, sgw_ref[gi], 0.0).astype(BF16)
        bias = sgbt_ref[:, gi:gi + 1]
        cols = slice(gi * SG_BLOCK, (gi + 1) * SG_BLOCK)
        blocks = [slice(blk * SG_BLOCK, (blk + 1) * SG_BLOCK) for blk in range(ts // SG_BLOCK)]
        vcat = jnp.concatenate([vn[rows, cols] for rows in blocks], axis=1).astype(BF16)
        mixed = _dot(wm, vcat) + bias
        for blk, rows in enumerate(blocks):
            ysg_ref[rows, cols] = (u[rows, cols] * mixed[:, blk * SG_BLOCK:(blk + 1) * SG_BLOCK]).astype(BF16)
    routing_stage()
    acc_sg = _dot(ysg_ref[...], p_sg_ref[0])

    hmix = jax.nn.sigmoid(ga) * acc_ret + jax.nn.sigmoid(gb) * acc_sg
    mix = _dot(hmix.astype(BF16), w_o_ref[0])
    for _ in routing:
        pass
    x1 = _layer_norm(DN_ALPHA * xb + mix, ln1g_ref[...], ln1b_ref[...])
    x1_ref[...] = x1
    x1_prev_ref[...] = x1


def _route_and_sort(x1, rwt_ref, rbb_ref, pos_ref, gate_ref, cnt_ref, xs_ref):
    ts = TOK_BLOCK
    x_hi, x_lo = _split_bf16(x1)
    w_hi, w_lo = _split_bf16(rwt_ref[...])
    logits = (_dot_nt(w_hi, x_hi) + _dot_nt(w_hi, x_lo) + _dot_nt(w_lo, x_hi)) + rbb_ref[...]
    yield
    e_iota = lax.broadcasted_iota(jnp.int32, (N_EXPERTS, ts), 0).astype(F32)
    sels, vals = [], []
    work = logits
    for _ in range(TOP_K):
        m = jnp.max(work, axis=0, keepdims=True)
        idx = jnp.min(jnp.where(work == m, e_iota, float(N_EXPERTS)), axis=0, keepdims=True)
        sel = e_iota == idx
        work = jnp.where(sel, -jnp.inf, work)
        sels.append(sel)
        vals.append(m)
        yield
    exps = [jnp.exp(v - vals[0]) for v in vals]
    denom = exps[0] + exps[1] + exps[2] + exps[3]
    gate_ref[...] = jnp.concatenate([e / denom for e in exps], axis=0)

    onehots = [jnp.where(s, 1.0, 0.0) for s in sels]
    oh_sum = onehots[0] + onehots[1] + onehots[2] + onehots[3]
    t_row = lax.broadcasted_iota(jnp.int32, (ts, ts), 0)
    t_col = lax.broadcasted_iota(jnp.int32, (ts, ts), 1)
    upper = jnp.where(t_row < t_col, 1.0, 0.0).astype(BF16)
    before = _dot(oh_sum.astype(BF16), upper)
    cnt = jnp.sum(oh_sum, axis=1, keepdims=True)
    cnt_pad = jnp.floor((cnt + (PIECE - 1)) * (1.0 / PIECE)) * PIECE
    cnt_b = jnp.broadcast_to(cnt_pad, (N_EXPERTS, ts))
    e_row = lax.broadcasted_iota(jnp.int32, (N_EXPERTS, N_EXPERTS), 0)
    e_col = lax.broadcasted_iota(jnp.int32, (N_EXPERTS, N_EXPERTS), 1)
    lower = jnp.where(e_col < e_row, 1.0, 0.0).astype(BF16)
    strip_off = _dot(lower, cnt_b.astype(BF16))
    slot = strip_off + before
    pos = [jnp.sum(oh * slot, axis=0, keepdims=True).astype(jnp.int32) for oh in onehots]
    pos_ref[...] = jnp.concatenate(pos, axis=0)
    cnt_ref[0] = cnt_b[:, :128].astype(jnp.int32)
    yield
    for r0 in range(0, GROUP_ROWS, SORT_ROWS):
        r_iota = lax.broadcasted_iota(jnp.int32, (SORT_ROWS, ts), 0) + r0
        hit = (r_iota == pos[0]) | (r_iota == pos[1]) | (r_iota == pos[2]) | (r_iota == pos[3])
        perm = jnp.where(hit, 1.0, 0.0).astype(BF16)
        xs_ref[r0:r0 + SORT_ROWS, :] = _dot(perm, x_hi)
        yield


def _mixer_call(x2d, batch, seq, layer, lw, tables):
    t = batch * seq
    ns = seq // TOK_BLOCK
    groups = t // TOK_BLOCK
    const = lambda *shape: pl.BlockSpec(shape, lambda i: (0,) * len(shape),
                                        pipeline_mode=pl.Buffered(1))
    stacked = lambda *shape: pl.BlockSpec((1,) + shape, lambda i: (layer, 0, 0),
                                          pipeline_mode=pl.Buffered(1))
    cur = lambda i: jnp.minimum(i, groups - 1)
    prev = lambda i: jnp.maximum(i - 1, 0)
    seqtab = pl.BlockSpec((TOK_BLOCK, RET_QK_DIM), lambda i: (cur(i) % ns, 0))
    lanes = lambda rows: pl.BlockSpec((rows, TOK_BLOCK), lambda i: (0, prev(i)))
    in_specs = [
        pl.BlockSpec((TOK_BLOCK, D_MODEL), lambda i: (cur(i), 0)),
        stacked(D_MODEL, IN_WIDTH), stacked(RET_V, D_MODEL), stacked(SG_WIDTH, D_MODEL),
        stacked(D_MODEL, D_MODEL),
        const(SG_GROUPS, SG_BLOCK, SG_BLOCK), const(SG_BLOCK, SG_GROUPS),
        const(1, SG_WIDTH), const(1, SG_WIDTH), const(1, D_MODEL), const(1, D_MODEL),
        const(N_EXPERTS, D_MODEL), const(N_EXPERTS, TOK_BLOCK),
        seqtab, seqtab, seqtab, seqtab,
        const(RET_HEADS, TOK_BLOCK, TOK_BLOCK), const(RET_HEADS, TOK_BLOCK, RET_QK_DIM),
        const(RET_HEADS, TOK_BLOCK, RET_QK_DIM),
    ]
    out_shape = [
        jax.ShapeDtypeStruct((t + TOK_BLOCK, D_MODEL), F32),
        jax.ShapeDtypeStruct((TOP_K, t), jnp.int32),
        jax.ShapeDtypeStruct((TOP_K, t), F32),
        jax.ShapeDtypeStruct((groups, N_EXPERTS, 128), jnp.int32),
        jax.ShapeDtypeStruct((groups * GROUP_ROWS, D_MODEL), F32),
    ]
    out_specs = [
        pl.BlockSpec((TOK_BLOCK, D_MODEL), lambda i: (i, 0)), lanes(TOP_K), lanes(TOP_K),
        pl.BlockSpec((1, N_EXPERTS, 128), lambda i: (prev(i), 0, 0)),
        pl.BlockSpec((GROUP_ROWS, D_MODEL), lambda i: (prev(i), 0)),
    ]
    kern = functools.partial(_mixer_kernel, chunk_decay=tables["chunk_decay"], steps_per_seq=ns)
    return pl.pallas_call(
        kern,
        grid=(groups + 1,),
        in_specs=in_specs,
        out_specs=out_specs,
        out_shape=out_shape,
        scratch_shapes=[pltpu.VMEM((RET_HEADS, RET_QK_DIM, RET_V_DIM), F32),
                        pltpu.VMEM((TOK_BLOCK, SG_WIDTH), BF16),
                        pltpu.VMEM((TOK_BLOCK, D_MODEL), F32)],
        compiler_params=pltpu.CompilerParams(dimension_semantics=("arbitrary",),
                                             vmem_limit_bytes=VMEM_LIMIT_V7X),
        name="mixer",
    )(x2d, lw["w_in"], lw["p_ret"], lw["p_sg"], lw["w_o"], lw["sg_w"], lw["sg_bt"],
      lw["sg_ln_g"], lw["sg_ln_b"], lw["ln1_g"], lw["ln1_b"], lw["router_wt"], lw["router_bb"],
      tables["cq"], tables["sq"], tables["ck"], tables["sk"],
      tables["dmat"], tables["qdec"], tables["kdec"])


def _expert_kernel(bexp_ref, src_ref, nvb_ref, xs_hbm, w1_ref, b1_ref, w2_ref, b2_ref,
                   yb_ref, xbuf0, xbuf1, w1b, w2b, sem):
    b = pl.program_id(0)
    nvb = nvb_ref[0]
    xbufs = (xbuf0, xbuf1)

    def gather(blk, slot, pieces=range(EXP_PIECES)):
        for i in pieces:
            row = pl.multiple_of(src_ref[blk * EXP_PIECES + i] * PIECE, PIECE)
            pltpu.make_async_copy(xs_hbm.at[pl.ds(row, PIECE), :],
                                  xbufs[slot].at[pl.ds(i * PIECE, PIECE), :],
                                  sem.at[slot]).start()

    def wait(slot):
        pltpu.make_async_copy(xs_hbm.at[pl.ds(0, EXP_BLOCK), :], xbufs[slot], sem.at[slot]).wait()

    @pl.when(b == 0)
    def _():
        gather(0, 0)

    new_expert = (b == 0) | (bexp_ref[b] != bexp_ref[jnp.maximum(b - 1, 0)])

    @pl.when(new_expert & (b < nvb))
    def _():
        def cast_rows(i, carry):
            r = pl.multiple_of(i * W_CAST_ROWS, W_CAST_ROWS)
            w1b[pl.ds(r, W_CAST_ROWS), :] = w1_ref[0, 0, pl.ds(r, W_CAST_ROWS), :].astype(BF16)
            w2b[pl.ds(r, W_CAST_ROWS), :] = w2_ref[0, 0, pl.ds(r, W_CAST_ROWS), :].astype(BF16)
            return carry
        lax.fori_loop(0, D_MODEL // W_CAST_ROWS, cast_rows, 0)

    def block(cur):
        nxt = 1 - cur
        gather(jnp.minimum(b + 1, nvb - 1), nxt)
        wait(cur)
        x = xbufs[cur][...].astype(BF16)
        y = jnp.zeros((EXP_BLOCK, D_MODEL), F32)
        for j in range(D_EXPERT // F_CHUNK):
            c0 = j * F_CHUNK
            hg = _dot(x, w1b[:, c0:c0 + F_CHUNK]) + b1_ref[0, 0, :, c0:c0 + F_CHUNK]
            hu = (_dot(x, w1b[:, D_EXPERT + c0:D_EXPERT + c0 + F_CHUNK])
                  + b1_ref[0, 0, :, D_EXPERT + c0:D_EXPERT + c0 + F_CHUNK])
            gate = jnp.minimum(hg, SWIGLU_LIMIT)
            up = jnp.clip(hu, -SWIGLU_LIMIT, SWIGLU_LIMIT)
            act = gate * jax.nn.sigmoid(SWIGLU_ALPHA * gate) * (up + 1.0)
            y = y + _dot(act.astype(BF16), w2b[c0:c0 + F_CHUNK, :])
        yb_ref[...] = y + b2_ref[0, 0]

        @pl.when(b == nvb - 1)
        def _():
            wait(nxt)

    for cur in range(2):
        pl.when((b < nvb) & (b % 2 == cur))(functools.partial(block, cur))

    @pl.when(b >= nvb)
    def _():
        yb_ref[...] = jnp.zeros_like(yb_ref)


def _expert_call(xs, block_exp, src_piece, nvb, layer, e_w1, e_b1, e_w2, e_b2, n_blocks):
    wspec = lambda *shape: pl.BlockSpec((1, 1) + shape, lambda b, be, sp, nv: (layer, be[b], 0, 0))
    grid_spec = pltpu.PrefetchScalarGridSpec(
        num_scalar_prefetch=3,
        grid=(n_blocks,),
        in_specs=[
            pl.BlockSpec(memory_space=pl.ANY),
            wspec(D_MODEL, 2 * D_EXPERT), wspec(1, 2 * D_EXPERT),
            wspec(D_EXPERT, D_MODEL), wspec(1, D_MODEL),
        ],
        out_specs=pl.BlockSpec((EXP_BLOCK, D_MODEL), lambda b, be, sp, nv: (b, 0)),
        scratch_shapes=[pltpu.VMEM((EXP_BLOCK, D_MODEL), F32),
                        pltpu.VMEM((EXP_BLOCK, D_MODEL), F32),
                        pltpu.VMEM((D_MODEL, 2 * D_EXPERT), BF16),
                        pltpu.VMEM((D_EXPERT, D_MODEL), BF16),
                        pltpu.SemaphoreType.DMA((2,))],
    )
    return pl.pallas_call(
        _expert_kernel,
        grid_spec=grid_spec,
        out_shape=jax.ShapeDtypeStruct((n_blocks * EXP_BLOCK, D_MODEL), F32),
        compiler_params=pltpu.CompilerParams(dimension_semantics=("arbitrary",),
                                             vmem_limit_bytes=VMEM_LIMIT_V7X),
        name="experts",
    )(block_exp, src_piece, nvb, xs, e_w1, e_b1, e_w2, e_b2)


def _combine_kernel(yp_ref, x1_ref, yb_hbm, post_ref, gatet_ref, g_ref, b_ref, out_ref,
                    ybuf0, ybuf1, sem):
    g = pl.program_id(0)
    ng = pl.num_programs(0)
    ybufs = (ybuf0, ybuf1)

    def gather(grp, slot, pieces=range(GROUP_PIECES)):
        for i in pieces:
            row = pl.multiple_of(yp_ref[grp * GROUP_PIECES + i] * PIECE, PIECE)
            pltpu.make_async_copy(yb_hbm.at[pl.ds(row, PIECE), :],
                                  ybufs[slot].at[pl.ds(i * PIECE, PIECE), :],
                                  sem.at[slot]).start()

    def wait(slot):
        pltpu.make_async_copy(yb_hbm.at[pl.ds(0, GROUP_ROWS), :], ybufs[slot], sem.at[slot]).wait()

    @pl.when(g == 0)
    def _():
        gather(0, 0)

    def group(cur):
        nxt = 1 - cur
        gather(jnp.minimum(g + 1, ng - 1), nxt)
        post = post_ref[...]
        gatet = gatet_ref[...]
        slabs = []
        for r0 in range(0, GROUP_ROWS, SORT_ROWS):
            c_iota = lax.broadcasted_iota(jnp.int32, (TOK_BLOCK, SORT_ROWS), 1) + r0
            unsort = jnp.zeros((TOK_BLOCK, SORT_ROWS), F32)
            for k in range(TOP_K):
                unsort = unsort + jnp.where(c_iota == post[:, k:k + 1], gatet[:, k:k + 1], 0.0)
            slabs.append(unsort.astype(BF16))
        wait(cur)
        ffn = jnp.zeros((TOK_BLOCK, D_MODEL), F32)
        for i, r0 in enumerate(range(0, GROUP_ROWS, SORT_ROWS)):
            ffn = ffn + _dot(slabs[i], ybufs[cur][r0:r0 + SORT_ROWS, :].astype(BF16))
        out_ref[...] = _layer_norm(DN_ALPHA * x1_ref[...] + ffn, g_ref[...], b_ref[...])

        @pl.when(g == ng - 1)
        def _():
            wait(nxt)

    for cur in range(2):
        pl.when(g % 2 == cur)(functools.partial(group, cur))


def _combine_call(x1, yb, ypiece, post, gatet, ln_g, ln_b):
    t = post.shape[0]
    groups = t // TOK_BLOCK
    grid_spec = pltpu.PrefetchScalarGridSpec(
        num_scalar_prefetch=1,
        grid=(groups,),
        in_specs=[
            pl.BlockSpec((TOK_BLOCK, D_MODEL), lambda g, yp: (g, 0)),
            pl.BlockSpec(memory_space=pl.ANY),
            pl.BlockSpec((TOK_BLOCK, TOP_K), lambda g, yp: (g, 0)),
            pl.BlockSpec((TOK_BLOCK, TOP_K), lambda g, yp: (g, 0)),
            pl.BlockSpec((1, D_MODEL), lambda g, yp: (0, 0)),
            pl.BlockSpec((1, D_MODEL), lambda g, yp: (0, 0)),
        ],
        out_specs=pl.BlockSpec((TOK_BLOCK, D_MODEL), lambda g, yp: (g, 0)),
        scratch_shapes=[pltpu.VMEM((GROUP_ROWS, D_MODEL), F32), pltpu.VMEM((GROUP_ROWS, D_MODEL), F32),
                        pltpu.SemaphoreType.DMA((2,))],
    )
    return pl.pallas_call(
        _combine_kernel,
        grid_spec=grid_spec,
        out_shape=jax.ShapeDtypeStruct((t, D_MODEL), F32),
        compiler_params=pltpu.CompilerParams(dimension_semantics=("arbitrary",),
                                             vmem_limit_bytes=VMEM_LIMIT_V7X),
        name="combine",
    )(ypiece, x1, yb, post, gatet, ln_g, ln_b)


def _routing_tables(cnt_pad, n_blocks):
    groups = cnt_pad.shape[0]
    npc = cnt_pad // PIECE
    strip_start = jnp.cumsum(npc, axis=1) - npc
    tot = jnp.sum(npc, axis=0)
    tot_pad = (tot + EXP_PIECES - 1) // EXP_PIECES * EXP_PIECES
    exp_end = jnp.cumsum(tot_pad)
    exp_start = exp_end - tot_pad
    dstart = exp_start[None, :] + jnp.cumsum(npc, axis=0) - npc
    nvb = (exp_end[-1] // EXP_PIECES).astype(jnp.int32)

    blocks = jnp.arange(n_blocks, dtype=jnp.int32)
    bexp = jnp.sum(blocks[:, None] * EXP_PIECES >= exp_end[None, :], axis=1).astype(jnp.int32)
    bexp = jnp.minimum(bexp, N_EXPERTS - 1)
    last = jnp.sum((nvb - 1) * EXP_PIECES >= exp_end).astype(jnp.int32)
    bexp = jnp.where(blocks < nvb, bexp, jnp.minimum(last, N_EXPERTS - 1))

    d = jnp.arange(n_blocks * EXP_PIECES, dtype=jnp.int32)
    e_of = jnp.minimum(jnp.sum(d[:, None] >= exp_end[None, :], axis=1), N_EXPERTS - 1)
    oh_e = (e_of[:, None] == jnp.arange(N_EXPERTS, dtype=jnp.int32)[None, :]).astype(F32)
    row_of = lambda tab: jnp.dot(oh_e, tab.T.astype(F32), precision=lax.Precision.HIGHEST).astype(jnp.int32)
    ds_e = row_of(dstart)
    g_of = jnp.maximum(jnp.sum(ds_e <= d[:, None], axis=1) - 1, 0)
    at_g = g_of[:, None] == jnp.arange(groups, dtype=jnp.int32)[None, :]
    pick = lambda rows: jnp.sum(jnp.where(at_g, rows, 0), axis=1)
    i_of = d - pick(ds_e)
    valid = (i_of >= 0) & (i_of < pick(row_of(npc)))
    src = g_of * GROUP_PIECES + pick(row_of(strip_start)) + i_of
    src_piece = jnp.where(valid, src, 0).astype(jnp.int32)

    j = jnp.arange(GROUP_PIECES, dtype=jnp.int32)
    e_loc = jnp.maximum(jnp.sum(strip_start[:, None, :] <= j[None, :, None], axis=2) - 1, 0)
    at_e = e_loc[:, :, None] == jnp.arange(N_EXPERTS, dtype=jnp.int32)[None, None, :]
    pick_e = lambda tab: jnp.sum(jnp.where(at_e, tab[:, None, :], 0), axis=2)
    st_loc = pick_e(strip_start)
    np_loc = pick_e(npc)
    ds_loc = pick_e(dstart)
    i_loc = j[None, :] - st_loc
    ypiece = jnp.where(i_loc < np_loc, ds_loc + i_loc, 0).astype(jnp.int32).reshape(groups * GROUP_PIECES)
    return bexp, src_piece, nvb.reshape(1), ypiece


def _tables(seq):
    half = RET_QK_DIM // 2
    inv = ROPE_BASE ** (-jnp.arange(half, dtype=F32) / half)
    ang = jnp.arange(seq, dtype=jnp.int32).astype(F32)[:, None] * inv[None, :]
    cos, sin = jnp.cos(ang), jnp.sin(ang)
    cosf = jnp.concatenate([cos, cos], axis=1)
    sinf = jnp.concatenate([-sin, sin], axis=1)
    kscale = RET_QK_DIM ** -0.5
    log_g = jnp.log(1.0 - jnp.exp(jnp.linspace(math.log(1.0 / 32), math.log(1.0 / 512), RET_HEADS)))
    idx = jnp.arange(TOK_BLOCK, dtype=F32)
    diff = idx[:, None] - idx[None, :]
    dmat = jnp.where(diff >= 0, jnp.exp(log_g[:, None, None] * jnp.maximum(diff, 0.0)), 0.0)
    qdec = jnp.exp(log_g[:, None] * (idx[None, :] + 1.0))
    kdec = jnp.exp(log_g[:, None] * (TOK_BLOCK - 1.0 - idx[None, :]))
    bc = lambda a: jnp.broadcast_to(a[:, :, None], (RET_HEADS, TOK_BLOCK, RET_QK_DIM))
    log_g_host = np.log(1.0 - np.exp(np.linspace(math.log(1.0 / 32), math.log(1.0 / 512), RET_HEADS)))
    chunk_decay = tuple(float(np.float32(np.exp(np.float32(lg) * np.float32(TOK_BLOCK)))) for lg in log_g_host)
    return {"cq": cosf, "sq": sinf, "ck": cosf * kscale, "sk": sinf * kscale,
            "dmat": dmat.astype(F32), "qdec": bc(qdec), "kdec": bc(kdec), "chunk_decay": chunk_decay}


def kernel(x, w_in, p_ret, sg_ln_g, sg_ln_b, sg_w, sg_b, p_sg, w_o, ln1_g, ln1_b,
           router_w, router_b, e_w1, e_b1, e_w2, e_b2, ln2_g, ln2_b):
    batch, seq, d = x.shape
    assert d == D_MODEL and seq % TOK_BLOCK == 0
    depth = w_in.shape[0]
    t = batch * seq
    groups = t // TOK_BLOCK
    n_blocks = (groups * GROUP_PIECES + N_EXPERTS * (EXP_PIECES - 1)) // EXP_PIECES + 1
    tables = _tables(seq)
    xc = x.reshape(t, D_MODEL)
    w_in_b, p_ret_b, p_sg_b, w_o_b = (w.astype(BF16) for w in (w_in, p_ret, p_sg, w_o))
    e_b1r = e_b1.reshape(depth, N_EXPERTS, 1, 2 * D_EXPERT)
    e_b2r = e_b2.reshape(depth, N_EXPERTS, 1, D_MODEL)
    for l in range(depth):
        lw = {
            "w_in": w_in_b, "p_ret": p_ret_b, "p_sg": p_sg_b,
            "w_o": w_o_b, "sg_w": sg_w[l], "sg_bt": sg_b[l].T,
            "sg_ln_g": sg_ln_g[l][None], "sg_ln_b": sg_ln_b[l][None],
            "ln1_g": ln1_g[l][None], "ln1_b": ln1_b[l][None],
            "router_wt": router_w[l].T,
            "router_bb": jnp.broadcast_to(router_b[l][:, None], (N_EXPERTS, TOK_BLOCK)),
        }
        x1, pos, gates, cnt, xs = _mixer_call(xc, batch, seq, l, lw, tables)
        bexp, src_piece, nvb, ypiece = _routing_tables(cnt[:, :, 0], n_blocks)
        yb = _expert_call(xs, bexp, src_piece, nvb, l, e_w1, e_b1r, e_w2, e_b2r, n_blocks)
        xc = _combine_call(x1, yb, ypiece, pos.T, gates.T, ln2_g[l][None], ln2_b[l][None])
    return xc.reshape(batch, seq, D_MODEL)
```

```python
import functools
import math

import jax
import jax.numpy as jnp
import numpy as np
from jax import lax
from jax.experimental import pallas as pl
from jax.experimental.pallas import tpu as pltpu

F32 = jnp.float32
BF16 = jnp.bfloat16

D_MODEL = 1024
DEPTH = 4
RET_HEADS = 4
RET_QK_DIM = 128
RET_V_DIM = 256
RET_QK = RET_HEADS * RET_QK_DIM
RET_V = RET_HEADS * RET_V_DIM
ROPE_BASE = 10000.0
SG_BLOCK = 128
SG_CHUNK = 64
SG_GROUPS = 8
SG_WIDTH = D_MODEL
N_EXPERTS = 32
TOP_K = 4
D_EXPERT = D_MODEL
SWIGLU_LIMIT = 7.0
SWIGLU_ALPHA = 1.702
LN_EPS = 1e-5
DN_ALPHA = (2 * DEPTH) ** 0.25

OFF_Q = 0
OFF_K = OFF_Q + RET_QK
OFF_V = OFF_K + RET_QK
OFF_G = OFF_V + RET_V
OFF_U = OFF_G + RET_V
OFF_VS = OFF_U + SG_WIDTH
OFF_GA = OFF_VS + SG_WIDTH
OFF_GB = OFF_GA + D_MODEL
IN_WIDTH = OFF_GB + D_MODEL

SUBLANES_V7X = 8
TOK_BLOCK = 256
PIECE = SUBLANES_V7X
GROUP_ROWS = TOK_BLOCK * TOP_K + N_EXPERTS * PIECE
GROUP_PIECES = GROUP_ROWS // PIECE
SORT_ROWS = 256
EXP_BLOCK = 512
EXP_PIECES = EXP_BLOCK // PIECE
F_CHUNK = 512
W_CAST_ROWS = 128
VMEM_LIMIT_V7X = 58 * 1024 * 1024


def _layer_norm(x, g, b):
    mu = jnp.mean(x, axis=-1, keepdims=True)
    xc = x - mu
    var = jnp.mean(xc * xc, axis=-1, keepdims=True)
    return xc * lax.rsqrt(var + LN_EPS) * g + b


def _gelu_tanh(x):
    c = math.sqrt(2.0 / math.pi)
    return 0.5 * x * (1.0 + jnp.tanh(c * (x + 0.044715 * (x * x * x))))


def _dot(a, b):
    return jnp.dot(a, b, preferred_element_type=F32)


def _dot_nt(a, b):
    return lax.dot_general(a, b, (((1,), (1,)), ((), ())), preferred_element_type=F32)


def _dot_tn(a, b):
    return lax.dot_general(a, b, (((0,), (0,)), ((), ())), preferred_element_type=F32)


def _split_bf16(x):
    hi = x.astype(BF16)
    lo = (x - hi.astype(F32)).astype(BF16)
    return hi, lo


def _unsort_slabs(post, gatet):
    slabs = []
    for r0 in range(0, GROUP_ROWS, SORT_ROWS):
        c_iota = lax.broadcasted_iota(jnp.int32, (TOK_BLOCK, SORT_ROWS), 1) + r0
        unsort = jnp.zeros((TOK_BLOCK, SORT_ROWS), F32)
        for k in range(TOP_K):
            unsort = unsort + jnp.where(c_iota == post[:, k:k + 1], gatet[:, k:k + 1], 0.0)
        slabs.append(unsort.astype(BF16))
    return slabs


def _unsort_matmul(slabs, ybuf):
    ffn = jnp.zeros((TOK_BLOCK, D_MODEL), F32)
    for i, r0 in enumerate(range(0, GROUP_ROWS, SORT_ROWS)):
        ffn = ffn + _dot(slabs[i], ybuf[r0:r0 + SORT_ROWS, :].astype(BF16))
    return ffn


def _mixer_kernel(yp_ref, *refs, chunk_decay, steps_per_seq, n_groups, fused):
    if fused:
        x_ref, yb_hbm, post_ref, gatet_ref, ln2g_ref, ln2b_ref = refs[:6]
        refs = refs[6:]
        ybuf, sem = refs[-2:]
        refs = refs[:-2]
    else:
        x_ref = refs[0]
        refs = refs[1:]
    (w_in_ref, p_ret_ref, p_sg_ref, w_o_ref, sgw_ref, sgbt_ref,
     sglng_ref, sglnb_ref, ln1g_ref, ln1b_ref, rwt_ref, rbb_ref,
     cq_ref, sq_ref, ck_ref, sk_ref, dmat_ref, qdec_ref, kdec_ref,
     x1_ref, pos_ref, gate_ref, cnt_ref, xs_ref,
     state_ref, ysg_ref, x1_prev_ref) = refs
    ts = TOK_BLOCK
    step = pl.program_id(0)

    @pl.when(step == 0)
    def _():
        x1_prev_ref[...] = jnp.zeros_like(x1_prev_ref)

    @pl.when(step % steps_per_seq == 0)
    def _():
        state_ref[...] = jnp.zeros_like(state_ref)

    if fused:
        slot = step % 2

        def gather(grp, s):
            for i in range(GROUP_PIECES):
                row = pl.multiple_of(yp_ref[grp * GROUP_PIECES + i] * PIECE, PIECE)
                pltpu.make_async_copy(yb_hbm.at[pl.ds(row, PIECE), :],
                                      ybuf.at[s, pl.ds(i * PIECE, PIECE), :], sem.at[s]).start()

        def wait(s):
            pltpu.make_async_copy(yb_hbm.at[pl.ds(0, GROUP_ROWS), :], ybuf.at[s], sem.at[s]).wait()

        @pl.when(step == 0)
        def _():
            gather(0, 0)

    routing = _route_and_sort(x1_prev_ref[...], rwt_ref, rbb_ref, pos_ref, gate_ref, cnt_ref, xs_ref)
    routing_stage = lambda: next(routing, None)

    if fused:
        slabs = _unsort_slabs(post_ref[...], gatet_ref[...])
        wait(slot)
        ffn = _unsort_matmul(slabs, ybuf.at[slot])
        xb = _layer_norm(DN_ALPHA * x_ref[...] + ffn, ln2g_ref[...], ln2b_ref[...])
        gather(jnp.minimum(step + 1, n_groups - 1), 1 - slot)
    else:
        xb = x_ref[...]
    xbf = xb.astype(BF16)

    def proj(off, width):
        return _dot(xbf, w_in_ref[0, :, off:off + width])

    cq, sq, ck, sk = cq_ref[...], sq_ref[...], ck_ref[...], sk_ref[...]
    acc_ret = jnp.zeros((ts, D_MODEL), F32)
    pair, head_in = {}, {}

    def project_head(h):
        if h % 2 == 0:
            pair["q"] = proj(OFF_Q + h * RET_QK_DIM, 2 * RET_QK_DIM)
            yield
            pair["k"] = proj(OFF_K + h * RET_QK_DIM, 2 * RET_QK_DIM)
            yield
        lanes = slice((h % 2) * RET_QK_DIM, (h % 2 + 1) * RET_QK_DIM)
        v = proj(OFF_V + h * RET_V_DIM, RET_V_DIM).astype(BF16)
        yield
        g = proj(OFF_G + h * RET_V_DIM, RET_V_DIM)
        head_in[h] = (pair["q"][:, lanes], pair["k"][:, lanes], v, g)
        yield

    for _ in project_head(0):
        pass
    for h in range(RET_HEADS):
        upcoming = project_head(h + 1) if h + 1 < RET_HEADS else iter(())
        q, k, v, g = head_in.pop(h)
        routing_stage()
        next(upcoming, None)
        qr = q * cq + pltpu.roll(q, RET_QK_DIM // 2, 1) * sq
        kr = k * ck + pltpu.roll(k, RET_QK_DIM // 2, 1) * sk
        scores = _dot_nt(qr.astype(BF16), kr.astype(BF16)) * dmat_ref[h]
        next(upcoming, None)
        inner = _dot(scores.astype(BF16), v)
        st = state_ref[h]
        cross = _dot((qr * qdec_ref[h]).astype(BF16), st.astype(BF16))
        kd = (kr * kdec_ref[h]).astype(BF16)
        state_ref[h] = st * chunk_decay[h] + _dot_tn(kd, v)
        routing_stage()
        next(upcoming, None)
        y = inner + cross
        mu = jnp.mean(y, axis=-1, keepdims=True)
        yc = y - mu
        var = jnp.mean(yc * yc, axis=-1, keepdims=True)
        yn = yc * lax.rsqrt(var + LN_EPS)
        next(upcoming, None)
        yr = yn * (g * jax.nn.sigmoid(g))
        acc_ret = acc_ret + _dot(yr.astype(BF16), p_ret_ref[0, h * RET_V_DIM:(h + 1) * RET_V_DIM, :])
        for _ in upcoming:
            pass
        if fused and h == 1:
            @pl.when(step == n_groups)
            def _():
                wait(1 - slot)

    pre_u = proj(OFF_U, SG_WIDTH)
    pre_vs = proj(OFF_VS, SG_WIDTH)
    ga = proj(OFF_GA, D_MODEL)
    u = _gelu_tanh(pre_u)
    routing_stage()
    gb = proj(OFF_GB, D_MODEL)
    vn = _layer_norm(_gelu_tanh(pre_vs), sglng_ref[...], sglnb_ref[...])
    qi = lax.broadcasted_iota(jnp.int32, (SG_BLOCK, SG_BLOCK), 0) // SG_CHUNK
    pi = lax.broadcasted_iota(jnp.int32, (SG_BLOCK, SG_BLOCK), 1) // SG_CHUNK
    sg_mask = qi >= pi
    for gi in range(SG_GROUPS):
        wm = jnp.where(sg_mask, sgw_ref[gi], 0.0).astype(BF16)
        bias = sgbt_ref[:, gi:gi + 1]
        cols = slice(gi * SG_BLOCK, (gi + 1) * SG_BLOCK)
        blocks = [slice(blk * SG_BLOCK, (blk + 1) * SG_BLOCK) for blk in range(ts // SG_BLOCK)]
        vcat = jnp.concatenate([vn[rows, cols] for rows in blocks], axis=1).astype(BF16)
        mixed = _dot(wm, vcat) + bias
        for blk, rows in enumerate(blocks):
            ysg_ref[rows, cols] = (u[rows, cols] * mixed[:, blk * SG_BLOCK:(blk + 1) * SG_BLOCK]).astype(BF16)
    routing_stage()
    acc_sg = _dot(ysg_ref[...], p_sg_ref[0])

    hmix = jax.nn.sigmoid(ga) * acc_ret + jax.nn.sigmoid(gb) * acc_sg
    mix = _dot(hmix.astype(BF16), w_o_ref[0])
    for _ in routing:
        pass
    x1 = _layer_norm(DN_ALPHA * xb + mix, ln1g_ref[...], ln1b_ref[...])
    x1_ref[...] = x1
    x1_prev_ref[...] = x1


def _route_and_sort(x1, rwt_ref, rbb_ref, pos_ref, gate_ref, cnt_ref, xs_ref):
    ts = TOK_BLOCK
    x_hi, x_lo = _split_bf16(x1)
    w_hi, w_lo = _split_bf16(rwt_ref[...])
    logits = (_dot_nt(w_hi, x_hi) + _dot_nt(w_hi, x_lo) + _dot_nt(w_lo, x_hi)) + rbb_ref[...]
    yield
    e_iota = lax.broadcasted_iota(jnp.int32, (N_EXPERTS, ts), 0).astype(F32)
    sels, vals = [], []
    work = logits
    for _ in range(TOP_K):
        m = jnp.max(work, axis=0, keepdims=True)
        idx = jnp.min(jnp.where(work == m, e_iota, float(N_EXPERTS)), axis=0, keepdims=True)
        sel = e_iota == idx
        work = jnp.where(sel, -jnp.inf, work)
        sels.append(sel)
        vals.append(m)
        yield
    exps = [jnp.exp(v - vals[0]) for v in vals]
    denom = exps[0] + exps[1] + exps[2] + exps[3]
    gate_ref[...] = jnp.concatenate([e / denom for e in exps], axis=0)

    onehots = [jnp.where(s, 1.0, 0.0) for s in sels]
    oh_sum = onehots[0] + onehots[1] + onehots[2] + onehots[3]
    t_row = lax.broadcasted_iota(jnp.int32, (ts, ts), 0)
    t_col = lax.broadcasted_iota(jnp.int32, (ts, ts), 1)
    upper = jnp.where(t_row < t_col, 1.0, 0.0).astype(BF16)
    before = _dot(oh_sum.astype(BF16), upper)
    cnt = jnp.sum(oh_sum, axis=1, keepdims=True)
    cnt_pad = jnp.floor((cnt + (PIECE - 1)) * (1.0 / PIECE)) * PIECE
    cnt_b = jnp.broadcast_to(cnt_pad, (N_EXPERTS, ts))
    e_row = lax.broadcasted_iota(jnp.int32, (N_EXPERTS, N_EXPERTS), 0)
    e_col = lax.broadcasted_iota(jnp.int32, (N_EXPERTS, N_EXPERTS), 1)
    lower = jnp.where(e_col < e_row, 1.0, 0.0).astype(BF16)
    strip_off = _dot(lower, cnt_b.astype(BF16))
    slot = strip_off + before
    pos = [jnp.sum(oh * slot, axis=0, keepdims=True).astype(jnp.int32) for oh in onehots]
    pos_ref[...] = jnp.concatenate(pos, axis=0)
    cnt_ref[0] = cnt_b[:, :128].astype(jnp.int32)
    yield
    for r0 in range(0, GROUP_ROWS, SORT_ROWS):
        r_iota = lax.broadcasted_iota(jnp.int32, (SORT_ROWS, ts), 0) + r0
        hit = (r_iota == pos[0]) | (r_iota == pos[1]) | (r_iota == pos[2]) | (r_iota == pos[3])
        perm = jnp.where(hit, 1.0, 0.0).astype(BF16)
        xs_ref[r0:r0 + SORT_ROWS, :] = _dot(perm, x_hi)
        yield


def _mixer_call(x2d, batch, seq, layer, lw, tables, prev_moe=None):
    t = batch * seq
    ns = seq // TOK_BLOCK
    groups = t // TOK_BLOCK
    fused = prev_moe is not None
    const = lambda *shape: pl.BlockSpec(shape, lambda i, yp: (0,) * len(shape),
                                        pipeline_mode=pl.Buffered(1))
    stacked = lambda *shape: pl.BlockSpec((1,) + shape, lambda i, yp: (layer, 0, 0),
                                          pipeline_mode=pl.Buffered(1))
    cur = lambda i: jnp.minimum(i, groups - 1)
    prev = lambda i: jnp.maximum(i - 1, 0)
    seqtab = pl.BlockSpec((TOK_BLOCK, RET_QK_DIM), lambda i, yp: (cur(i) % ns, 0))
    lanes = lambda rows: pl.BlockSpec((rows, TOK_BLOCK), lambda i, yp: (0, prev(i)))
    tok_cur = lambda width: pl.BlockSpec((TOK_BLOCK, width), lambda i, yp: (cur(i), 0))
    in_specs = [tok_cur(D_MODEL)]
    operands = [x2d]
    scratch = [pltpu.VMEM((RET_HEADS, RET_QK_DIM, RET_V_DIM), F32),
               pltpu.VMEM((TOK_BLOCK, SG_WIDTH), BF16),
               pltpu.VMEM((TOK_BLOCK, D_MODEL), F32)]
    if fused:
        yb, ypiece, post, gatet, ln2_g, ln2_b = prev_moe
        in_specs += [pl.BlockSpec(memory_space=pl.ANY), tok_cur(TOP_K), tok_cur(TOP_K),
                     const(1, D_MODEL), const(1, D_MODEL)]
        operands += [yb, post, gatet, ln2_g, ln2_b]
        scratch += [pltpu.VMEM((2, GROUP_ROWS, D_MODEL), F32), pltpu.SemaphoreType.DMA((2,))]
    else:
        ypiece = jnp.zeros((1,), jnp.int32)
    in_specs += [
        stacked(D_MODEL, IN_WIDTH), stacked(RET_V, D_MODEL), stacked(SG_WIDTH, D_MODEL),
        stacked(D_MODEL, D_MODEL),
        const(SG_GROUPS, SG_BLOCK, SG_BLOCK), const(SG_BLOCK, SG_GROUPS),
        const(1, SG_WIDTH), const(1, SG_WIDTH), const(1, D_MODEL), const(1, D_MODEL),
        const(N_EXPERTS, D_MODEL), const(N_EXPERTS, TOK_BLOCK),
        seqtab, seqtab, seqtab, seqtab,
        const(RET_HEADS, TOK_BLOCK, TOK_BLOCK), const(RET_HEADS, TOK_BLOCK, RET_QK_DIM),
        const(RET_HEADS, TOK_BLOCK, RET_QK_DIM),
    ]
    out_shape = [
        jax.ShapeDtypeStruct((t + TOK_BLOCK, D_MODEL), F32),
        jax.ShapeDtypeStruct((TOP_K, t), jnp.int32),
        jax.ShapeDtypeStruct((TOP_K, t), F32),
        jax.ShapeDtypeStruct((groups, N_EXPERTS, 128), jnp.int32),
        jax.ShapeDtypeStruct((groups * GROUP_ROWS, D_MODEL), F32),
    ]
    out_specs = [
        pl.BlockSpec((TOK_BLOCK, D_MODEL), lambda i, yp: (i, 0)), lanes(TOP_K), lanes(TOP_K),
        pl.BlockSpec((1, N_EXPERTS, 128), lambda i, yp: (prev(i), 0, 0)),
        pl.BlockSpec((GROUP_ROWS, D_MODEL), lambda i, yp: (prev(i), 0)),
    ]
    operands += [lw["w_in"], lw["p_ret"], lw["p_sg"], lw["w_o"], lw["sg_w"], lw["sg_bt"],
                 lw["sg_ln_g"], lw["sg_ln_b"], lw["ln1_g"], lw["ln1_b"], lw["router_wt"], lw["router_bb"],
                 tables["cq"], tables["sq"], tables["ck"], tables["sk"],
                 tables["dmat"], tables["qdec"], tables["kdec"]]
    kern = functools.partial(_mixer_kernel, chunk_decay=tables["chunk_decay"], steps_per_seq=ns,
                             n_groups=groups, fused=fused)
    grid_spec = pltpu.PrefetchScalarGridSpec(
        num_scalar_prefetch=1, grid=(groups + 1,), in_specs=in_specs, out_specs=out_specs,
        scratch_shapes=scratch)
    return pl.pallas_call(
        kern,
        grid_spec=grid_spec,
        out_shape=out_shape,
        compiler_params=pltpu.CompilerParams(dimension_semantics=("arbitrary",),
                                             vmem_limit_bytes=VMEM_LIMIT_V7X),
        name="mixer",
    )(ypiece, *operands)


def _expert_kernel(bexp_ref, src_ref, nvb_ref, xs_hbm, w1_ref, b1_ref, w2_ref, b2_ref,
                   yb_ref, xbuf0, xbuf1, w1b, w2b, sem):
    b = pl.program_id(0)
    nvb = nvb_ref[0]
    xbufs = (xbuf0, xbuf1)

    def gather(blk, slot, pieces=range(EXP_PIECES)):
        for i in pieces:
            row = pl.multiple_of(src_ref[blk * EXP_PIECES + i] * PIECE, PIECE)
            pltpu.make_async_copy(xs_hbm.at[pl.ds(row, PIECE), :],
                                  xbufs[slot].at[pl.ds(i * PIECE, PIECE), :],
                                  sem.at[slot]).start()

    def wait(slot):
        pltpu.make_async_copy(xs_hbm.at[pl.ds(0, EXP_BLOCK), :], xbufs[slot], sem.at[slot]).wait()

    @pl.when(b == 0)
    def _():
        gather(0, 0)

    new_expert = (b == 0) | (bexp_ref[b] != bexp_ref[jnp.maximum(b - 1, 0)])

    @pl.when(new_expert & (b < nvb))
    def _():
        def cast_rows(i, carry):
            r = pl.multiple_of(i * W_CAST_ROWS, W_CAST_ROWS)
            w1b[pl.ds(r, W_CAST_ROWS), :] = w1_ref[0, 0, pl.ds(r, W_CAST_ROWS), :].astype(BF16)
            w2b[pl.ds(r, W_CAST_ROWS), :] = w2_ref[0, 0, pl.ds(r, W_CAST_ROWS), :].astype(BF16)
            return carry
        lax.fori_loop(0, D_MODEL // W_CAST_ROWS, cast_rows, 0)

    def block(cur):
        nxt = 1 - cur
        gather(jnp.minimum(b + 1, nvb - 1), nxt)
        wait(cur)
        x = xbufs[cur][...].astype(BF16)
        y = jnp.zeros((EXP_BLOCK, D_MODEL), F32)
        for j in range(D_EXPERT // F_CHUNK):
            c0 = j * F_CHUNK
            hg = _dot(x, w1b[:, c0:c0 + F_CHUNK]) + b1_ref[0, 0, :, c0:c0 + F_CHUNK]
            hu = (_dot(x, w1b[:, D_EXPERT + c0:D_EXPERT + c0 + F_CHUNK])
                  + b1_ref[0, 0, :, D_EXPERT + c0:D_EXPERT + c0 + F_CHUNK])
            gate = jnp.minimum(hg, SWIGLU_LIMIT)
            up = jnp.clip(hu, -SWIGLU_LIMIT, SWIGLU_LIMIT)
            act = gate * jax.nn.sigmoid(SWIGLU_ALPHA * gate) * (up + 1.0)
            y = y + _dot(act.astype(BF16), w2b[c0:c0 + F_CHUNK, :])
        yb_ref[...] = y + b2_ref[0, 0]

        @pl.when(b == nvb - 1)
        def _():
            wait(nxt)

    for cur in range(2):
        pl.when((b < nvb) & (b % 2 == cur))(functools.partial(block, cur))

    @pl.when(b >= nvb)
    def _():
        yb_ref[...] = jnp.zeros_like(yb_ref)


def _expert_call(xs, block_exp, src_piece, nvb, layer, e_w1, e_b1, e_w2, e_b2, n_blocks):
    wspec = lambda *shape: pl.BlockSpec((1, 1) + shape, lambda b, be, sp, nv: (layer, be[b], 0, 0))
    grid_spec = pltpu.PrefetchScalarGridSpec(
        num_scalar_prefetch=3,
        grid=(n_blocks,),
        in_specs=[
            pl.BlockSpec(memory_space=pl.ANY),
            wspec(D_MODEL, 2 * D_EXPERT), wspec(1, 2 * D_EXPERT),
            wspec(D_EXPERT, D_MODEL), wspec(1, D_MODEL),
        ],
        out_specs=pl.BlockSpec((EXP_BLOCK, D_MODEL), lambda b, be, sp, nv: (b, 0)),
        scratch_shapes=[pltpu.VMEM((EXP_BLOCK, D_MODEL), F32),
                        pltpu.VMEM((EXP_BLOCK, D_MODEL), F32),
                        pltpu.VMEM((D_MODEL, 2 * D_EXPERT), BF16),
                        pltpu.VMEM((D_EXPERT, D_MODEL), BF16),
                        pltpu.SemaphoreType.DMA((2,))],
    )
    return pl.pallas_call(
        _expert_kernel,
        grid_spec=grid_spec,
        out_shape=jax.ShapeDtypeStruct((n_blocks * EXP_BLOCK, D_MODEL), F32),
        compiler_params=pltpu.CompilerParams(dimension_semantics=("arbitrary",),
                                             vmem_limit_bytes=VMEM_LIMIT_V7X),
        name="experts",
    )(block_exp, src_piece, nvb, xs, e_w1, e_b1, e_w2, e_b2)


def _combine_kernel(yp_ref, x1_ref, yb_hbm, post_ref, gatet_ref, g_ref, b_ref, out_ref,
                    ybuf0, ybuf1, sem):
    g = pl.program_id(0)
    ng = pl.num_programs(0)
    ybufs = (ybuf0, ybuf1)

    def gather(grp, slot, pieces=range(GROUP_PIECES)):
        for i in pieces:
            row = pl.multiple_of(yp_ref[grp * GROUP_PIECES + i] * PIECE, PIECE)
            pltpu.make_async_copy(yb_hbm.at[pl.ds(row, PIECE), :],
                                  ybufs[slot].at[pl.ds(i * PIECE, PIECE), :],
                                  sem.at[slot]).start()

    def wait(slot):
        pltpu.make_async_copy(yb_hbm.at[pl.ds(0, GROUP_ROWS), :], ybufs[slot], sem.at[slot]).wait()

    @pl.when(g == 0)
    def _():
        gather(0, 0)

    def group(cur):
        nxt = 1 - cur
        gather(jnp.minimum(g + 1, ng - 1), nxt)
        slabs = _unsort_slabs(post_ref[...], gatet_ref[...])
        wait(cur)
        ffn = _unsort_matmul(slabs, ybufs[cur])
        out_ref[...] = _layer_norm(DN_ALPHA * x1_ref[...] + ffn, g_ref[...], b_ref[...])

        @pl.when(g == ng - 1)
        def _():
            wait(nxt)

    for cur in range(2):
        pl.when(g % 2 == cur)(functools.partial(group, cur))


def _combine_call(x1, yb, ypiece, post, gatet, ln_g, ln_b):
    t = post.shape[0]
    groups = t // TOK_BLOCK
    grid_spec = pltpu.PrefetchScalarGridSpec(
        num_scalar_prefetch=1,
        grid=(groups,),
        in_specs=[
            pl.BlockSpec((TOK_BLOCK, D_MODEL), lambda g, yp: (g, 0)),
            pl.BlockSpec(memory_space=pl.ANY),
            pl.BlockSpec((TOK_BLOCK, TOP_K), lambda g, yp: (g, 0)),
            pl.BlockSpec((TOK_BLOCK, TOP_K), lambda g, yp: (g, 0)),
            pl.BlockSpec((1, D_MODEL), lambda g, yp: (0, 0)),
            pl.BlockSpec((1, D_MODEL), lambda g, yp: (0, 0)),
        ],
        out_specs=pl.BlockSpec((TOK_BLOCK, D_MODEL), lambda g, yp: (g, 0)),
        scratch_shapes=[pltpu.VMEM((GROUP_ROWS, D_MODEL), F32), pltpu.VMEM((GROUP_ROWS, D_MODEL), F32),
                        pltpu.SemaphoreType.DMA((2,))],
    )
    return pl.pallas_call(
        _combine_kernel,
        grid_spec=grid_spec,
        out_shape=jax.ShapeDtypeStruct((t, D_MODEL), F32),
        compiler_params=pltpu.CompilerParams(dimension_semantics=("arbitrary",),
                                             vmem_limit_bytes=VMEM_LIMIT_V7X),
        name="combine",
    )(ypiece, x1, yb, post, gatet, ln_g, ln_b)


def _routing_tables(cnt_pad, n_blocks):
    groups = cnt_pad.shape[0]
    npc = cnt_pad // PIECE
    strip_start = jnp.cumsum(npc, axis=1) - npc
    tot = jnp.sum(npc, axis=0)
    tot_pad = (tot + EXP_PIECES - 1) // EXP_PIECES * EXP_PIECES
    exp_end = jnp.cumsum(tot_pad)
    exp_start = exp_end - tot_pad
    dstart = exp_start[None, :] + jnp.cumsum(npc, axis=0) - npc
    nvb = (exp_end[-1] // EXP_PIECES).astype(jnp.int32)

    blocks = jnp.arange(n_blocks, dtype=jnp.int32)
    bexp = jnp.sum(blocks[:, None] * EXP_PIECES >= exp_end[None, :], axis=1).astype(jnp.int32)
    bexp = jnp.minimum(bexp, N_EXPERTS - 1)
    last = jnp.sum((nvb - 1) * EXP_PIECES >= exp_end).astype(jnp.int32)
    bexp = jnp.where(blocks < nvb, bexp, jnp.minimum(last, N_EXPERTS - 1))

    d = jnp.arange(n_blocks * EXP_PIECES, dtype=jnp.int32)
    e_of = jnp.minimum(jnp.sum(d[:, None] >= exp_end[None, :], axis=1), N_EXPERTS - 1)
    oh_e = (e_of[:, None] == jnp.arange(N_EXPERTS, dtype=jnp.int32)[None, :]).astype(F32)
    row_of = lambda tab: jnp.dot(oh_e, tab.T.astype(F32), precision=lax.Precision.HIGHEST).astype(jnp.int32)
    ds_e = row_of(dstart)
    g_of = jnp.maximum(jnp.sum(ds_e <= d[:, None], axis=1) - 1, 0)
    at_g = g_of[:, None] == jnp.arange(groups, dtype=jnp.int32)[None, :]
    pick = lambda rows: jnp.sum(jnp.where(at_g, rows, 0), axis=1)
    i_of = d - pick(ds_e)
    valid = (i_of >= 0) & (i_of < pick(row_of(npc)))
    src = g_of * GROUP_PIECES + pick(row_of(strip_start)) + i_of
    src_piece = jnp.where(valid, src, 0).astype(jnp.int32)

    j = jnp.arange(GROUP_PIECES, dtype=jnp.int32)
    e_loc = jnp.maximum(jnp.sum(strip_start[:, None, :] <= j[None, :, None], axis=2) - 1, 0)
    at_e = e_loc[:, :, None] == jnp.arange(N_EXPERTS, dtype=jnp.int32)[None, None, :]
    pick_e = lambda tab: jnp.sum(jnp.where(at_e, tab[:, None, :], 0), axis=2)
    st_loc = pick_e(strip_start)
    np_loc = pick_e(npc)
    ds_loc = pick_e(dstart)
    i_loc = j[None, :] - st_loc
    ypiece = jnp.where(i_loc < np_loc, ds_loc + i_loc, 0).astype(jnp.int32).reshape(groups * GROUP_PIECES)
    return bexp, src_piece, nvb.reshape(1), ypiece


def _tables(seq):
    half = RET_QK_DIM // 2
    inv = ROPE_BASE ** (-jnp.arange(half, dtype=F32) / half)
    ang = jnp.arange(seq, dtype=jnp.int32).astype(F32)[:, None] * inv[None, :]
    cos, sin = jnp.cos(ang), jnp.sin(ang)
    cosf = jnp.concatenate([cos, cos], axis=1)
    sinf = jnp.concatenate([-sin, sin], axis=1)
    kscale = RET_QK_DIM ** -0.5
    log_g = jnp.log(1.0 - jnp.exp(jnp.linspace(math.log(1.0 / 32), math.log(1.0 / 512), RET_HEADS)))
    idx = jnp.arange(TOK_BLOCK, dtype=F32)
    diff = idx[:, None] - idx[None, :]
    dmat = jnp.where(diff >= 0, jnp.exp(log_g[:, None, None] * jnp.maximum(diff, 0.0)), 0.0)
    qdec = jnp.exp(log_g[:, None] * (idx[None, :] + 1.0))
    kdec = jnp.exp(log_g[:, None] * (TOK_BLOCK - 1.0 - idx[None, :]))
    bc = lambda a: jnp.broadcast_to(a[:, :, None], (RET_HEADS, TOK_BLOCK, RET_QK_DIM))
    log_g_host = np.log(1.0 - np.exp(np.linspace(math.log(1.0 / 32), math.log(1.0 / 512), RET_HEADS)))
    chunk_decay = tuple(float(np.float32(np.exp(np.float32(lg) * np.float32(TOK_BLOCK)))) for lg in log_g_host)
    return {"cq": cosf, "sq": sinf, "ck": cosf * kscale, "sk": sinf * kscale,
            "dmat": dmat.astype(F32), "qdec": bc(qdec), "kdec": bc(kdec), "chunk_decay": chunk_decay}


def kernel(x, w_in, p_ret, sg_ln_g, sg_ln_b, sg_w, sg_b, p_sg, w_o, ln1_g, ln1_b,
           router_w, router_b, e_w1, e_b1, e_w2, e_b2, ln2_g, ln2_b):
    batch, seq, d = x.shape
    assert d == D_MODEL and seq % TOK_BLOCK == 0
    depth = w_in.shape[0]
    t = batch * seq
    groups = t // TOK_BLOCK
    n_blocks = (groups * GROUP_PIECES + N_EXPERTS * (EXP_PIECES - 1)) // EXP_PIECES + 1
    tables = _tables(seq)
    xc = x.reshape(t, D_MODEL)
    w_in_b, p_ret_b, p_sg_b, w_o_b = (w.astype(BF16) for w in (w_in, p_ret, p_sg, w_o))
    e_b1r = e_b1.reshape(depth, N_EXPERTS, 1, 2 * D_EXPERT)
    e_b2r = e_b2.reshape(depth, N_EXPERTS, 1, D_MODEL)
    prev_moe = None
    for l in range(depth):
        lw = {
            "w_in": w_in_b, "p_ret": p_ret_b, "p_sg": p_sg_b,
            "w_o": w_o_b, "sg_w": sg_w[l], "sg_bt": sg_b[l].T,
            "sg_ln_g": sg_ln_g[l][None], "sg_ln_b": sg_ln_b[l][None],
            "ln1_g": ln1_g[l][None], "ln1_b": ln1_b[l][None],
            "router_wt": router_w[l].T,
            "router_bb": jnp.broadcast_to(router_b[l][:, None], (N_EXPERTS, TOK_BLOCK)),
        }
        x1, pos, gates, cnt, xs = _mixer_call(xc, batch, seq, l, lw, tables, prev_moe)
        bexp, src_piece, nvb, ypiece = _routing_tables(cnt[:, :, 0], n_blocks)
        yb = _expert_call(xs, bexp, src_piece, nvb, l, e_w1, e_b1r, e_w2, e_b2r, n_blocks)
        prev_moe = (yb, ypiece, pos.T, gates.T, ln2_g[l][None], ln2_b[l][None])
        xc = x1
    out = _combine_call(xc, *prev_moe)
    return out.reshape(batch, seq, D_MODEL)
```

```python
import functools
import math

import jax
import jax.numpy as jnp
import numpy as np
from jax import lax
from jax.experimental import pallas as pl
from jax.experimental.pallas import tpu as pltpu

F32 = jnp.float32
BF16 = jnp.bfloat16

D_MODEL = 1024
DEPTH = 4
RET_HEADS = 4
RET_QK_DIM = 128
RET_V_DIM = 256
RET_QK = RET_HEADS * RET_QK_DIM
RET_V = RET_HEADS * RET_V_DIM
ROPE_BASE = 10000.0
SG_BLOCK = 128
SG_CHUNK = 64
SG_GROUPS = 8
SG_WIDTH = D_MODEL
N_EXPERTS = 32
TOP_K = 4
D_EXPERT = D_MODEL
SWIGLU_LIMIT = 7.0
SWIGLU_ALPHA = 1.702
LN_EPS = 1e-5
DN_ALPHA = (2 * DEPTH) ** 0.25

OFF_Q = 0
OFF_K = OFF_Q + RET_QK
OFF_V = OFF_K + RET_QK
OFF_G = OFF_V + RET_V
OFF_U = OFF_G + RET_V
OFF_VS = OFF_U + SG_WIDTH
OFF_GA = OFF_VS + SG_WIDTH
OFF_GB = OFF_GA + D_MODEL
IN_WIDTH = OFF_GB + D_MODEL

SUBLANES_V7X = 8
TOK_BLOCK = 256
PIECE = SUBLANES_V7X
GROUP_ROWS = TOK_BLOCK * TOP_K + N_EXPERTS * PIECE
GROUP_PIECES = GROUP_ROWS // PIECE
SORT_ROWS = 256
EXP_BLOCK = 512
EXP_PIECES = EXP_BLOCK // PIECE
F_CHUNK = 512
W_CAST_ROWS = 128
VMEM_LIMIT_V7X = 58 * 1024 * 1024


def _layer_norm(x, g, b):
    mu = jnp.mean(x, axis=-1, keepdims=True)
    xc = x - mu
    var = jnp.mean(xc * xc, axis=-1, keepdims=True)
    return xc * lax.rsqrt(var + LN_EPS) * g + b


def _gelu_tanh(x):
    c = math.sqrt(2.0 / math.pi)
    return 0.5 * x * (1.0 + jnp.tanh(c * (x + 0.044715 * (x * x * x))))


def _dot(a, b):
    return jnp.dot(a, b, preferred_element_type=F32)


def _dot_nt(a, b):
    return lax.dot_general(a, b, (((1,), (1,)), ((), ())), preferred_element_type=F32)


def _dot_tn(a, b):
    return lax.dot_general(a, b, (((0,), (0,)), ((), ())), preferred_element_type=F32)


def _split_bf16(x):
    hi = x.astype(BF16)
    lo = (x - hi.astype(F32)).astype(BF16)
    return hi, lo


def _unsort_slab(post, gatet, r0):
    c_iota = lax.broadcasted_iota(jnp.int32, (TOK_BLOCK, SORT_ROWS), 1) + r0
    unsort = jnp.zeros((TOK_BLOCK, SORT_ROWS), F32)
    for k in range(TOP_K):
        unsort = jnp.where(c_iota == post[:, k:k + 1], gatet[:, k:k + 1], unsort)
    return unsort.astype(BF16)


def _unsort_slabs(post, gatet):
    return [_unsort_slab(post, gatet, r0) for r0 in range(0, GROUP_ROWS, SORT_ROWS)]


def _unsort_matmul(slabs, ybuf):
    ffn = jnp.zeros((TOK_BLOCK, D_MODEL), F32)
    for i, r0 in enumerate(range(0, GROUP_ROWS, SORT_ROWS)):
        ffn = ffn + _dot(slabs[i], ybuf[r0:r0 + SORT_ROWS, :].astype(BF16))
    return ffn


def _mixer_kernel(yp_ref, *refs, chunk_decay, steps_per_seq, n_groups, fused):
    if fused:
        x_ref, yb_hbm, post_ref, gatet_ref, ln2g_ref, ln2b_ref = refs[:6]
        refs = refs[6:]
        ybuf, sem = refs[-2:]
        refs = refs[:-2]
    else:
        x_ref = refs[0]
        refs = refs[1:]
    (w_in_ref, p_ret_ref, p_sg_ref, w_o_ref, sgw_ref, sgbt_ref,
     sglng_ref, sglnb_ref, ln1g_ref, ln1b_ref, rwt_ref, rbb_ref,
     cq_ref, sq_ref, ck_ref, sk_ref, dmat_ref, qdec_ref, kdec_ref,
     x1_ref, pos_ref, gate_ref, cnt_ref, xs_ref,
     state_ref, ysg_ref, x1_prev_ref) = refs
    ts = TOK_BLOCK
    step = pl.program_id(0)

    @pl.when(step == 0)
    def _():
        x1_prev_ref[...] = jnp.zeros_like(x1_prev_ref)

    @pl.when(step % steps_per_seq == 0)
    def _():
        state_ref[...] = jnp.zeros_like(state_ref)

    if fused:
        slot = step % 2

        def gather(grp, s):
            for i in range(GROUP_PIECES):
                row = pl.multiple_of(yp_ref[grp * GROUP_PIECES + i] * PIECE, PIECE)
                pltpu.make_async_copy(yb_hbm.at[pl.ds(row, PIECE), :],
                                      ybuf.at[s, pl.ds(i * PIECE, PIECE), :], sem.at[s]).start()

        def wait(s):
            pltpu.make_async_copy(yb_hbm.at[pl.ds(0, GROUP_ROWS), :], ybuf.at[s], sem.at[s]).wait()

        @pl.when(step == 0)
        def _():
            gather(0, 0)

    routing = _route_and_sort(x1_prev_ref[...], rwt_ref, rbb_ref, pos_ref, gate_ref, cnt_ref, xs_ref)
    routing_stage = lambda: next(routing, None)

    if fused:
        slabs = _unsort_slabs(post_ref[...], gatet_ref[...])
        wait(slot)
        ffn = _unsort_matmul(slabs, ybuf.at[slot])
        xb = _layer_norm(DN_ALPHA * x_ref[...] + ffn, ln2g_ref[...], ln2b_ref[...])
        gather(jnp.minimum(step + 1, n_groups - 1), 1 - slot)
    else:
        xb = x_ref[...]
    xbf = xb.astype(BF16)

    def proj(off, width):
        return _dot(xbf, w_in_ref[0, :, off:off + width])

    cq, sq, ck, sk = cq_ref[...], sq_ref[...], ck_ref[...], sk_ref[...]
    acc_ret = jnp.zeros((ts, D_MODEL), F32)
    pair, head_in = {}, {}

    def project_head(h):
        if h % 2 == 0:
            pair["q"] = proj(OFF_Q + h * RET_QK_DIM, 2 * RET_QK_DIM)
            yield
            pair["k"] = proj(OFF_K + h * RET_QK_DIM, 2 * RET_QK_DIM)
            yield
        lanes = slice((h % 2) * RET_QK_DIM, (h % 2 + 1) * RET_QK_DIM)
        v = proj(OFF_V + h * RET_V_DIM, RET_V_DIM).astype(BF16)
        yield
        g = proj(OFF_G + h * RET_V_DIM, RET_V_DIM)
        head_in[h] = (pair["q"][:, lanes], pair["k"][:, lanes], v, g)
        yield

    for _ in project_head(0):
        pass
    for h in range(RET_HEADS):
        upcoming = project_head(h + 1) if h + 1 < RET_HEADS else iter(())
        q, k, v, g = head_in.pop(h)
        routing_stage()
        next(upcoming, None)
        qr = q * cq + pltpu.roll(q, RET_QK_DIM // 2, 1) * sq
        kr = k * ck + pltpu.roll(k, RET_QK_DIM // 2, 1) * sk
        scores = _dot_nt(qr.astype(BF16), kr.astype(BF16)) * dmat_ref[h]
        next(upcoming, None)
        inner = _dot(scores.astype(BF16), v)
        st = state_ref[h]
        cross = _dot((qr * qdec_ref[h]).astype(BF16), st.astype(BF16))
        kd = (kr * kdec_ref[h]).astype(BF16)
        state_ref[h] = st * chunk_decay[h] + _dot_tn(kd, v)
        routing_stage()
        next(upcoming, None)
        y = inner + cross
        mu = jnp.mean(y, axis=-1, keepdims=True)
        yc = y - mu
        var = jnp.mean(yc * yc, axis=-1, keepdims=True)
        yn = yc * lax.rsqrt(var + LN_EPS)
        next(upcoming, None)
        yr = yn * (g * jax.nn.sigmoid(g))
        acc_ret = acc_ret + _dot(yr.astype(BF16), p_ret_ref[0, h * RET_V_DIM:(h + 1) * RET_V_DIM, :])
        for _ in upcoming:
            pass
        if fused and h == 1:
            @pl.when(step == n_groups)
            def _():
                wait(1 - slot)

    pre_u = proj(OFF_U, SG_WIDTH)
    pre_vs = proj(OFF_VS, SG_WIDTH)
    ga = proj(OFF_GA, D_MODEL)
    u = _gelu_tanh(pre_u)
    routing_stage()
    gb = proj(OFF_GB, D_MODEL)
    vn = _layer_norm(_gelu_tanh(pre_vs), sglng_ref[...], sglnb_ref[...])
    qi = lax.broadcasted_iota(jnp.int32, (SG_BLOCK, SG_BLOCK), 0) // SG_CHUNK
    pi = lax.broadcasted_iota(jnp.int32, (SG_BLOCK, SG_BLOCK), 1) // SG_CHUNK
    sg_mask = qi >= pi
    for gi in range(SG_GROUPS):
        wm = jnp.where(sg_mask, sgw_ref[gi], 0.0).astype(BF16)
        bias = sgbt_ref[:, gi:gi + 1]
        cols = slice(gi * SG_BLOCK, (gi + 1) * SG_BLOCK)
        blocks = [slice(blk * SG_BLOCK, (blk + 1) * SG_BLOCK) for blk in range(ts // SG_BLOCK)]
        vcat = jnp.concatenate([vn[rows, cols] for rows in blocks], axis=1).astype(BF16)
        mixed = _dot(wm, vcat) + bias
        for blk, rows in enumerate(blocks):
            ysg_ref[rows, cols] = (u[rows, cols] * mixed[:, blk * SG_BLOCK:(blk + 1) * SG_BLOCK]).astype(BF16)
    routing_stage()
    acc_sg = _dot(ysg_ref[...], p_sg_ref[0])

    hmix = jax.nn.sigmoid(ga) * acc_ret + jax.nn.sigmoid(gb) * acc_sg
    mix = _dot(hmix.astype(BF16), w_o_ref[0])
    for _ in routing:
        pass
    x1 = _layer_norm(DN_ALPHA * xb + mix, ln1g_ref[...], ln1b_ref[...])
    x1_ref[...] = x1
    x1_prev_ref[...] = x1


def _route_and_sort(x1, rwt_ref, rbb_ref, pos_ref, gate_ref, cnt_ref, xs_ref):
    ts = TOK_BLOCK
    x_hi, x_lo = _split_bf16(x1)
    w_hi, w_lo = _split_bf16(rwt_ref[...])
    logits = (_dot_nt(w_hi, x_hi) + _dot_nt(w_hi, x_lo) + _dot_nt(w_lo, x_hi)) + rbb_ref[...]
    yield
    e_iota = lax.broadcasted_iota(jnp.int32, (N_EXPERTS, ts), 0).astype(F32)
    sels, vals = [], []
    work = logits
    for _ in range(TOP_K):
        m = jnp.max(work, axis=0, keepdims=True)
        idx = jnp.min(jnp.where(work == m, e_iota, float(N_EXPERTS)), axis=0, keepdims=True)
        sel = e_iota == idx
        work = jnp.where(sel, -jnp.inf, work)
        sels.append(sel)
        vals.append(m)
        yield
    exps = [jnp.exp(v - vals[0]) for v in vals]
    denom = exps[0] + exps[1] + exps[2] + exps[3]
    gate_ref[...] = jnp.concatenate([e / denom for e in exps], axis=0)

    onehots = [jnp.where(s, 1.0, 0.0) for s in sels]
    oh_sum = onehots[0] + onehots[1] + onehots[2] + onehots[3]
    t_row = lax.broadcasted_iota(jnp.int32, (ts, ts), 0)
    t_col = lax.broadcasted_iota(jnp.int32, (ts, ts), 1)
    upper = jnp.where(t_row < t_col, 1.0, 0.0).astype(BF16)
    before = _dot(oh_sum.astype(BF16), upper)
    cnt = jnp.sum(oh_sum, axis=1, keepdims=True)
    cnt_pad = jnp.floor((cnt + (PIECE - 1)) * (1.0 / PIECE)) * PIECE
    cnt_b = jnp.broadcast_to(cnt_pad, (N_EXPERTS, ts))
    e_row = lax.broadcasted_iota(jnp.int32, (N_EXPERTS, N_EXPERTS), 0)
    e_col = lax.broadcasted_iota(jnp.int32, (N_EXPERTS, N_EXPERTS), 1)
    lower = jnp.where(e_col < e_row, 1.0, 0.0).astype(BF16)
    strip_off = _dot(lower, cnt_b.astype(BF16))
    slot = strip_off + before
    pos = [jnp.sum(oh * slot, axis=0, keepdims=True).astype(jnp.int32) for oh in onehots]
    pos_ref[...] = jnp.concatenate(pos, axis=0)
    cnt_ref[0] = cnt_b[:, :128].astype(jnp.int32)
    yield
    for r0 in range(0, GROUP_ROWS, SORT_ROWS):
        r_iota = lax.broadcasted_iota(jnp.int32, (SORT_ROWS, ts), 0) + r0
        perm = jnp.zeros((SORT_ROWS, ts), F32)
        for k in range(TOP_K):
            perm = jnp.where(r_iota == pos[k], 1.0, perm)
        perm = perm.astype(BF16)
        xs_ref[r0:r0 + SORT_ROWS, :] = _dot(perm, x_hi)
        yield


def _mixer_call(x2d, batch, seq, layer, lw, tables, prev_moe=None):
    t = batch * seq
    ns = seq // TOK_BLOCK
    groups = t // TOK_BLOCK
    fused = prev_moe is not None
    const = lambda *shape: pl.BlockSpec(shape, lambda i, yp: (0,) * len(shape),
                                        pipeline_mode=pl.Buffered(1))
    stacked = lambda *shape: pl.BlockSpec((1,) + shape, lambda i, yp: (layer, 0, 0),
                                          pipeline_mode=pl.Buffered(1))
    cur = lambda i: jnp.minimum(i, groups - 1)
    prev = lambda i: jnp.maximum(i - 1, 0)
    seqtab = pl.BlockSpec((TOK_BLOCK, RET_QK_DIM), lambda i, yp: (cur(i) % ns, 0))
    lanes = lambda rows: pl.BlockSpec((rows, TOK_BLOCK), lambda i, yp: (0, prev(i)))
    tok_cur = lambda width: pl.BlockSpec((TOK_BLOCK, width), lambda i, yp: (cur(i), 0))
    in_specs = [tok_cur(D_MODEL)]
    operands = [x2d]
    scratch = [pltpu.VMEM((RET_HEADS, RET_QK_DIM, RET_V_DIM), F32),
               pltpu.VMEM((TOK_BLOCK, SG_WIDTH), BF16),
               pltpu.VMEM((TOK_BLOCK, D_MODEL), F32)]
    if fused:
        yb, ypiece, post, gatet, ln2_g, ln2_b = prev_moe
        in_specs += [pl.BlockSpec(memory_space=pl.ANY), tok_cur(TOP_K), tok_cur(TOP_K),
                     const(1, D_MODEL), const(1, D_MODEL)]
        operands += [yb, post, gatet, ln2_g, ln2_b]
        scratch += [pltpu.VMEM((2, GROUP_ROWS, D_MODEL), F32), pltpu.SemaphoreType.DMA((2,))]
    else:
        ypiece = jnp.zeros((1,), jnp.int32)
    in_specs += [
        stacked(D_MODEL, IN_WIDTH), stacked(RET_V, D_MODEL), stacked(SG_WIDTH, D_MODEL),
        stacked(D_MODEL, D_MODEL),
        const(SG_GROUPS, SG_BLOCK, SG_BLOCK), const(SG_BLOCK, SG_GROUPS),
        const(1, SG_WIDTH), const(1, SG_WIDTH), const(1, D_MODEL), const(1, D_MODEL),
        const(N_EXPERTS, D_MODEL), const(N_EXPERTS, TOK_BLOCK),
        seqtab, seqtab, seqtab, seqtab,
        const(RET_HEADS, TOK_BLOCK, TOK_BLOCK), const(RET_HEADS, TOK_BLOCK, RET_QK_DIM),
        const(RET_HEADS, TOK_BLOCK, RET_QK_DIM),
    ]
    out_shape = [
        jax.ShapeDtypeStruct((t + TOK_BLOCK, D_MODEL), F32),
        jax.ShapeDtypeStruct((TOP_K, t), jnp.int32),
        jax.ShapeDtypeStruct((TOP_K, t), F32),
        jax.ShapeDtypeStruct((groups, N_EXPERTS, 128), jnp.int32),
        jax.ShapeDtypeStruct((groups * GROUP_ROWS, D_MODEL), F32),
    ]
    out_specs = [
        pl.BlockSpec((TOK_BLOCK, D_MODEL), lambda i, yp: (i, 0)), lanes(TOP_K), lanes(TOP_K),
        pl.BlockSpec((1, N_EXPERTS, 128), lambda i, yp: (prev(i), 0, 0)),
        pl.BlockSpec((GROUP_ROWS, D_MODEL), lambda i, yp: (prev(i), 0)),
    ]
    operands += [lw["w_in"], lw["p_ret"], lw["p_sg"], lw["w_o"], lw["sg_w"], lw["sg_bt"],
                 lw["sg_ln_g"], lw["sg_ln_b"], lw["ln1_g"], lw["ln1_b"], lw["router_wt"], lw["router_bb"],
                 tables["cq"], tables["sq"], tables["ck"], tables["sk"],
                 tables["dmat"], tables["qdec"], tables["kdec"]]
    kern = functools.partial(_mixer_kernel, chunk_decay=tables["chunk_decay"], steps_per_seq=ns,
                             n_groups=groups, fused=fused)
    grid_spec = pltpu.PrefetchScalarGridSpec(
        num_scalar_prefetch=1, grid=(groups + 1,), in_specs=in_specs, out_specs=out_specs,
        scratch_shapes=scratch)
    return pl.pallas_call(
        kern,
        grid_spec=grid_spec,
        out_shape=out_shape,
        compiler_params=pltpu.CompilerParams(dimension_semantics=("arbitrary",),
                                             vmem_limit_bytes=VMEM_LIMIT_V7X),
        name="mixer",
    )(ypiece, *operands)


def _expert_kernel(bexp_ref, src_ref, nvb_ref, xs_hbm, w1_ref, b1_ref, w2_ref, b2_ref,
                   yb_ref, xbuf0, xbuf1, w1b, w2b, sem):
    b = pl.program_id(0)
    nvb = nvb_ref[0]
    xbufs = (xbuf0, xbuf1)

    def gather(blk, slot, pieces=range(EXP_PIECES)):
        for i in pieces:
            row = pl.multiple_of(src_ref[blk * EXP_PIECES + i] * PIECE, PIECE)
            pltpu.make_async_copy(xs_hbm.at[pl.ds(row, PIECE), :],
                                  xbufs[slot].at[pl.ds(i * PIECE, PIECE), :],
                                  sem.at[slot]).start()

    def wait(slot):
        pltpu.make_async_copy(xs_hbm.at[pl.ds(0, EXP_BLOCK), :], xbufs[slot], sem.at[slot]).wait()

    @pl.when(b == 0)
    def _():
        gather(0, 0)

    new_expert = (b == 0) | (bexp_ref[b] != bexp_ref[jnp.maximum(b - 1, 0)])

    @pl.when(new_expert & (b < nvb))
    def _():
        def cast_rows(i, carry):
            r = pl.multiple_of(i * W_CAST_ROWS, W_CAST_ROWS)
            w1b[pl.ds(r, W_CAST_ROWS), :] = w1_ref[0, 0, pl.ds(r, W_CAST_ROWS), :].astype(BF16)
            w2b[pl.ds(r, W_CAST_ROWS), :] = w2_ref[0, 0, pl.ds(r, W_CAST_ROWS), :].astype(BF16)
            return carry
        lax.fori_loop(0, D_MODEL // W_CAST_ROWS, cast_rows, 0)

    def block(cur):
        nxt = 1 - cur
        gather(jnp.minimum(b + 1, nvb - 1), nxt)
        wait(cur)
        x = xbufs[cur][...].astype(BF16)
        y = jnp.zeros((EXP_BLOCK, D_MODEL), F32)
        for j in range(D_EXPERT // F_CHUNK):
            c0 = j * F_CHUNK
            hg = _dot(x, w1b[:, c0:c0 + F_CHUNK]) + b1_ref[0, 0, :, c0:c0 + F_CHUNK]
            hu = (_dot(x, w1b[:, D_EXPERT + c0:D_EXPERT + c0 + F_CHUNK])
                  + b1_ref[0, 0, :, D_EXPERT + c0:D_EXPERT + c0 + F_CHUNK])
            gate = jnp.minimum(hg, SWIGLU_LIMIT)
            up = jnp.clip(hu, -SWIGLU_LIMIT, SWIGLU_LIMIT)
            act = gate * jax.nn.sigmoid(SWIGLU_ALPHA * gate) * (up + 1.0)
            y = y + _dot(act.astype(BF16), w2b[c0:c0 + F_CHUNK, :])
        yb_ref[...] = y + b2_ref[0, 0]

        @pl.when(b == nvb - 1)
        def _():
            wait(nxt)

    for cur in range(2):
        pl.when((b < nvb) & (b % 2 == cur))(functools.partial(block, cur))

    @pl.when(b >= nvb)
    def _():
        yb_ref[...] = jnp.zeros_like(yb_ref)


def _expert_call(xs, block_exp, src_piece, nvb, layer, e_w1, e_b1, e_w2, e_b2, n_blocks):
    wspec = lambda *shape: pl.BlockSpec((1, 1) + shape, lambda b, be, sp, nv: (layer, be[b], 0, 0))
    grid_spec = pltpu.PrefetchScalarGridSpec(
        num_scalar_prefetch=3,
        grid=(n_blocks,),
        in_specs=[
            pl.BlockSpec(memory_space=pl.ANY),
            wspec(D_MODEL, 2 * D_EXPERT), wspec(1, 2 * D_EXPERT),
            wspec(D_EXPERT, D_MODEL), wspec(1, D_MODEL),
        ],
        out_specs=pl.BlockSpec((EXP_BLOCK, D_MODEL), lambda b, be, sp, nv: (b, 0)),
        scratch_shapes=[pltpu.VMEM((EXP_BLOCK, D_MODEL), F32),
                        pltpu.VMEM((EXP_BLOCK, D_MODEL), F32),
                        pltpu.VMEM((D_MODEL, 2 * D_EXPERT), BF16),
                        pltpu.VMEM((D_EXPERT, D_MODEL), BF16),
                        pltpu.SemaphoreType.DMA((2,))],
    )
    return pl.pallas_call(
        _expert_kernel,
        grid_spec=grid_spec,
        out_shape=jax.ShapeDtypeStruct((n_blocks * EXP_BLOCK, D_MODEL), F32),
        compiler_params=pltpu.CompilerParams(dimension_semantics=("arbitrary",),
                                             vmem_limit_bytes=VMEM_LIMIT_V7X),
        name="experts",
    )(block_exp, src_piece, nvb, xs, e_w1, e_b1, e_w2, e_b2)


def _combine_kernel(yp_ref, x1_ref, yb_hbm, post_ref, gatet_ref, g_ref, b_ref, out_ref,
                    ybuf0, ybuf1, sem):
    g = pl.program_id(0)
    ng = pl.num_programs(0)
    ybufs = (ybuf0, ybuf1)

    def gather(grp, slot, pieces=range(GROUP_PIECES)):
        for i in pieces:
            row = pl.multiple_of(yp_ref[grp * GROUP_PIECES + i] * PIECE, PIECE)
            pltpu.make_async_copy(yb_hbm.at[pl.ds(row, PIECE), :],
                                  ybufs[slot].at[pl.ds(i * PIECE, PIECE), :],
                                  sem.at[slot]).start()

    def wait(slot):
        pltpu.make_async_copy(yb_hbm.at[pl.ds(0, GROUP_ROWS), :], ybufs[slot], sem.at[slot]).wait()

    @pl.when(g == 0)
    def _():
        gather(0, 0)

    def group(cur):
        nxt = 1 - cur
        gather(jnp.minimum(g + 1, ng - 1), nxt)
        slabs = _unsort_slabs(post_ref[...], gatet_ref[...])
        wait(cur)
        ffn = _unsort_matmul(slabs, ybufs[cur])
        out_ref[...] = _layer_norm(DN_ALPHA * x1_ref[...] + ffn, g_ref[...], b_ref[...])

        @pl.when(g == ng - 1)
        def _():
            wait(nxt)

    for cur in range(2):
        pl.when(g % 2 == cur)(functools.partial(group, cur))


def _combine_call(x1, yb, ypiece, post, gatet, ln_g, ln_b):
    t = post.shape[0]
    groups = t // TOK_BLOCK
    grid_spec = pltpu.PrefetchScalarGridSpec(
        num_scalar_prefetch=1,
        grid=(groups,),
        in_specs=[
            pl.BlockSpec((TOK_BLOCK, D_MODEL), lambda g, yp: (g, 0)),
            pl.BlockSpec(memory_space=pl.ANY),
            pl.BlockSpec((TOK_BLOCK, TOP_K), lambda g, yp: (g, 0)),
            pl.BlockSpec((TOK_BLOCK, TOP_K), lambda g, yp: (g, 0)),
            pl.BlockSpec((1, D_MODEL), lambda g, yp: (0, 0)),
            pl.BlockSpec((1, D_MODEL), lambda g, yp: (0, 0)),
        ],
        out_specs=pl.BlockSpec((TOK_BLOCK, D_MODEL), lambda g, yp: (g, 0)),
        scratch_shapes=[pltpu.VMEM((GROUP_ROWS, D_MODEL), F32), pltpu.VMEM((GROUP_ROWS, D_MODEL), F32),
                        pltpu.SemaphoreType.DMA((2,))],
    )
    return pl.pallas_call(
        _combine_kernel,
        grid_spec=grid_spec,
        out_shape=jax.ShapeDtypeStruct((t, D_MODEL), F32),
        compiler_params=pltpu.CompilerParams(dimension_semantics=("arbitrary",),
                                             vmem_limit_bytes=VMEM_LIMIT_V7X),
        name="combine",
    )(ypiece, x1, yb, post, gatet, ln_g, ln_b)


def _routing_tables(cnt_pad, n_blocks):
    groups = cnt_pad.shape[0]
    npc = cnt_pad // PIECE
    strip_start = jnp.cumsum(npc, axis=1) - npc
    tot = jnp.sum(npc, axis=0)
    tot_pad = (tot + EXP_PIECES - 1) // EXP_PIECES * EXP_PIECES
    exp_end = jnp.cumsum(tot_pad)
    exp_start = exp_end - tot_pad
    dstart = exp_start[None, :] + jnp.cumsum(npc, axis=0) - npc
    nvb = (exp_end[-1] // EXP_PIECES).astype(jnp.int32)

    blocks = jnp.arange(n_blocks, dtype=jnp.int32)
    bexp = jnp.sum(blocks[:, None] * EXP_PIECES >= exp_end[None, :], axis=1).astype(jnp.int32)
    bexp = jnp.minimum(bexp, N_EXPERTS - 1)
    last = jnp.sum((nvb - 1) * EXP_PIECES >= exp_end).astype(jnp.int32)
    bexp = jnp.where(blocks < nvb, bexp, jnp.minimum(last, N_EXPERTS - 1))

    d = jnp.arange(n_blocks * EXP_PIECES, dtype=jnp.int32)
    e_of = jnp.minimum(jnp.sum(d[:, None] >= exp_end[None, :], axis=1), N_EXPERTS - 1)
    oh_e = (e_of[:, None] == jnp.arange(N_EXPERTS, dtype=jnp.int32)[None, :]).astype(F32)
    row_of = lambda tab: jnp.dot(oh_e, tab.T.astype(F32), precision=lax.Precision.HIGHEST).astype(jnp.int32)
    ds_e = row_of(dstart)
    g_of = jnp.maximum(jnp.sum(ds_e <= d[:, None], axis=1) - 1, 0)
    at_g = g_of[:, None] == jnp.arange(groups, dtype=jnp.int32)[None, :]
    pick = lambda rows: jnp.sum(jnp.where(at_g, rows, 0), axis=1)
    i_of = d - pick(ds_e)
    valid = (i_of >= 0) & (i_of < pick(row_of(npc)))
    src = g_of * GROUP_PIECES + pick(row_of(strip_start)) + i_of
    src_piece = jnp.where(valid, src, 0).astype(jnp.int32)

    j = jnp.arange(GROUP_PIECES, dtype=jnp.int32)
    e_loc = jnp.maximum(jnp.sum(strip_start[:, None, :] <= j[None, :, None], axis=2) - 1, 0)
    at_e = e_loc[:, :, None] == jnp.arange(N_EXPERTS, dtype=jnp.int32)[None, None, :]
    pick_e = lambda tab: jnp.sum(jnp.where(at_e, tab[:, None, :], 0), axis=2)
    st_loc = pick_e(strip_start)
    np_loc = pick_e(npc)
    ds_loc = pick_e(dstart)
    i_loc = j[None, :] - st_loc
    ypiece = jnp.where(i_loc < np_loc, ds_loc + i_loc, 0).astype(jnp.int32).reshape(groups * GROUP_PIECES)
    return bexp, src_piece, nvb.reshape(1), ypiece


def _tables(seq):
    half = RET_QK_DIM // 2
    inv = ROPE_BASE ** (-jnp.arange(half, dtype=F32) / half)
    ang = jnp.arange(seq, dtype=jnp.int32).astype(F32)[:, None] * inv[None, :]
    cos, sin = jnp.cos(ang), jnp.sin(ang)
    cosf = jnp.concatenate([cos, cos], axis=1)
    sinf = jnp.concatenate([-sin, sin], axis=1)
    kscale = RET_QK_DIM ** -0.5
    log_g = jnp.log(1.0 - jnp.exp(jnp.linspace(math.log(1.0 / 32), math.log(1.0 / 512), RET_HEADS)))
    idx = jnp.arange(TOK_BLOCK, dtype=F32)
    diff = idx[:, None] - idx[None, :]
    dmat = jnp.where(diff >= 0, jnp.exp(log_g[:, None, None] * jnp.maximum(diff, 0.0)), 0.0)
    qdec = jnp.exp(log_g[:, None] * (idx[None, :] + 1.0))
    kdec = jnp.exp(log_g[:, None] * (TOK_BLOCK - 1.0 - idx[None, :]))
    bc = lambda a: jnp.broadcast_to(a[:, :, None], (RET_HEADS, TOK_BLOCK, RET_QK_DIM))
    log_g_host = np.log(1.0 - np.exp(np.linspace(math.log(1.0 / 32), math.log(1.0 / 512), RET_HEADS)))
    chunk_decay = tuple(float(np.float32(np.exp(np.float32(lg) * np.float32(TOK_BLOCK)))) for lg in log_g_host)
    return {"cq": cosf, "sq": sinf, "ck": cosf * kscale, "sk": sinf * kscale,
            "dmat": dmat.astype(F32), "qdec": bc(qdec), "kdec": bc(kdec), "chunk_decay": chunk_decay}


def kernel(x, w_in, p_ret, sg_ln_g, sg_ln_b, sg_w, sg_b, p_sg, w_o, ln1_g, ln1_b,
           router_w, router_b, e_w1, e_b1, e_w2, e_b2, ln2_g, ln2_b):
    batch, seq, d = x.shape
    assert d == D_MODEL and seq % TOK_BLOCK == 0
    depth = w_in.shape[0]
    t = batch * seq
    groups = t // TOK_BLOCK
    used_pieces = (TOK_BLOCK * TOP_K + N_EXPERTS * (PIECE - 1)) // PIECE
    n_blocks = (groups * used_pieces + N_EXPERTS * (EXP_PIECES - 1)) // EXP_PIECES
    tables = _tables(seq)
    xc = x.reshape(t, D_MODEL)
    w_in_b, p_ret_b, p_sg_b, w_o_b = (w.astype(BF16) for w in (w_in, p_ret, p_sg, w_o))
    e_b1r = e_b1.reshape(depth, N_EXPERTS, 1, 2 * D_EXPERT)
    e_b2r = e_b2.reshape(depth, N_EXPERTS, 1, D_MODEL)
    prev_moe = None
    for l in range(depth):
        lw = {
            "w_in": w_in_b, "p_ret": p_ret_b, "p_sg": p_sg_b,
            "w_o": w_o_b, "sg_w": sg_w[l], "sg_bt": sg_b[l].T,
            "sg_ln_g": sg_ln_g[l][None], "sg_ln_b": sg_ln_b[l][None],
            "ln1_g": ln1_g[l][None], "ln1_b": ln1_b[l][None],
            "router_wt": router_w[l].T,
            "router_bb": jnp.broadcast_to(router_b[l][:, None], (N_EXPERTS, TOK_BLOCK)),
        }
        x1, pos, gates, cnt, xs = _mixer_call(xc, batch, seq, l, lw, tables, prev_moe)
        bexp, src_piece, nvb, ypiece = _routing_tables(cnt[:, :, 0], n_blocks)
        yb = _expert_call(xs, bexp, src_piece, nvb, l, e_w1, e_b1r, e_w2, e_b2r, n_blocks)
        prev_moe = (yb, ypiece, pos.T, gates.T, ln2_g[l][None], ln2_b[l][None])
        xc = x1
    out = _combine_call(xc, *prev_moe)
    return out.reshape(batch, seq, D_MODEL)
```

```python
import functools
import math

import jax
import jax.numpy as jnp
import numpy as np
from jax import lax
from jax.experimental import pallas as pl
from jax.experimental.pallas import tpu as pltpu

F32 = jnp.float32
BF16 = jnp.bfloat16

D_MODEL = 1024
DEPTH = 4
RET_HEADS = 4
RET_QK_DIM = 128
RET_V_DIM = 256
RET_QK = RET_HEADS * RET_QK_DIM
RET_V = RET_HEADS * RET_V_DIM
ROPE_BASE = 10000.0
SG_BLOCK = 128
SG_CHUNK = 64
SG_GROUPS = 8
SG_WIDTH = D_MODEL
N_EXPERTS = 32
TOP_K = 4
D_EXPERT = D_MODEL
SWIGLU_LIMIT = 7.0
SWIGLU_ALPHA = 1.702
LN_EPS = 1e-5
DN_ALPHA = (2 * DEPTH) ** 0.25

OFF_Q = 0
OFF_K = OFF_Q + RET_QK
OFF_V = OFF_K + RET_QK
OFF_G = OFF_V + RET_V
OFF_U = OFF_G + RET_V
OFF_VS = OFF_U + SG_WIDTH
OFF_GA = OFF_VS + SG_WIDTH
OFF_GB = OFF_GA + D_MODEL
IN_WIDTH = OFF_GB + D_MODEL

SUBLANES_V7X = 8
TOK_BLOCK = 256
PIECE = SUBLANES_V7X
GROUP_ROWS = TOK_BLOCK * TOP_K + N_EXPERTS * PIECE
GROUP_PIECES = GROUP_ROWS // PIECE
SORT_ROWS = 256
ROUTE_COLS = SUBLANES_V7X
EXP_BLOCK = 512
EXP_PIECES = EXP_BLOCK // PIECE
F_CHUNK = 512
W_CAST_ROWS = 128
VMEM_LIMIT_V7X = 60 * 1024 * 1024


def _layer_norm(x, g, b):
    mu = jnp.mean(x, axis=-1, keepdims=True)
    xc = x - mu
    var = jnp.mean(xc * xc, axis=-1, keepdims=True)
    return xc * lax.rsqrt(var + LN_EPS) * g + b


def _gelu_tanh(x):
    c = math.sqrt(2.0 / math.pi)
    return 0.5 * x * (1.0 + jnp.tanh(c * (x + 0.044715 * (x * x * x))))


def _dot(a, b):
    return jnp.dot(a, b, preferred_element_type=F32)


def _dot_nt(a, b):
    return lax.dot_general(a, b, (((1,), (1,)), ((), ())), preferred_element_type=F32)


def _dot_tn(a, b):
    return lax.dot_general(a, b, (((0,), (0,)), ((), ())), preferred_element_type=F32)


def _split_bf16(x):
    hi = x.astype(BF16)
    lo = (x - hi.astype(F32)).astype(BF16)
    return hi, lo


def _unsort_slab(post, gatet, r0):
    c_iota = lax.broadcasted_iota(jnp.int32, (TOK_BLOCK, SORT_ROWS), 1) + r0
    unsort = jnp.zeros((TOK_BLOCK, SORT_ROWS), F32)
    for k in range(TOP_K):
        unsort = jnp.where(c_iota == post[:, k:k + 1], gatet[:, k:k + 1], unsort)
    return unsort.astype(BF16)


def _unsort_slabs(post, gatet):
    return [_unsort_slab(post, gatet, r0) for r0 in range(0, GROUP_ROWS, SORT_ROWS)]


def _unsort_matmul(slabs, ybuf):
    ffn = jnp.zeros((TOK_BLOCK, D_MODEL), F32)
    for i, r0 in enumerate(range(0, GROUP_ROWS, SORT_ROWS)):
        ffn = ffn + _dot(slabs[i], ybuf[r0:r0 + SORT_ROWS, :].astype(BF16))
    return ffn


def _mixer_kernel(yp_ref, *refs, chunk_decay, steps_per_seq, n_groups, fused):
    if fused:
        x_ref, yb_hbm, post_ref, gatet_ref, ln2g_ref, ln2b_ref = refs[:6]
        refs = refs[6:]
        ybuf, sem = refs[-2:]
        refs = refs[:-2]
    else:
        x_ref = refs[0]
        refs = refs[1:]
    (w_in_ref, p_ret_ref, p_sg_ref, w_o_ref, sgw_ref, sgbt_ref,
     sglng_ref, sglnb_ref, ln1g_ref, ln1b_ref, rwt_ref, rbb_ref,
     cq_ref, sq_ref, ck_ref, sk_ref, dmat_ref, qdec_ref, kdec_ref,
     x1_ref, pos_ref, gate_ref, cnt_ref, xs_ref,
     state_ref, ysg_ref, x1_prev_ref) = refs
    ts = TOK_BLOCK
    step = pl.program_id(0)

    @pl.when(step == 0)
    def _():
        x1_prev_ref[...] = jnp.zeros_like(x1_prev_ref)

    @pl.when(step % steps_per_seq == 0)
    def _():
        state_ref[...] = jnp.zeros_like(state_ref)

    if fused:
        slot = step % 2

        def gather(grp, s):
            for i in range(GROUP_PIECES):
                row = pl.multiple_of(yp_ref[grp * GROUP_PIECES + i] * PIECE, PIECE)
                pltpu.make_async_copy(yb_hbm.at[pl.ds(row, PIECE), :],
                                      ybuf.at[s, pl.ds(i * PIECE, PIECE), :], sem.at[s]).start()

        def wait(s):
            pltpu.make_async_copy(yb_hbm.at[pl.ds(0, GROUP_ROWS), :], ybuf.at[s], sem.at[s]).wait()

        @pl.when(step == 0)
        def _():
            gather(0, 0)

    routing = _route_and_sort(x1_prev_ref[...], rwt_ref, rbb_ref, pos_ref, gate_ref, cnt_ref, xs_ref)
    routing_stage = lambda: next(routing, None)

    if fused:
        slabs = _unsort_slabs(post_ref[...], gatet_ref[...])
        wait(slot)
        ffn = _unsort_matmul(slabs, ybuf.at[slot])
        xb = _layer_norm(DN_ALPHA * x_ref[...] + ffn, ln2g_ref[...], ln2b_ref[...])
        gather(jnp.minimum(step + 1, n_groups - 1), 1 - slot)
    else:
        xb = x_ref[...]
    xbf = xb.astype(BF16)

    def proj(off, width):
        return _dot(xbf, w_in_ref[0, :, off:off + width])

    cq, sq, ck, sk = cq_ref[...], sq_ref[...], ck_ref[...], sk_ref[...]
    acc_ret = jnp.zeros((ts, D_MODEL), F32)
    pair, head_in = {}, {}

    def project_head(h):
        if h % 2 == 0:
            pair["q"] = proj(OFF_Q + h * RET_QK_DIM, 2 * RET_QK_DIM)
            yield
            pair["k"] = proj(OFF_K + h * RET_QK_DIM, 2 * RET_QK_DIM)
            yield
        lanes = slice((h % 2) * RET_QK_DIM, (h % 2 + 1) * RET_QK_DIM)
        v = proj(OFF_V + h * RET_V_DIM, RET_V_DIM).astype(BF16)
        yield
        g = proj(OFF_G + h * RET_V_DIM, RET_V_DIM)
        head_in[h] = (pair["q"][:, lanes], pair["k"][:, lanes], v, g)
        yield

    for _ in project_head(0):
        pass
    for h in range(RET_HEADS):
        upcoming = project_head(h + 1) if h + 1 < RET_HEADS else iter(())
        q, k, v, g = head_in.pop(h)
        routing_stage()
        next(upcoming, None)
        qr = q * cq + pltpu.roll(q, RET_QK_DIM // 2, 1) * sq
        kr = k * ck + pltpu.roll(k, RET_QK_DIM // 2, 1) * sk
        scores = _dot_nt(qr.astype(BF16), kr.astype(BF16)) * dmat_ref[h]
        next(upcoming, None)
        inner = _dot(scores.astype(BF16), v)
        st = state_ref[h]
        cross = _dot((qr * qdec_ref[h]).astype(BF16), st.astype(BF16))
        kd = (kr * kdec_ref[h]).astype(BF16)
        state_ref[h] = st * chunk_decay[h] + _dot_tn(kd, v)
        routing_stage()
        next(upcoming, None)
        y = inner + cross
        mu = jnp.mean(y, axis=-1, keepdims=True)
        yc = y - mu
        var = jnp.mean(yc * yc, axis=-1, keepdims=True)
        yn = yc * lax.rsqrt(var + LN_EPS)
        next(upcoming, None)
        yr = yn * (g * jax.nn.sigmoid(g))
        acc_ret = acc_ret + _dot(yr.astype(BF16), p_ret_ref[0, h * RET_V_DIM:(h + 1) * RET_V_DIM, :])
        for _ in upcoming:
            pass
        if fused and h == 1:
            @pl.when(step == n_groups)
            def _():
                wait(1 - slot)

    pre_u = proj(OFF_U, SG_WIDTH)
    pre_vs = proj(OFF_VS, SG_WIDTH)
    ga = proj(OFF_GA, D_MODEL)
    u = _gelu_tanh(pre_u)
    routing_stage()
    gb = proj(OFF_GB, D_MODEL)
    vn = _layer_norm(_gelu_tanh(pre_vs), sglng_ref[...], sglnb_ref[...])
    qi = lax.broadcasted_iota(jnp.int32, (SG_BLOCK, SG_BLOCK), 0) // SG_CHUNK
    pi = lax.broadcasted_iota(jnp.int32, (SG_BLOCK, SG_BLOCK), 1) // SG_CHUNK
    sg_mask = qi >= pi
    for gi in range(SG_GROUPS):
        wm = jnp.where(sg_mask, sgw_ref[gi], 0.0).astype(BF16)
        bias = sgbt_ref[:, gi:gi + 1]
        cols = slice(gi * SG_BLOCK, (gi + 1) * SG_BLOCK)
        blocks = [slice(blk * SG_BLOCK, (blk + 1) * SG_BLOCK) for blk in range(ts // SG_BLOCK)]
        vcat = jnp.concatenate([vn[rows, cols] for rows in blocks], axis=1).astype(BF16)
        mixed = _dot(wm, vcat) + bias
        for blk, rows in enumerate(blocks):
            ysg_ref[rows, cols] = (u[rows, cols] * mixed[:, blk * SG_BLOCK:(blk + 1) * SG_BLOCK]).astype(BF16)
    routing_stage()
    acc_sg = _dot(ysg_ref[...], p_sg_ref[0])

    hmix = jax.nn.sigmoid(ga) * acc_ret + jax.nn.sigmoid(gb) * acc_sg
    mix = _dot(hmix.astype(BF16), w_o_ref[0])
    for _ in routing:
        pass
    x1 = _layer_norm(DN_ALPHA * xb + mix, ln1g_ref[...], ln1b_ref[...])
    x1_ref[...] = x1
    x1_prev_ref[...] = x1


def _route_and_sort(x1, rwt_ref, rbb_ref, pos_ref, gate_ref, cnt_ref, xs_ref):
    ts = TOK_BLOCK
    x_hi, x_lo = _split_bf16(x1)
    w_hi, w_lo = _split_bf16(rwt_ref[...])
    logits = (_dot_nt(w_hi, x_hi) + _dot_nt(w_hi, x_lo) + _dot_nt(w_lo, x_hi)) + rbb_ref[...]
    yield
    e_iota = lax.broadcasted_iota(jnp.int32, (N_EXPERTS, ts), 0).astype(F32)
    sels, vals = [], []
    work = logits
    for _ in range(TOP_K):
        m = jnp.max(work, axis=0, keepdims=True)
        idx = jnp.min(jnp.where(work == m, e_iota, float(N_EXPERTS)), axis=0, keepdims=True)
        sel = e_iota == idx
        work = jnp.where(sel, -jnp.inf, work)
        sels.append(sel)
        vals.append(m)
        yield
    exps = [jnp.exp(v - vals[0]) for v in vals]
    denom = exps[0] + exps[1] + exps[2] + exps[3]
    pad_rows = jnp.zeros((ROUTE_COLS - TOP_K, ts), F32)
    gate_ref[...] = jnp.concatenate([e / denom for e in exps] + [pad_rows], axis=0).T

    onehots = [jnp.where(s, 1.0, 0.0) for s in sels]
    oh_sum = onehots[0] + onehots[1] + onehots[2] + onehots[3]
    t_row = lax.broadcasted_iota(jnp.int32, (ts, ts), 0)
    t_col = lax.broadcasted_iota(jnp.int32, (ts, ts), 1)
    upper = jnp.where(t_row < t_col, 1.0, 0.0).astype(BF16)
    before = _dot(oh_sum.astype(BF16), upper)
    cnt = jnp.sum(oh_sum, axis=1, keepdims=True)
    cnt_pad = jnp.floor((cnt + (PIECE - 1)) * (1.0 / PIECE)) * PIECE
    cnt_b = jnp.broadcast_to(cnt_pad, (N_EXPERTS, ts))
    e_row = lax.broadcasted_iota(jnp.int32, (N_EXPERTS, N_EXPERTS), 0)
    e_col = lax.broadcasted_iota(jnp.int32, (N_EXPERTS, N_EXPERTS), 1)
    lower = jnp.where(e_col < e_row, 1.0, 0.0).astype(BF16)
    strip_off = _dot(lower, cnt_b.astype(BF16))
    slot = strip_off + before
    pos_f = [jnp.sum(oh * slot, axis=0, keepdims=True) for oh in onehots]
    pos_ref[...] = jnp.concatenate(pos_f + [pad_rows], axis=0).T.astype(jnp.int32)
    pos = [p.astype(jnp.int32) for p in pos_f]
    cnt_ref[0] = cnt_b[:, :128].astype(jnp.int32)
    yield
    for r0 in range(0, GROUP_ROWS, SORT_ROWS):
        r_iota = lax.broadcasted_iota(jnp.int32, (SORT_ROWS, ts), 0) + r0
        perm = jnp.zeros((SORT_ROWS, ts), F32)
        for k in range(TOP_K):
            perm = jnp.where(r_iota == pos[k], 1.0, perm)
        perm = perm.astype(BF16)
        xs_ref[r0:r0 + SORT_ROWS, :] = _dot(perm, x_hi)
        yield


def _mixer_call(x2d, batch, seq, layer, lw, tables, prev_moe=None):
    t = batch * seq
    ns = seq // TOK_BLOCK
    groups = t // TOK_BLOCK
    fused = prev_moe is not None
    const = lambda *shape: pl.BlockSpec(shape, lambda i, yp: (0,) * len(shape),
                                        pipeline_mode=pl.Buffered(1))
    stacked = lambda *shape: pl.BlockSpec((1,) + shape, lambda i, yp: (layer, 0, 0),
                                          pipeline_mode=pl.Buffered(1))
    cur = lambda i: jnp.minimum(i, groups - 1)
    prev = lambda i: jnp.maximum(i - 1, 0)
    seqtab = pl.BlockSpec((TOK_BLOCK, RET_QK_DIM), lambda i, yp: (cur(i) % ns, 0))
    route_prev = pl.BlockSpec((TOK_BLOCK, ROUTE_COLS), lambda i, yp: (prev(i), 0))
    tok_cur = lambda width: pl.BlockSpec((TOK_BLOCK, width), lambda i, yp: (cur(i), 0))
    in_specs = [tok_cur(D_MODEL)]
    operands = [x2d]
    scratch = [pltpu.VMEM((RET_HEADS, RET_QK_DIM, RET_V_DIM), F32),
               pltpu.VMEM((TOK_BLOCK, SG_WIDTH), BF16),
               pltpu.VMEM((TOK_BLOCK, D_MODEL), F32)]
    if fused:
        yb, ypiece, post, gatet, ln2_g, ln2_b = prev_moe
        in_specs += [pl.BlockSpec(memory_space=pl.ANY), tok_cur(ROUTE_COLS), tok_cur(ROUTE_COLS),
                     const(1, D_MODEL), const(1, D_MODEL)]
        operands += [yb, post, gatet, ln2_g, ln2_b]
        scratch += [pltpu.VMEM((2, GROUP_ROWS, D_MODEL), F32), pltpu.SemaphoreType.DMA((2,))]
    else:
        ypiece = jnp.zeros((1,), jnp.int32)
    in_specs += [
        stacked(D_MODEL, IN_WIDTH), stacked(RET_V, D_MODEL), stacked(SG_WIDTH, D_MODEL),
        stacked(D_MODEL, D_MODEL),
        const(SG_GROUPS, SG_BLOCK, SG_BLOCK), const(SG_BLOCK, SG_GROUPS),
        const(1, SG_WIDTH), const(1, SG_WIDTH), const(1, D_MODEL), const(1, D_MODEL),
        const(N_EXPERTS, D_MODEL), const(N_EXPERTS, TOK_BLOCK),
        seqtab, seqtab, seqtab, seqtab,
        const(RET_HEADS, TOK_BLOCK, TOK_BLOCK), const(RET_HEADS, TOK_BLOCK, RET_QK_DIM),
        const(RET_HEADS, TOK_BLOCK, RET_QK_DIM),
    ]
    out_shape = [
        jax.ShapeDtypeStruct((t + TOK_BLOCK, D_MODEL), F32),
        jax.ShapeDtypeStruct((t, ROUTE_COLS), jnp.int32),
        jax.ShapeDtypeStruct((t, ROUTE_COLS), F32),
        jax.ShapeDtypeStruct((groups, N_EXPERTS, 128), jnp.int32),
        jax.ShapeDtypeStruct((groups * GROUP_ROWS, D_MODEL), F32),
    ]
    out_specs = [
        pl.BlockSpec((TOK_BLOCK, D_MODEL), lambda i, yp: (i, 0)), route_prev, route_prev,
        pl.BlockSpec((1, N_EXPERTS, 128), lambda i, yp: (prev(i), 0, 0)),
        pl.BlockSpec((GROUP_ROWS, D_MODEL), lambda i, yp: (prev(i), 0)),
    ]
    operands += [lw["w_in"], lw["p_ret"], lw["p_sg"], lw["w_o"], lw["sg_w"], lw["sg_bt"],
                 lw["sg_ln_g"], lw["sg_ln_b"], lw["ln1_g"], lw["ln1_b"], lw["router_wt"], lw["router_bb"],
                 tables["cq"], tables["sq"], tables["ck"], tables["sk"],
                 tables["dmat"], tables["qdec"], tables["kdec"]]
    kern = functools.partial(_mixer_kernel, chunk_decay=tables["chunk_decay"], steps_per_seq=ns,
                             n_groups=groups, fused=fused)
    grid_spec = pltpu.PrefetchScalarGridSpec(
        num_scalar_prefetch=1, grid=(groups + 1,), in_specs=in_specs, out_specs=out_specs,
        scratch_shapes=scratch)
    return pl.pallas_call(
        kern,
        grid_spec=grid_spec,
        out_shape=out_shape,
        compiler_params=pltpu.CompilerParams(dimension_semantics=("arbitrary",),
                                             vmem_limit_bytes=VMEM_LIMIT_V7X),
        name="mixer",
    )(ypiece, *operands)


def _expert_kernel(bexp_ref, src_ref, nvb_ref, xs_hbm, w1_ref, b1_ref, w2_ref, b2_ref,
                   yb_ref, xbuf0, xbuf1, w1b, w2b, sem):
    b = pl.program_id(0)
    nvb = nvb_ref[0]
    xbufs = (xbuf0, xbuf1)

    def gather(blk, slot, pieces=range(EXP_PIECES)):
        for i in pieces:
            row = pl.multiple_of(src_ref[blk * EXP_PIECES + i] * PIECE, PIECE)
            pltpu.make_async_copy(xs_hbm.at[pl.ds(row, PIECE), :],
                                  xbufs[slot].at[pl.ds(i * PIECE, PIECE), :],
                                  sem.at[slot]).start()

    def wait(slot):
        pltpu.make_async_copy(xs_hbm.at[pl.ds(0, EXP_BLOCK), :], xbufs[slot], sem.at[slot]).wait()

    @pl.when(b == 0)
    def _():
        gather(0, 0)

    new_expert = (b == 0) | (bexp_ref[b] != bexp_ref[jnp.maximum(b - 1, 0)])

    @pl.when(new_expert & (b < nvb))
    def _():
        def cast_rows(i, carry):
            r = pl.multiple_of(i * W_CAST_ROWS, W_CAST_ROWS)
            w1b[pl.ds(r, W_CAST_ROWS), :] = w1_ref[0, 0, pl.ds(r, W_CAST_ROWS), :].astype(BF16)
            w2b[pl.ds(r, W_CAST_ROWS), :] = w2_ref[0, 0, pl.ds(r, W_CAST_ROWS), :].astype(BF16)
            return carry
        lax.fori_loop(0, D_MODEL // W_CAST_ROWS, cast_rows, 0)

    def block(cur):
        nxt = 1 - cur
        gather(jnp.minimum(b + 1, nvb - 1), nxt)
        wait(cur)
        x = xbufs[cur][...].astype(BF16)
        y = jnp.zeros((EXP_BLOCK, D_MODEL), F32)
        for j in range(D_EXPERT // F_CHUNK):
            c0 = j * F_CHUNK
            hg = _dot(x, w1b[:, c0:c0 + F_CHUNK]) + b1_ref[0, 0, :, c0:c0 + F_CHUNK]
            hu = (_dot(x, w1b[:, D_EXPERT + c0:D_EXPERT + c0 + F_CHUNK])
                  + b1_ref[0, 0, :, D_EXPERT + c0:D_EXPERT + c0 + F_CHUNK])
            gate = jnp.minimum(hg, SWIGLU_LIMIT)
            up = jnp.clip(hu, -SWIGLU_LIMIT, SWIGLU_LIMIT)
            act = gate * jax.nn.sigmoid(SWIGLU_ALPHA * gate) * (up + 1.0)
            y = y + _dot(act.astype(BF16), w2b[c0:c0 + F_CHUNK, :])
        yb_ref[...] = y + b2_ref[0, 0]

        @pl.when(b == nvb - 1)
        def _():
            wait(nxt)

    for cur in range(2):
        pl.when((b < nvb) & (b % 2 == cur))(functools.partial(block, cur))

    @pl.when(b >= nvb)
    def _():
        yb_ref[...] = jnp.zeros_like(yb_ref)


def _expert_call(xs, block_exp, src_piece, nvb, layer, e_w1, e_b1, e_w2, e_b2, n_blocks):
    wspec = lambda *shape: pl.BlockSpec((1, 1) + shape, lambda b, be, sp, nv: (layer, be[b], 0, 0))
    grid_spec = pltpu.PrefetchScalarGridSpec(
        num_scalar_prefetch=3,
        grid=(n_blocks,),
        in_specs=[
            pl.BlockSpec(memory_space=pl.ANY),
            wspec(D_MODEL, 2 * D_EXPERT), wspec(1, 2 * D_EXPERT),
            wspec(D_EXPERT, D_MODEL), wspec(1, D_MODEL),
        ],
        out_specs=pl.BlockSpec((EXP_BLOCK, D_MODEL), lambda b, be, sp, nv: (b, 0)),
        scratch_shapes=[pltpu.VMEM((EXP_BLOCK, D_MODEL), F32),
                        pltpu.VMEM((EXP_BLOCK, D_MODEL), F32),
                        pltpu.VMEM((D_MODEL, 2 * D_EXPERT), BF16),
                        pltpu.VMEM((D_EXPERT, D_MODEL), BF16),
                        pltpu.SemaphoreType.DMA((2,))],
    )
    return pl.pallas_call(
        _expert_kernel,
        grid_spec=grid_spec,
        out_shape=jax.ShapeDtypeStruct((n_blocks * EXP_BLOCK, D_MODEL), F32),
        compiler_params=pltpu.CompilerParams(dimension_semantics=("arbitrary",),
                                             vmem_limit_bytes=VMEM_LIMIT_V7X),
        name="experts",
    )(block_exp, src_piece, nvb, xs, e_w1, e_b1, e_w2, e_b2)


def _combine_kernel(yp_ref, x1_ref, yb_hbm, post_ref, gatet_ref, g_ref, b_ref, out_ref,
                    ybuf0, ybuf1, sem):
    g = pl.program_id(0)
    ng = pl.num_programs(0)
    ybufs = (ybuf0, ybuf1)

    def gather(grp, slot, pieces=range(GROUP_PIECES)):
        for i in pieces:
            row = pl.multiple_of(yp_ref[grp * GROUP_PIECES + i] * PIECE, PIECE)
            pltpu.make_async_copy(yb_hbm.at[pl.ds(row, PIECE), :],
                                  ybufs[slot].at[pl.ds(i * PIECE, PIECE), :],
                                  sem.at[slot]).start()

    def wait(slot):
        pltpu.make_async_copy(yb_hbm.at[pl.ds(0, GROUP_ROWS), :], ybufs[slot], sem.at[slot]).wait()

    @pl.when(g == 0)
    def _():
        gather(0, 0)

    def group(cur):
        nxt = 1 - cur
        gather(jnp.minimum(g + 1, ng - 1), nxt)
        slabs = _unsort_slabs(post_ref[...], gatet_ref[...])
        wait(cur)
        ffn = _unsort_matmul(slabs, ybufs[cur])
        out_ref[...] = _layer_norm(DN_ALPHA * x1_ref[...] + ffn, g_ref[...], b_ref[...])

        @pl.when(g == ng - 1)
        def _():
            wait(nxt)

    for cur in range(2):
        pl.when(g % 2 == cur)(functools.partial(group, cur))


def _combine_call(x1, yb, ypiece, post, gatet, ln_g, ln_b):
    t = post.shape[0]
    groups = t // TOK_BLOCK
    grid_spec = pltpu.PrefetchScalarGridSpec(
        num_scalar_prefetch=1,
        grid=(groups,),
        in_specs=[
            pl.BlockSpec((TOK_BLOCK, D_MODEL), lambda g, yp: (g, 0)),
            pl.BlockSpec(memory_space=pl.ANY),
            pl.BlockSpec((TOK_BLOCK, ROUTE_COLS), lambda g, yp: (g, 0)),
            pl.BlockSpec((TOK_BLOCK, ROUTE_COLS), lambda g, yp: (g, 0)),
            pl.BlockSpec((1, D_MODEL), lambda g, yp: (0, 0)),
            pl.BlockSpec((1, D_MODEL), lambda g, yp: (0, 0)),
        ],
        out_specs=pl.BlockSpec((TOK_BLOCK, D_MODEL), lambda g, yp: (g, 0)),
        scratch_shapes=[pltpu.VMEM((GROUP_ROWS, D_MODEL), F32), pltpu.VMEM((GROUP_ROWS, D_MODEL), F32),
                        pltpu.SemaphoreType.DMA((2,))],
    )
    return pl.pallas_call(
        _combine_kernel,
        grid_spec=grid_spec,
        out_shape=jax.ShapeDtypeStruct((t, D_MODEL), F32),
        compiler_params=pltpu.CompilerParams(dimension_semantics=("arbitrary",),
                                             vmem_limit_bytes=VMEM_LIMIT_V7X),
        name="combine",
    )(ypiece, x1, yb, post, gatet, ln_g, ln_b)


def _routing_tables(cnt_pad, n_blocks):
    groups = cnt_pad.shape[0]
    npc = cnt_pad // PIECE
    strip_start = jnp.cumsum(npc, axis=1) - npc
    tot = jnp.sum(npc, axis=0)
    tot_pad = (tot + EXP_PIECES - 1) // EXP_PIECES * EXP_PIECES
    exp_end = jnp.cumsum(tot_pad)
    exp_start = exp_end - tot_pad
    dstart = exp_start[None, :] + jnp.cumsum(npc, axis=0) - npc
    nvb = (exp_end[-1] // EXP_PIECES).astype(jnp.int32)

    blocks = jnp.arange(n_blocks, dtype=jnp.int32)
    bexp = jnp.sum(blocks[:, None] * EXP_PIECES >= exp_end[None, :], axis=1).astype(jnp.int32)
    bexp = jnp.minimum(bexp, N_EXPERTS - 1)
    last = jnp.sum((nvb - 1) * EXP_PIECES >= exp_end).astype(jnp.int32)
    bexp = jnp.where(blocks < nvb, bexp, jnp.minimum(last, N_EXPERTS - 1))

    d = jnp.arange(n_blocks * EXP_PIECES, dtype=jnp.int32)
    e_of = jnp.minimum(jnp.sum(d[:, None] >= exp_end[None, :], axis=1), N_EXPERTS - 1)
    oh_e = (e_of[:, None] == jnp.arange(N_EXPERTS, dtype=jnp.int32)[None, :]).astype(F32)
    row_of = lambda tab: jnp.dot(oh_e, tab.T.astype(F32), precision=lax.Precision.HIGHEST).astype(jnp.int32)
    ds_e = row_of(dstart)
    g_of = jnp.maximum(jnp.sum(ds_e <= d[:, None], axis=1) - 1, 0)
    at_g = g_of[:, None] == jnp.arange(groups, dtype=jnp.int32)[None, :]
    pick = lambda rows: jnp.sum(jnp.where(at_g, rows, 0), axis=1)
    i_of = d - pick(ds_e)
    valid = (i_of >= 0) & (i_of < pick(row_of(npc)))
    src = g_of * GROUP_PIECES + pick(row_of(strip_start)) + i_of
    src_piece = jnp.where(valid, src, 0).astype(jnp.int32)

    j = jnp.arange(GROUP_PIECES, dtype=jnp.int32)
    e_loc = jnp.maximum(jnp.sum(strip_start[:, None, :] <= j[None, :, None], axis=2) - 1, 0)
    at_e = e_loc[:, :, None] == jnp.arange(N_EXPERTS, dtype=jnp.int32)[None, None, :]
    pick_e = lambda tab: jnp.sum(jnp.where(at_e, tab[:, None, :], 0), axis=2)
    st_loc = pick_e(strip_start)
    np_loc = pick_e(npc)
    ds_loc = pick_e(dstart)
    i_loc = j[None, :] - st_loc
    ypiece = jnp.where(i_loc < np_loc, ds_loc + i_loc, 0).astype(jnp.int32).reshape(groups * GROUP_PIECES)
    return bexp, src_piece, nvb.reshape(1), ypiece


def _tables(seq):
    half = RET_QK_DIM // 2
    inv = ROPE_BASE ** (-jnp.arange(half, dtype=F32) / half)
    ang = jnp.arange(seq, dtype=jnp.int32).astype(F32)[:, None] * inv[None, :]
    cos, sin = jnp.cos(ang), jnp.sin(ang)
    cosf = jnp.concatenate([cos, cos], axis=1)
    sinf = jnp.concatenate([-sin, sin], axis=1)
    kscale = RET_QK_DIM ** -0.5
    log_g = jnp.log(1.0 - jnp.exp(jnp.linspace(math.log(1.0 / 32), math.log(1.0 / 512), RET_HEADS)))
    idx = jnp.arange(TOK_BLOCK, dtype=F32)
    diff = idx[:, None] - idx[None, :]
    dmat = jnp.where(diff >= 0, jnp.exp(log_g[:, None, None] * jnp.maximum(diff, 0.0)), 0.0)
    qdec = jnp.exp(log_g[:, None] * (idx[None, :] + 1.0))
    kdec = jnp.exp(log_g[:, None] * (TOK_BLOCK - 1.0 - idx[None, :]))
    bc = lambda a: jnp.broadcast_to(a[:, :, None], (RET_HEADS, TOK_BLOCK, RET_QK_DIM))
    log_g_host = np.log(1.0 - np.exp(np.linspace(math.log(1.0 / 32), math.log(1.0 / 512), RET_HEADS)))
    chunk_decay = tuple(float(np.float32(np.exp(np.float32(lg) * np.float32(TOK_BLOCK)))) for lg in log_g_host)
    return {"cq": cosf, "sq": sinf, "ck": cosf * kscale, "sk": sinf * kscale,
            "dmat": dmat.astype(F32), "qdec": bc(qdec), "kdec": bc(kdec), "chunk_decay": chunk_decay}


def kernel(x, w_in, p_ret, sg_ln_g, sg_ln_b, sg_w, sg_b, p_sg, w_o, ln1_g, ln1_b,
           router_w, router_b, e_w1, e_b1, e_w2, e_b2, ln2_g, ln2_b):
    batch, seq, d = x.shape
    assert d == D_MODEL and seq % TOK_BLOCK == 0
    depth = w_in.shape[0]
    t = batch * seq
    groups = t // TOK_BLOCK
    used_pieces = (TOK_BLOCK * TOP_K + N_EXPERTS * (PIECE - 1)) // PIECE
    n_blocks = (groups * used_pieces + N_EXPERTS * (EXP_PIECES - 1)) // EXP_PIECES
    n_blocks = -(-n_blocks // SUBLANES_V7X) * SUBLANES_V7X
    tables = _tables(seq)
    xc = x.reshape(t, D_MODEL)
    w_in_b, p_ret_b, p_sg_b, w_o_b = (w.astype(BF16) for w in (w_in, p_ret, p_sg, w_o))
    e_b1r = e_b1.reshape(depth, N_EXPERTS, 1, 2 * D_EXPERT)
    e_b2r = e_b2.reshape(depth, N_EXPERTS, 1, D_MODEL)
    prev_moe = None
    for l in range(depth):
        lw = {
            "w_in": w_in_b, "p_ret": p_ret_b, "p_sg": p_sg_b,
            "w_o": w_o_b, "sg_w": sg_w[l], "sg_bt": sg_b[l].T,
            "sg_ln_g": sg_ln_g[l][None], "sg_ln_b": sg_ln_b[l][None],
            "ln1_g": ln1_g[l][None], "ln1_b": ln1_b[l][None],
            "router_wt": router_w[l].T,
            "router_bb": jnp.broadcast_to(router_b[l][:, None], (N_EXPERTS, TOK_BLOCK)),
        }
        x1, pos, gates, cnt, xs = _mixer_call(xc, batch, seq, l, lw, tables, prev_moe)
        bexp, src_piece, nvb, ypiece = _routing_tables(cnt[:, :, 0], n_blocks)
        yb = _expert_call(xs, bexp, src_piece, nvb, l, e_w1, e_b1r, e_w2, e_b2r, n_blocks)
        prev_moe = (yb, ypiece, pos, gates, ln2_g[l][None], ln2_b[l][None])
        xc = x1
    out = _combine_call(xc, *prev_moe)
    return out.reshape(batch, seq, D_MODEL)
```

```python
import functools
import math

import jax
import jax.numpy as jnp
import numpy as np
from jax import lax
from jax.experimental import pallas as pl
from jax.experimental.pallas import tpu as pltpu

F32 = jnp.float32
BF16 = jnp.bfloat16

D_MODEL = 1024
DEPTH = 4
RET_HEADS = 4
RET_QK_DIM = 128
RET_V_DIM = 256
RET_QK = RET_HEADS * RET_QK_DIM
RET_V = RET_HEADS * RET_V_DIM
ROPE_BASE = 10000.0
SG_BLOCK = 128
SG_CHUNK = 64
SG_GROUPS = 8
SG_WIDTH = D_MODEL
N_EXPERTS = 32
TOP_K = 4
D_EXPERT = D_MODEL
SWIGLU_LIMIT = 7.0
SWIGLU_ALPHA = 1.702
LN_EPS = 1e-5
DN_ALPHA = (2 * DEPTH) ** 0.25

OFF_Q = 0
OFF_K = OFF_Q + RET_QK
OFF_V = OFF_K + RET_QK
OFF_G = OFF_V + RET_V
OFF_U = OFF_G + RET_V
OFF_VS = OFF_U + SG_WIDTH
OFF_GA = OFF_VS + SG_WIDTH
OFF_GB = OFF_GA + D_MODEL
IN_WIDTH = OFF_GB + D_MODEL

SUBLANES_V7X = 8
LANES_V7X = 128
TOK_BLOCK = 256
PIECE = SUBLANES_V7X
GROUP_ROWS = TOK_BLOCK * TOP_K + N_EXPERTS * PIECE
GROUP_PIECES = GROUP_ROWS // PIECE
SORT_ROWS = 256
ROUTE_COLS = SUBLANES_V7X
EXP_BLOCK = 512
EXP_PIECES = EXP_BLOCK // PIECE
F_CHUNK = 512
W_CAST_ROWS = 128
VMEM_LIMIT_V7X = 60 * 1024 * 1024


def _layer_norm(x, g, b):
    mu = jnp.mean(x, axis=-1, keepdims=True)
    xc = x - mu
    var = jnp.mean(xc * xc, axis=-1, keepdims=True)
    return xc * lax.rsqrt(var + LN_EPS) * g + b


def _gelu_tanh(x):
    c = math.sqrt(2.0 / math.pi)
    return 0.5 * x * (1.0 + jnp.tanh(c * (x + 0.044715 * (x * x * x))))


def _dot(a, b):
    return jnp.dot(a, b, preferred_element_type=F32)


def _dot_nt(a, b):
    return lax.dot_general(a, b, (((1,), (1,)), ((), ())), preferred_element_type=F32)


def _dot_tn(a, b):
    return lax.dot_general(a, b, (((0,), (0,)), ((), ())), preferred_element_type=F32)


def _split_bf16(x):
    hi = x.astype(BF16)
    lo = (x - hi.astype(F32)).astype(BF16)
    return hi, lo


def _unsort_slab(post, gatet, r0):
    c_iota = lax.broadcasted_iota(jnp.int32, (TOK_BLOCK, SORT_ROWS), 1) + r0
    unsort = jnp.zeros((TOK_BLOCK, SORT_ROWS), F32)
    for k in range(TOP_K):
        unsort = jnp.where(c_iota == post[:, k:k + 1], gatet[:, k:k + 1], unsort)
    return unsort.astype(BF16)


def _unsort_slabs(post, gatet):
    return [_unsort_slab(post, gatet, r0) for r0 in range(0, GROUP_ROWS, SORT_ROWS)]


def _unsort_matmul(slabs, ybuf):
    ffn = jnp.zeros((TOK_BLOCK, D_MODEL), F32)
    for i, r0 in enumerate(range(0, GROUP_ROWS, SORT_ROWS)):
        ffn = ffn + _dot(slabs[i], ybuf[r0:r0 + SORT_ROWS, :].astype(BF16))
    return ffn


def _mixer_kernel(yp_ref, *refs, chunk_decay, steps_per_seq, n_groups, fused):
    if fused:
        x_ref, yb_hbm, post_ref, gatet_ref, ln2g_ref, ln2b_ref = refs[:6]
        refs = refs[6:]
        ybuf, sem = refs[-2:]
        refs = refs[:-2]
    else:
        x_ref = refs[0]
        refs = refs[1:]
    (w_in_ref, p_ret_ref, p_sg_ref, w_o_ref, sgw_ref, sgbt_ref,
     sglng_ref, sglnb_ref, ln1g_ref, ln1b_ref, rwt_ref, rbb_ref,
     cq_ref, sq_ref, ck_ref, sk_ref, dmat_ref, qdec_ref, kdec_ref,
     x1_ref, pos_ref, gate_ref, cnt_ref, xs_ref,
     state_ref, ysg_ref, x1_prev_ref) = refs
    ts = TOK_BLOCK
    step = pl.program_id(0)

    @pl.when(step == 0)
    def _():
        x1_prev_ref[...] = jnp.zeros_like(x1_prev_ref)

    @pl.when(step % steps_per_seq == 0)
    def _():
        state_ref[...] = jnp.zeros_like(state_ref)

    if fused:
        slot = step % 2

        def gather(grp, s):
            for i in range(GROUP_PIECES):
                row = pl.multiple_of(yp_ref[grp * GROUP_PIECES + i] * PIECE, PIECE)
                pltpu.make_async_copy(yb_hbm.at[pl.ds(row, PIECE), :],
                                      ybuf.at[s, pl.ds(i * PIECE, PIECE), :], sem.at[s]).start()

        def wait(s):
            pltpu.make_async_copy(yb_hbm.at[pl.ds(0, GROUP_ROWS), :], ybuf.at[s], sem.at[s]).wait()

        @pl.when(step == 0)
        def _():
            gather(0, 0)

    routing = _route_and_sort(x1_prev_ref[...], rwt_ref, rbb_ref, pos_ref, gate_ref, cnt_ref, xs_ref)
    routing_stage = lambda: next(routing, None)

    if fused:
        slabs = _unsort_slabs(post_ref[...], gatet_ref[...])
        wait(slot)
        ffn = _unsort_matmul(slabs, ybuf.at[slot])
        xb = _layer_norm(DN_ALPHA * x_ref[...] + ffn, ln2g_ref[...], ln2b_ref[...])
        gather(jnp.minimum(step + 1, n_groups - 1), 1 - slot)
    else:
        xb = x_ref[...]
    xbf = xb.astype(BF16)

    def proj(off, width):
        return _dot(xbf, w_in_ref[0, :, off:off + width])

    cq, sq, ck, sk = cq_ref[...], sq_ref[...], ck_ref[...], sk_ref[...]
    acc_ret = jnp.zeros((ts, D_MODEL), F32)
    pair, head_in = {}, {}

    def project_head(h):
        if h % 2 == 0:
            pair["q"] = proj(OFF_Q + h * RET_QK_DIM, 2 * RET_QK_DIM)
            yield
            pair["k"] = proj(OFF_K + h * RET_QK_DIM, 2 * RET_QK_DIM)
            yield
        lanes = slice((h % 2) * RET_QK_DIM, (h % 2 + 1) * RET_QK_DIM)
        v = proj(OFF_V + h * RET_V_DIM, RET_V_DIM).astype(BF16)
        yield
        g = proj(OFF_G + h * RET_V_DIM, RET_V_DIM)
        head_in[h] = (pair["q"][:, lanes], pair["k"][:, lanes], v, g)
        yield

    for _ in project_head(0):
        pass
    for h in range(RET_HEADS):
        upcoming = project_head(h + 1) if h + 1 < RET_HEADS else iter(())
        q, k, v, g = head_in.pop(h)
        routing_stage()
        next(upcoming, None)
        qr = q * cq + pltpu.roll(q, RET_QK_DIM // 2, 1) * sq
        kr = k * ck + pltpu.roll(k, RET_QK_DIM // 2, 1) * sk
        scores = _dot_nt(qr.astype(BF16), kr.astype(BF16)) * dmat_ref[h]
        next(upcoming, None)
        inner = _dot(scores.astype(BF16), v)
        st = state_ref[h]
        cross = _dot((qr * qdec_ref[h]).astype(BF16), st.astype(BF16))
        kd = (kr * kdec_ref[h]).astype(BF16)
        state_ref[h] = st * chunk_decay[h] + _dot_tn(kd, v)
        routing_stage()
        next(upcoming, None)
        y = inner + cross
        mu = jnp.mean(y, axis=-1, keepdims=True)
        yc = y - mu
        var = jnp.mean(yc * yc, axis=-1, keepdims=True)
        yn = yc * lax.rsqrt(var + LN_EPS)
        next(upcoming, None)
        yr = yn * (g * jax.nn.sigmoid(g))
        acc_ret = acc_ret + _dot(yr.astype(BF16), p_ret_ref[0, h * RET_V_DIM:(h + 1) * RET_V_DIM, :])
        for _ in upcoming:
            pass
        if fused and h == 1:
            @pl.when(step == n_groups)
            def _():
                wait(1 - slot)

    pre_u = proj(OFF_U, SG_WIDTH)
    pre_vs = proj(OFF_VS, SG_WIDTH)
    ga = proj(OFF_GA, D_MODEL)
    u = _gelu_tanh(pre_u)
    routing_stage()
    gb = proj(OFF_GB, D_MODEL)
    vn = _layer_norm(_gelu_tanh(pre_vs), sglng_ref[...], sglnb_ref[...])
    qi = lax.broadcasted_iota(jnp.int32, (SG_BLOCK, SG_BLOCK), 0) // SG_CHUNK
    pi = lax.broadcasted_iota(jnp.int32, (SG_BLOCK, SG_BLOCK), 1) // SG_CHUNK
    sg_mask = qi >= pi
    for gi in range(SG_GROUPS):
        wm = jnp.where(sg_mask, sgw_ref[gi], 0.0).astype(BF16)
        bias = sgbt_ref[:, gi:gi + 1]
        cols = slice(gi * SG_BLOCK, (gi + 1) * SG_BLOCK)
        blocks = [slice(blk * SG_BLOCK, (blk + 1) * SG_BLOCK) for blk in range(ts // SG_BLOCK)]
        vcat = jnp.concatenate([vn[rows, cols] for rows in blocks], axis=1).astype(BF16)
        mixed = _dot(wm, vcat) + bias
        for blk, rows in enumerate(blocks):
            ysg_ref[rows, cols] = (u[rows, cols] * mixed[:, blk * SG_BLOCK:(blk + 1) * SG_BLOCK]).astype(BF16)
    routing_stage()
    acc_sg = _dot(ysg_ref[...], p_sg_ref[0])

    hmix = jax.nn.sigmoid(ga) * acc_ret + jax.nn.sigmoid(gb) * acc_sg
    mix = _dot(hmix.astype(BF16), w_o_ref[0])
    for _ in routing:
        pass
    x1 = _layer_norm(DN_ALPHA * xb + mix, ln1g_ref[...], ln1b_ref[...])
    x1_ref[...] = x1
    x1_prev_ref[...] = x1


def _route_and_sort(x1, rwt_ref, rbb_ref, pos_ref, gate_ref, cnt_ref, xs_ref):
    ts = TOK_BLOCK
    x_hi, x_lo = _split_bf16(x1)
    w_hi, w_lo = _split_bf16(rwt_ref[...])
    logits = (_dot_nt(w_hi, x_hi) + _dot_nt(w_hi, x_lo) + _dot_nt(w_lo, x_hi)) + rbb_ref[...]
    yield
    e_iota = lax.broadcasted_iota(jnp.int32, (N_EXPERTS, ts), 0).astype(F32)
    sels, vals = [], []
    work = logits
    for _ in range(TOP_K):
        m = jnp.max(work, axis=0, keepdims=True)
        idx = jnp.min(jnp.where(work == m, e_iota, float(N_EXPERTS)), axis=0, keepdims=True)
        sel = e_iota == idx
        work = jnp.where(sel, -jnp.inf, work)
        sels.append(sel)
        vals.append(m)
        yield
    exps = [jnp.exp(v - vals[0]) for v in vals]
    denom = exps[0] + exps[1] + exps[2] + exps[3]
    pad_rows = jnp.zeros((ROUTE_COLS - TOP_K, ts), F32)
    gate_ref[...] = jnp.concatenate([e / denom for e in exps] + [pad_rows], axis=0).T

    onehots = [jnp.where(s, 1.0, 0.0) for s in sels]
    oh_sum = onehots[0] + onehots[1] + onehots[2] + onehots[3]
    t_row = lax.broadcasted_iota(jnp.int32, (ts, ts), 0)
    t_col = lax.broadcasted_iota(jnp.int32, (ts, ts), 1)
    upper = jnp.where(t_row < t_col, 1.0, 0.0).astype(BF16)
    before = _dot(oh_sum.astype(BF16), upper)
    cnt = jnp.sum(oh_sum, axis=1, keepdims=True)
    cnt_pad = jnp.floor((cnt + (PIECE - 1)) * (1.0 / PIECE)) * PIECE
    cnt_b = jnp.broadcast_to(cnt_pad, (N_EXPERTS, ts))
    e_row = lax.broadcasted_iota(jnp.int32, (N_EXPERTS, N_EXPERTS), 0)
    e_col = lax.broadcasted_iota(jnp.int32, (N_EXPERTS, N_EXPERTS), 1)
    lower = jnp.where(e_col < e_row, 1.0, 0.0).astype(BF16)
    strip_off = _dot(lower, cnt_b.astype(BF16))
    slot = strip_off + before
    pos_f = [jnp.sum(oh * slot, axis=0, keepdims=True) for oh in onehots]
    pos_ref[...] = jnp.concatenate(pos_f + [pad_rows], axis=0).T.astype(jnp.int32)
    pos = [p.astype(jnp.int32) for p in pos_f]
    cnt_ref[0] = cnt_b[:, :LANES_V7X].astype(jnp.int32)
    yield
    for r0 in range(0, GROUP_ROWS, SORT_ROWS):
        r_iota = lax.broadcasted_iota(jnp.int32, (SORT_ROWS, ts), 0) + r0
        perm = jnp.zeros((SORT_ROWS, ts), F32)
        for k in range(TOP_K):
            perm = jnp.where(r_iota == pos[k], 1.0, perm)
        perm = perm.astype(BF16)
        xs_ref[r0:r0 + SORT_ROWS, :] = _dot(perm, x_hi)
        yield


def _mixer_call(x2d, batch, seq, layer, lw, tables, prev_moe=None):
    t = batch * seq
    ns = seq // TOK_BLOCK
    groups = t // TOK_BLOCK
    fused = prev_moe is not None
    const = lambda *shape: pl.BlockSpec(shape, lambda i, yp: (0,) * len(shape),
                                        pipeline_mode=pl.Buffered(1))
    stacked = lambda *shape: pl.BlockSpec((1,) + shape, lambda i, yp: (layer, 0, 0),
                                          pipeline_mode=pl.Buffered(1))
    cur = lambda i: jnp.minimum(i, groups - 1)
    prev = lambda i: jnp.maximum(i - 1, 0)
    seqtab = pl.BlockSpec((TOK_BLOCK, RET_QK_DIM), lambda i, yp: (cur(i) % ns, 0))
    route_prev = pl.BlockSpec((TOK_BLOCK, ROUTE_COLS), lambda i, yp: (prev(i), 0))
    tok_cur = lambda width: pl.BlockSpec((TOK_BLOCK, width), lambda i, yp: (cur(i), 0))
    in_specs = [tok_cur(D_MODEL)]
    operands = [x2d]
    scratch = [pltpu.VMEM((RET_HEADS, RET_QK_DIM, RET_V_DIM), F32),
               pltpu.VMEM((TOK_BLOCK, SG_WIDTH), BF16),
               pltpu.VMEM((TOK_BLOCK, D_MODEL), F32)]
    if fused:
        yb, ypiece, post, gatet, ln2_g, ln2_b = prev_moe
        in_specs += [pl.BlockSpec(memory_space=pl.ANY), tok_cur(ROUTE_COLS), tok_cur(ROUTE_COLS),
                     const(1, D_MODEL), const(1, D_MODEL)]
        operands += [yb, post, gatet, ln2_g, ln2_b]
        scratch += [pltpu.VMEM((2, GROUP_ROWS, D_MODEL), F32), pltpu.SemaphoreType.DMA((2,))]
    else:
        ypiece = jnp.zeros((1,), jnp.int32)
    in_specs += [
        stacked(D_MODEL, IN_WIDTH), stacked(RET_V, D_MODEL), stacked(SG_WIDTH, D_MODEL),
        stacked(D_MODEL, D_MODEL),
        const(SG_GROUPS, SG_BLOCK, SG_BLOCK), const(SG_BLOCK, SG_GROUPS),
        const(1, SG_WIDTH), const(1, SG_WIDTH), const(1, D_MODEL), const(1, D_MODEL),
        const(N_EXPERTS, D_MODEL), const(N_EXPERTS, TOK_BLOCK),
        seqtab, seqtab, seqtab, seqtab,
        const(RET_HEADS, TOK_BLOCK, TOK_BLOCK), const(RET_HEADS, TOK_BLOCK, RET_QK_DIM),
        const(RET_HEADS, TOK_BLOCK, RET_QK_DIM),
    ]
    out_shape = [
        jax.ShapeDtypeStruct((t + TOK_BLOCK, D_MODEL), F32),
        jax.ShapeDtypeStruct((t, ROUTE_COLS), jnp.int32),
        jax.ShapeDtypeStruct((t, ROUTE_COLS), F32),
        jax.ShapeDtypeStruct((groups, N_EXPERTS, LANES_V7X), jnp.int32),
        jax.ShapeDtypeStruct((groups * GROUP_ROWS, D_MODEL), F32),
    ]
    out_specs = [
        pl.BlockSpec((TOK_BLOCK, D_MODEL), lambda i, yp: (i, 0)), route_prev, route_prev,
        pl.BlockSpec((1, N_EXPERTS, LANES_V7X), lambda i, yp: (prev(i), 0, 0)),
        pl.BlockSpec((GROUP_ROWS, D_MODEL), lambda i, yp: (prev(i), 0)),
    ]
    operands += [lw["w_in"], lw["p_ret"], lw["p_sg"], lw["w_o"], lw["sg_w"], lw["sg_bt"],
                 lw["sg_ln_g"], lw["sg_ln_b"], lw["ln1_g"], lw["ln1_b"], lw["router_wt"], lw["router_bb"],
                 tables["cq"], tables["sq"], tables["ck"], tables["sk"],
                 tables["dmat"], tables["qdec"], tables["kdec"]]
    kern = functools.partial(_mixer_kernel, chunk_decay=tables["chunk_decay"], steps_per_seq=ns,
                             n_groups=groups, fused=fused)
    grid_spec = pltpu.PrefetchScalarGridSpec(
        num_scalar_prefetch=1, grid=(groups + 1,), in_specs=in_specs, out_specs=out_specs,
        scratch_shapes=scratch)
    return pl.pallas_call(
        kern,
        grid_spec=grid_spec,
        out_shape=out_shape,
        compiler_params=pltpu.CompilerParams(dimension_semantics=("arbitrary",),
                                             vmem_limit_bytes=VMEM_LIMIT_V7X),
        name="mixer",
    )(ypiece, *operands)


def _expert_kernel(bexp_ref, src_ref, nvb_ref, xs_hbm, w1_ref, b1_ref, w2_ref, b2_ref,
                   yb_ref, xbuf0, xbuf1, w1b, w2b, sem):
    b = pl.program_id(0)
    nvb = nvb_ref[0]
    xbufs = (xbuf0, xbuf1)

    def gather(blk, slot, pieces=range(EXP_PIECES)):
        for i in pieces:
            row = pl.multiple_of(src_ref[blk * EXP_PIECES + i] * PIECE, PIECE)
            pltpu.make_async_copy(xs_hbm.at[pl.ds(row, PIECE), :],
                                  xbufs[slot].at[pl.ds(i * PIECE, PIECE), :],
                                  sem.at[slot]).start()

    def wait(slot):
        pltpu.make_async_copy(xs_hbm.at[pl.ds(0, EXP_BLOCK), :], xbufs[slot], sem.at[slot]).wait()

    @pl.when(b == 0)
    def _():
        gather(0, 0)

    new_expert = (b == 0) | (bexp_ref[b] != bexp_ref[jnp.maximum(b - 1, 0)])

    @pl.when(new_expert & (b < nvb))
    def _():
        def cast_rows(i, carry):
            r = pl.multiple_of(i * W_CAST_ROWS, W_CAST_ROWS)
            w1b[pl.ds(r, W_CAST_ROWS), :] = w1_ref[0, 0, pl.ds(r, W_CAST_ROWS), :].astype(BF16)
            w2b[pl.ds(r, W_CAST_ROWS), :] = w2_ref[0, 0, pl.ds(r, W_CAST_ROWS), :].astype(BF16)
            return carry
        lax.fori_loop(0, D_MODEL // W_CAST_ROWS, cast_rows, 0)

    def block(cur):
        nxt = 1 - cur
        gather(jnp.minimum(b + 1, nvb - 1), nxt)
        wait(cur)
        x = xbufs[cur][...].astype(BF16)
        y = jnp.zeros((EXP_BLOCK, D_MODEL), F32)
        for j in range(D_EXPERT // F_CHUNK):
            c0 = j * F_CHUNK
            hg = _dot(x, w1b[:, c0:c0 + F_CHUNK]) + b1_ref[0, 0, :, c0:c0 + F_CHUNK]
            hu = (_dot(x, w1b[:, D_EXPERT + c0:D_EXPERT + c0 + F_CHUNK])
                  + b1_ref[0, 0, :, D_EXPERT + c0:D_EXPERT + c0 + F_CHUNK])
            gate = jnp.minimum(hg, SWIGLU_LIMIT)
            up = jnp.clip(hu, -SWIGLU_LIMIT, SWIGLU_LIMIT)
            act = gate * jax.nn.sigmoid(SWIGLU_ALPHA * gate) * (up + 1.0)
            y = y + _dot(act.astype(BF16), w2b[c0:c0 + F_CHUNK, :])
        yb_ref[...] = y + b2_ref[0, 0]

        @pl.when(b == nvb - 1)
        def _():
            wait(nxt)

    for cur in range(2):
        pl.when((b < nvb) & (b % 2 == cur))(functools.partial(block, cur))

    @pl.when(b >= nvb)
    def _():
        yb_ref[...] = jnp.zeros_like(yb_ref)


def _expert_call(xs, block_exp, src_piece, nvb, layer, e_w1, e_b1, e_w2, e_b2, n_blocks):
    wspec = lambda *shape: pl.BlockSpec((1, 1) + shape, lambda b, be, sp, nv: (layer, be[b], 0, 0))
    grid_spec = pltpu.PrefetchScalarGridSpec(
        num_scalar_prefetch=3,
        grid=(n_blocks,),
        in_specs=[
            pl.BlockSpec(memory_space=pl.ANY),
            wspec(D_MODEL, 2 * D_EXPERT), wspec(1, 2 * D_EXPERT),
            wspec(D_EXPERT, D_MODEL), wspec(1, D_MODEL),
        ],
        out_specs=pl.BlockSpec((EXP_BLOCK, D_MODEL), lambda b, be, sp, nv: (b, 0)),
        scratch_shapes=[pltpu.VMEM((EXP_BLOCK, D_MODEL), F32),
                        pltpu.VMEM((EXP_BLOCK, D_MODEL), F32),
                        pltpu.VMEM((D_MODEL, 2 * D_EXPERT), BF16),
                        pltpu.VMEM((D_EXPERT, D_MODEL), BF16),
                        pltpu.SemaphoreType.DMA((2,))],
    )
    return pl.pallas_call(
        _expert_kernel,
        grid_spec=grid_spec,
        out_shape=jax.ShapeDtypeStruct((n_blocks * EXP_BLOCK, D_MODEL), F32),
        compiler_params=pltpu.CompilerParams(dimension_semantics=("arbitrary",),
                                             vmem_limit_bytes=VMEM_LIMIT_V7X),
        name="experts",
    )(block_exp, src_piece, nvb, xs, e_w1, e_b1, e_w2, e_b2)


def _combine_kernel(yp_ref, x1_ref, yb_hbm, post_ref, gatet_ref, g_ref, b_ref, out_ref,
                    ybuf0, ybuf1, sem):
    g = pl.program_id(0)
    ng = pl.num_programs(0)
    ybufs = (ybuf0, ybuf1)

    def gather(grp, slot, pieces=range(GROUP_PIECES)):
        for i in pieces:
            row = pl.multiple_of(yp_ref[grp * GROUP_PIECES + i] * PIECE, PIECE)
            pltpu.make_async_copy(yb_hbm.at[pl.ds(row, PIECE), :],
                                  ybufs[slot].at[pl.ds(i * PIECE, PIECE), :],
                                  sem.at[slot]).start()

    def wait(slot):
        pltpu.make_async_copy(yb_hbm.at[pl.ds(0, GROUP_ROWS), :], ybufs[slot], sem.at[slot]).wait()

    @pl.when(g == 0)
    def _():
        gather(0, 0)

    def group(cur):
        nxt = 1 - cur
        gather(jnp.minimum(g + 1, ng - 1), nxt)
        slabs = _unsort_slabs(post_ref[...], gatet_ref[...])
        wait(cur)
        ffn = _unsort_matmul(slabs, ybufs[cur])
        out_ref[...] = _layer_norm(DN_ALPHA * x1_ref[...] + ffn, g_ref[...], b_ref[...])

        @pl.when(g == ng - 1)
        def _():
            wait(nxt)

    for cur in range(2):
        pl.when(g % 2 == cur)(functools.partial(group, cur))


def _combine_call(x1, yb, ypiece, post, gatet, ln_g, ln_b):
    t = post.shape[0]
    groups = t // TOK_BLOCK
    grid_spec = pltpu.PrefetchScalarGridSpec(
        num_scalar_prefetch=1,
        grid=(groups,),
        in_specs=[
            pl.BlockSpec((TOK_BLOCK, D_MODEL), lambda g, yp: (g, 0)),
            pl.BlockSpec(memory_space=pl.ANY),
            pl.BlockSpec((TOK_BLOCK, ROUTE_COLS), lambda g, yp: (g, 0)),
            pl.BlockSpec((TOK_BLOCK, ROUTE_COLS), lambda g, yp: (g, 0)),
            pl.BlockSpec((1, D_MODEL), lambda g, yp: (0, 0)),
            pl.BlockSpec((1, D_MODEL), lambda g, yp: (0, 0)),
        ],
        out_specs=pl.BlockSpec((TOK_BLOCK, D_MODEL), lambda g, yp: (g, 0)),
        scratch_shapes=[pltpu.VMEM((GROUP_ROWS, D_MODEL), F32), pltpu.VMEM((GROUP_ROWS, D_MODEL), F32),
                        pltpu.SemaphoreType.DMA((2,))],
    )
    return pl.pallas_call(
        _combine_kernel,
        grid_spec=grid_spec,
        out_shape=jax.ShapeDtypeStruct((t, D_MODEL), F32),
        compiler_params=pltpu.CompilerParams(dimension_semantics=("arbitrary",),
                                             vmem_limit_bytes=VMEM_LIMIT_V7X),
        name="combine",
    )(ypiece, x1, yb, post, gatet, ln_g, ln_b)


def _routing_tables(cnt_pad, n_blocks):
    groups = cnt_pad.shape[0]
    npc = cnt_pad // PIECE
    strip_start = jnp.cumsum(npc, axis=1) - npc
    tot = jnp.sum(npc, axis=0)
    tot_pad = (tot + EXP_PIECES - 1) // EXP_PIECES * EXP_PIECES
    exp_end = jnp.cumsum(tot_pad)
    exp_start = exp_end - tot_pad
    dstart = exp_start[None, :] + jnp.cumsum(npc, axis=0) - npc
    nvb = (exp_end[-1] // EXP_PIECES).astype(jnp.int32)

    blocks = jnp.arange(n_blocks, dtype=jnp.int32)
    bexp = jnp.sum(blocks[:, None] * EXP_PIECES >= exp_end[None, :], axis=1).astype(jnp.int32)
    bexp = jnp.minimum(bexp, N_EXPERTS - 1)
    last = jnp.sum((nvb - 1) * EXP_PIECES >= exp_end).astype(jnp.int32)
    bexp = jnp.where(blocks < nvb, bexp, jnp.minimum(last, N_EXPERTS - 1))

    d = jnp.arange(n_blocks * EXP_PIECES, dtype=jnp.int32)
    e_of = jnp.minimum(jnp.sum(d[:, None] >= exp_end[None, :], axis=1), N_EXPERTS - 1)
    oh_e = (e_of[:, None] == jnp.arange(N_EXPERTS, dtype=jnp.int32)[None, :]).astype(F32)
    row_of = lambda tab: jnp.dot(oh_e, tab.T.astype(F32), precision=lax.Precision.HIGHEST).astype(jnp.int32)
    ds_e = row_of(dstart)
    g_of = jnp.maximum(jnp.sum(ds_e <= d[:, None], axis=1) - 1, 0)
    at_g = g_of[:, None] == jnp.arange(groups, dtype=jnp.int32)[None, :]
    pick = lambda rows: jnp.sum(jnp.where(at_g, rows, 0), axis=1)
    i_of = d - pick(ds_e)
    valid = (i_of >= 0) & (i_of < pick(row_of(npc)))
    src = g_of * GROUP_PIECES + pick(row_of(strip_start)) + i_of
    src_piece = jnp.where(valid, src, 0).astype(jnp.int32)

    j = jnp.arange(GROUP_PIECES, dtype=jnp.int32)
    e_loc = jnp.maximum(jnp.sum(strip_start[:, None, :] <= j[None, :, None], axis=2) - 1, 0)
    at_e = e_loc[:, :, None] == jnp.arange(N_EXPERTS, dtype=jnp.int32)[None, None, :]
    pick_e = lambda tab: jnp.sum(jnp.where(at_e, tab[:, None, :], 0), axis=2)
    st_loc = pick_e(strip_start)
    np_loc = pick_e(npc)
    ds_loc = pick_e(dstart)
    i_loc = j[None, :] - st_loc
    ypiece = jnp.where(i_loc < np_loc, ds_loc + i_loc, 0).astype(jnp.int32).reshape(groups * GROUP_PIECES)
    return bexp, src_piece, nvb.reshape(1), ypiece


def _tables(seq):
    half = RET_QK_DIM // 2
    inv = ROPE_BASE ** (-jnp.arange(half, dtype=F32) / half)
    ang = jnp.arange(seq, dtype=jnp.int32).astype(F32)[:, None] * inv[None, :]
    cos, sin = jnp.cos(ang), jnp.sin(ang)
    cosf = jnp.concatenate([cos, cos], axis=1)
    sinf = jnp.concatenate([-sin, sin], axis=1)
    kscale = RET_QK_DIM ** -0.5
    log_g = jnp.log(1.0 - jnp.exp(jnp.linspace(math.log(1.0 / 32), math.log(1.0 / 512), RET_HEADS)))
    idx = jnp.arange(TOK_BLOCK, dtype=F32)
    diff = idx[:, None] - idx[None, :]
    dmat = jnp.where(diff >= 0, jnp.exp(log_g[:, None, None] * jnp.maximum(diff, 0.0)), 0.0)
    qdec = jnp.exp(log_g[:, None] * (idx[None, :] + 1.0))
    kdec = jnp.exp(log_g[:, None] * (TOK_BLOCK - 1.0 - idx[None, :]))
    bc = lambda a: jnp.broadcast_to(a[:, :, None], (RET_HEADS, TOK_BLOCK, RET_QK_DIM))
    log_g_host = np.log(1.0 - np.exp(np.linspace(math.log(1.0 / 32), math.log(1.0 / 512), RET_HEADS)))
    chunk_decay = tuple(float(np.float32(np.exp(np.float32(lg) * np.float32(TOK_BLOCK)))) for lg in log_g_host)
    return {"cq": cosf, "sq": sinf, "ck": cosf * kscale, "sk": sinf * kscale,
            "dmat": dmat.astype(F32), "qdec": bc(qdec), "kdec": bc(kdec), "chunk_decay": chunk_decay}


def kernel(x, w_in, p_ret, sg_ln_g, sg_ln_b, sg_w, sg_b, p_sg, w_o, ln1_g, ln1_b,
           router_w, router_b, e_w1, e_b1, e_w2, e_b2, ln2_g, ln2_b):
    batch, seq, d = x.shape
    depth = w_in.shape[0]
    assert d == D_MODEL and seq % TOK_BLOCK == 0
    t = batch * seq
    groups = t // TOK_BLOCK
    used_pieces = (TOK_BLOCK * TOP_K + N_EXPERTS * (PIECE - 1)) // PIECE
    n_blocks = (groups * used_pieces + N_EXPERTS * (EXP_PIECES - 1)) // EXP_PIECES
    n_blocks = -(-n_blocks // SUBLANES_V7X) * SUBLANES_V7X
    tables = _tables(seq)
    xc = x.reshape(t, D_MODEL)
    w_in_b, p_ret_b, p_sg_b, w_o_b = (w.astype(BF16) for w in (w_in, p_ret, p_sg, w_o))
    e_b1r = e_b1.reshape(depth, N_EXPERTS, 1, 2 * D_EXPERT)
    e_b2r = e_b2.reshape(depth, N_EXPERTS, 1, D_MODEL)
    prev_moe = None
    for l in range(depth):
        lw = {
            "w_in": w_in_b, "p_ret": p_ret_b, "p_sg": p_sg_b,
            "w_o": w_o_b, "sg_w": sg_w[l], "sg_bt": sg_b[l].T,
            "sg_ln_g": sg_ln_g[l][None], "sg_ln_b": sg_ln_b[l][None],
            "ln1_g": ln1_g[l][None], "ln1_b": ln1_b[l][None],
            "router_wt": router_w[l].T,
            "router_bb": jnp.broadcast_to(router_b[l][:, None], (N_EXPERTS, TOK_BLOCK)),
        }
        x1, pos, gates, cnt, xs = _mixer_call(xc, batch, seq, l, lw, tables, prev_moe)
        bexp, src_piece, nvb, ypiece = _routing_tables(cnt[:, :, 0], n_blocks)
        yb = _expert_call(xs, bexp, src_piece, nvb, l, e_w1, e_b1r, e_w2, e_b2r, n_blocks)
        prev_moe = (yb, ypiece, pos, gates, ln2_g[l][None], ln2_b[l][None])
        xc = x1
    out = _combine_call(xc, *prev_moe)
    return out.reshape(batch, seq, D_MODEL)
```

```python
import functools
import math

import jax
import jax.numpy as jnp
import numpy as np
from jax import lax
from jax.experimental import pallas as pl
from jax.experimental.pallas import tpu as pltpu

F32 = jnp.float32
BF16 = jnp.bfloat16

D_MODEL = 1024
DEPTH = 4
RET_HEADS = 4
RET_QK_DIM = 128
RET_V_DIM = 256
RET_QK = RET_HEADS * RET_QK_DIM
RET_V = RET_HEADS * RET_V_DIM
ROPE_BASE = 10000.0
SG_BLOCK = 128
SG_CHUNK = 64
SG_GROUPS = 8
SG_WIDTH = D_MODEL
N_EXPERTS = 32
TOP_K = 4
D_EXPERT = D_MODEL
SWIGLU_LIMIT = 7.0
SWIGLU_ALPHA = 1.702
LN_EPS = 1e-5
DN_ALPHA = (2 * DEPTH) ** 0.25

OFF_Q = 0
OFF_K = OFF_Q + RET_QK
OFF_V = OFF_K + RET_QK
OFF_G = OFF_V + RET_V
OFF_U = OFF_G + RET_V
OFF_VS = OFF_U + SG_WIDTH
OFF_GA = OFF_VS + SG_WIDTH
OFF_GB = OFF_GA + D_MODEL
IN_WIDTH = OFF_GB + D_MODEL

SUBLANES_V7X = 8
LANES_V7X = 128
TOK_BLOCK = 256
PIECE = SUBLANES_V7X
GROUP_ROWS = TOK_BLOCK * TOP_K + N_EXPERTS * PIECE
GROUP_PIECES = GROUP_ROWS // PIECE
SORT_ROWS = 256
ROUTE_COLS = SUBLANES_V7X
EXP_BLOCK = 512
EXP_PIECES = EXP_BLOCK // PIECE
F_CHUNK = 512
W_CAST_ROWS = 128
VMEM_LIMIT_V7X = 60 * 1024 * 1024


def _layer_norm(x, g, b):
    mu = jnp.mean(x, axis=-1, keepdims=True)
    xc = x - mu
    var = jnp.mean(xc * xc, axis=-1, keepdims=True)
    return xc * lax.rsqrt(var + LN_EPS) * g + b


def _gelu_tanh(x):
    c = math.sqrt(2.0 / math.pi)
    return 0.5 * x * (1.0 + jnp.tanh(c * (x + 0.044715 * (x * x * x))))


def _dot(a, b):
    return jnp.dot(a, b, preferred_element_type=F32)


def _dot_nt(a, b):
    return lax.dot_general(a, b, (((1,), (1,)), ((), ())), preferred_element_type=F32)


def _dot_tn(a, b):
    return lax.dot_general(a, b, (((0,), (0,)), ((), ())), preferred_element_type=F32)


def _split_bf16(x):
    hi = x.astype(BF16)
    lo = (x - hi.astype(F32)).astype(BF16)
    return hi, lo


def _unsort_slab(post, gatet, r0):
    c_iota = lax.broadcasted_iota(jnp.int32, (TOK_BLOCK, SORT_ROWS), 1) + r0
    unsort = jnp.zeros((TOK_BLOCK, SORT_ROWS), F32)
    for k in range(TOP_K):
        unsort = jnp.where(c_iota == post[:, k:k + 1], gatet[:, k:k + 1], unsort)
    return unsort.astype(BF16)


def _unsort_slabs(post, gatet):
    return [_unsort_slab(post, gatet, r0) for r0 in range(0, GROUP_ROWS, SORT_ROWS)]


def _unsort_matmul(slabs, ybuf):
    ffn = jnp.zeros((TOK_BLOCK, D_MODEL), F32)
    for i, r0 in enumerate(range(0, GROUP_ROWS, SORT_ROWS)):
        ffn = ffn + _dot(slabs[i], ybuf[r0:r0 + SORT_ROWS, :].astype(BF16))
    return ffn


def _mixer_kernel(yp_ref, *refs, chunk_decay, steps_per_seq, n_groups, fused):
    if fused:
        x_ref, yb_hbm, post_ref, gatet_ref, ln2g_ref, ln2b_ref = refs[:6]
        refs = refs[6:]
        ybuf, sem = refs[-2:]
        refs = refs[:-2]
    else:
        x_ref = refs[0]
        refs = refs[1:]
    (w_in_ref, p_ret_ref, p_sg_ref, w_o_ref, sgw_ref, sgbt_ref,
     sglng_ref, sglnb_ref, ln1g_ref, ln1b_ref, rwt_ref, rbb_ref,
     cq_ref, sq_ref, ck_ref, sk_ref, dmat_ref, qdec_ref, kdec_ref,
     x1_ref, pos_ref, gate_ref, cnt_ref, xs_ref,
     state_ref, ysg_ref, x1_prev_ref) = refs
    ts = TOK_BLOCK
    step = pl.program_id(0)

    @pl.when(step == 0)
    def _():
        x1_prev_ref[...] = jnp.zeros_like(x1_prev_ref)

    @pl.when(step % steps_per_seq == 0)
    def _():
        state_ref[...] = jnp.zeros_like(state_ref)

    if fused:
        slot = step % 2

        def gather(grp, s):
            for i in range(GROUP_PIECES):
                row = pl.multiple_of(yp_ref[grp * GROUP_PIECES + i] * PIECE, PIECE)
                pltpu.make_async_copy(yb_hbm.at[pl.ds(row, PIECE), :],
                                      ybuf.at[s, pl.ds(i * PIECE, PIECE), :], sem.at[s]).start()

        def wait(s):
            pltpu.make_async_copy(yb_hbm.at[pl.ds(0, GROUP_ROWS), :], ybuf.at[s], sem.at[s]).wait()

        @pl.when(step == 0)
        def _():
            gather(0, 0)

    routing = _route_and_sort(x1_prev_ref[...], rwt_ref, rbb_ref, pos_ref, gate_ref, cnt_ref, xs_ref)
    routing_stage = lambda: next(routing, None)

    if fused:
        slabs = _unsort_slabs(post_ref[...], gatet_ref[...])
        wait(slot)
        ffn = _unsort_matmul(slabs, ybuf.at[slot])
        xb = _layer_norm(DN_ALPHA * x_ref[...] + ffn, ln2g_ref[...], ln2b_ref[...])
        gather(jnp.minimum(step + 1, n_groups - 1), 1 - slot)
    else:
        xb = x_ref[...]
    xbf = xb.astype(BF16)

    def proj(off, width):
        return _dot(xbf, w_in_ref[0, :, off:off + width])

    cq, sq, ck, sk = cq_ref[...], sq_ref[...], ck_ref[...], sk_ref[...]
    acc_ret = jnp.zeros((ts, D_MODEL), F32)
    pair, head_in = {}, {}

    def project_head(h):
        if h % 2 == 0:
            pair["q"] = proj(OFF_Q + h * RET_QK_DIM, 2 * RET_QK_DIM)
            yield
            pair["k"] = proj(OFF_K + h * RET_QK_DIM, 2 * RET_QK_DIM)
            yield
        lanes = slice((h % 2) * RET_QK_DIM, (h % 2 + 1) * RET_QK_DIM)
        v = proj(OFF_V + h * RET_V_DIM, RET_V_DIM).astype(BF16)
        yield
        g = proj(OFF_G + h * RET_V_DIM, RET_V_DIM)
        head_in[h] = (pair["q"][:, lanes], pair["k"][:, lanes], v, g)
        yield

    for _ in project_head(0):
        pass
    for h in range(RET_HEADS):
        upcoming = project_head(h + 1) if h + 1 < RET_HEADS else iter(())
        q, k, v, g = head_in.pop(h)
        routing_stage()
        next(upcoming, None)
        qr = q * cq + pltpu.roll(q, RET_QK_DIM // 2, 1) * sq
        kr = k * ck + pltpu.roll(k, RET_QK_DIM // 2, 1) * sk
        scores = _dot_nt(qr.astype(BF16), kr.astype(BF16)) * dmat_ref[h]
        next(upcoming, None)
        inner = _dot(scores.astype(BF16), v)
        st = state_ref[h]
        cross = _dot((qr * qdec_ref[h]).astype(BF16), st.astype(BF16))
        kd = (kr * kdec_ref[h]).astype(BF16)
        state_ref[h] = st * chunk_decay[h] + _dot_tn(kd, v)
        routing_stage()
        next(upcoming, None)
        y = inner + cross
        mu = jnp.mean(y, axis=-1, keepdims=True)
        yc = y - mu
        var = jnp.mean(yc * yc, axis=-1, keepdims=True)
        yn = yc * lax.rsqrt(var + LN_EPS)
        next(upcoming, None)
        yr = yn * (g * jax.nn.sigmoid(g))
        acc_ret = acc_ret + _dot(yr.astype(BF16), p_ret_ref[0, h * RET_V_DIM:(h + 1) * RET_V_DIM, :])
        for _ in upcoming:
            pass
        if fused and h == 1:
            @pl.when(step == n_groups)
            def _():
                wait(1 - slot)

    pre_u = proj(OFF_U, SG_WIDTH)
    pre_vs = proj(OFF_VS, SG_WIDTH)
    ga = proj(OFF_GA, D_MODEL)
    u = _gelu_tanh(pre_u)
    routing_stage()
    gb = proj(OFF_GB, D_MODEL)
    vn = _layer_norm(_gelu_tanh(pre_vs), sglng_ref[...], sglnb_ref[...])
    qi = lax.broadcasted_iota(jnp.int32, (SG_BLOCK, SG_BLOCK), 0) // SG_CHUNK
    pi = lax.broadcasted_iota(jnp.int32, (SG_BLOCK, SG_BLOCK), 1) // SG_CHUNK
    sg_mask = qi >= pi
    for gi in range(SG_GROUPS):
        wm = jnp.where(sg_mask, sgw_ref[gi], 0.0).astype(BF16)
        bias = sgbt_ref[:, gi:gi + 1]
        cols = slice(gi * SG_BLOCK, (gi + 1) * SG_BLOCK)
        blocks = [slice(blk * SG_BLOCK, (blk + 1) * SG_BLOCK) for blk in range(ts // SG_BLOCK)]
        vcat = jnp.concatenate([vn[rows, cols] for rows in blocks], axis=1).astype(BF16)
        mixed = _dot(wm, vcat) + bias
        for blk, rows in enumerate(blocks):
            ysg_ref[rows, cols] = (u[rows, cols] * mixed[:, blk * SG_BLOCK:(blk + 1) * SG_BLOCK]).astype(BF16)
    routing_stage()
    acc_sg = _dot(ysg_ref[...], p_sg_ref[0])

    hmix = jax.nn.sigmoid(ga) * acc_ret + jax.nn.sigmoid(gb) * acc_sg
    mix = _dot(hmix.astype(BF16), w_o_ref[0])
    for _ in routing:
        pass
    x1 = _layer_norm(DN_ALPHA * xb + mix, ln1g_ref[...], ln1b_ref[...])
    x1_ref[...] = x1
    x1_prev_ref[...] = x1


def _route_and_sort(x1, rwt_ref, rbb_ref, pos_ref, gate_ref, cnt_ref, xs_ref):
    ts = TOK_BLOCK
    x_hi, x_lo = _split_bf16(x1)
    w_hi, w_lo = _split_bf16(rwt_ref[...])
    logits = (_dot_nt(w_hi, x_hi) + _dot_nt(w_hi, x_lo) + _dot_nt(w_lo, x_hi)) + rbb_ref[...]
    yield
    e_iota = lax.broadcasted_iota(jnp.int32, (N_EXPERTS, ts), 0).astype(F32)
    sels, vals = [], []
    work = logits
    for _ in range(TOP_K):
        m = jnp.max(work, axis=0, keepdims=True)
        idx = jnp.min(jnp.where(work == m, e_iota, float(N_EXPERTS)), axis=0, keepdims=True)
        sel = e_iota == idx
        work = jnp.where(sel, -jnp.inf, work)
        sels.append(sel)
        vals.append(m)
        yield
    exps = [jnp.exp(v - vals[0]) for v in vals]
    denom = exps[0] + exps[1] + exps[2] + exps[3]
    pad_rows = jnp.zeros((ROUTE_COLS - TOP_K, ts), F32)
    gate_ref[...] = jnp.concatenate([e / denom for e in exps] + [pad_rows], axis=0).T

    onehots = [jnp.where(s, 1.0, 0.0) for s in sels]
    oh_sum = onehots[0] + onehots[1] + onehots[2] + onehots[3]
    t_row = lax.broadcasted_iota(jnp.int32, (ts, ts), 0)
    t_col = lax.broadcasted_iota(jnp.int32, (ts, ts), 1)
    upper = jnp.where(t_row < t_col, 1.0, 0.0).astype(BF16)
    before = _dot(oh_sum.astype(BF16), upper)
    cnt = jnp.sum(oh_sum, axis=1, keepdims=True)
    cnt_pad = jnp.floor((cnt + (PIECE - 1)) * (1.0 / PIECE)) * PIECE
    cnt_b = jnp.broadcast_to(cnt_pad, (N_EXPERTS, ts))
    e_row = lax.broadcasted_iota(jnp.int32, (N_EXPERTS, N_EXPERTS), 0)
    e_col = lax.broadcasted_iota(jnp.int32, (N_EXPERTS, N_EXPERTS), 1)
    lower = jnp.where(e_col < e_row, 1.0, 0.0).astype(BF16)
    strip_off = _dot(lower, cnt_b.astype(BF16))
    slot = strip_off + before
    pos_f = [jnp.sum(oh * slot, axis=0, keepdims=True) for oh in onehots]
    pos_ref[...] = jnp.concatenate(pos_f + [pad_rows], axis=0).T.astype(jnp.int32)
    pos = [p.astype(jnp.int32) for p in pos_f]
    cnt_ref[0] = cnt_b[:, :LANES_V7X].astype(jnp.int32)
    yield
    for r0 in range(0, GROUP_ROWS, SORT_ROWS):
        r_iota = lax.broadcasted_iota(jnp.int32, (SORT_ROWS, ts), 0) + r0
        perm = jnp.zeros((SORT_ROWS, ts), F32)
        for k in range(TOP_K):
            perm = jnp.where(r_iota == pos[k], 1.0, perm)
        perm = perm.astype(BF16)
        xs_ref[r0:r0 + SORT_ROWS, :] = _dot(perm, x_hi)
        yield


def _mixer_call(x2d, batch, seq, layer, lw, tables, prev_moe=None):
    t = batch * seq
    ns = seq // TOK_BLOCK
    groups = t // TOK_BLOCK
    fused = prev_moe is not None
    const = lambda *shape: pl.BlockSpec(shape, lambda i, yp: (0,) * len(shape),
                                        pipeline_mode=pl.Buffered(1))
    stacked = lambda *shape: pl.BlockSpec((1,) + shape, lambda i, yp: (layer, 0, 0),
                                          pipeline_mode=pl.Buffered(1))
    cur = lambda i: jnp.minimum(i, groups - 1)
    prev = lambda i: jnp.maximum(i - 1, 0)
    seqtab = pl.BlockSpec((TOK_BLOCK, RET_QK_DIM), lambda i, yp: (cur(i) % ns, 0))
    route_prev = pl.BlockSpec((TOK_BLOCK, ROUTE_COLS), lambda i, yp: (prev(i), 0))
    tok_cur = lambda width: pl.BlockSpec((TOK_BLOCK, width), lambda i, yp: (cur(i), 0))
    in_specs = [tok_cur(D_MODEL)]
    operands = [x2d]
    scratch = [pltpu.VMEM((RET_HEADS, RET_QK_DIM, RET_V_DIM), F32),
               pltpu.VMEM((TOK_BLOCK, SG_WIDTH), BF16),
               pltpu.VMEM((TOK_BLOCK, D_MODEL), F32)]
    if fused:
        yb, ypiece, post, gatet, ln2_g, ln2_b = prev_moe
        in_specs += [pl.BlockSpec(memory_space=pl.ANY), tok_cur(ROUTE_COLS), tok_cur(ROUTE_COLS),
                     const(1, D_MODEL), const(1, D_MODEL)]
        operands += [yb, post, gatet, ln2_g, ln2_b]
        scratch += [pltpu.VMEM((2, GROUP_ROWS, D_MODEL), F32), pltpu.SemaphoreType.DMA((2,))]
    else:
        ypiece = jnp.zeros((1,), jnp.int32)
    in_specs += [
        stacked(D_MODEL, IN_WIDTH), stacked(RET_V, D_MODEL), stacked(SG_WIDTH, D_MODEL),
        stacked(D_MODEL, D_MODEL),
        const(SG_GROUPS, SG_BLOCK, SG_BLOCK), const(SG_BLOCK, SG_GROUPS),
        const(1, SG_WIDTH), const(1, SG_WIDTH), const(1, D_MODEL), const(1, D_MODEL),
        const(N_EXPERTS, D_MODEL), const(N_EXPERTS, TOK_BLOCK),
        seqtab, seqtab, seqtab, seqtab,
        const(RET_HEADS, TOK_BLOCK, TOK_BLOCK), const(RET_HEADS, TOK_BLOCK, RET_QK_DIM),
        const(RET_HEADS, TOK_BLOCK, RET_QK_DIM),
    ]
    out_shape = [
        jax.ShapeDtypeStruct((t + TOK_BLOCK, D_MODEL), F32),
        jax.ShapeDtypeStruct((t, ROUTE_COLS), jnp.int32),
        jax.ShapeDtypeStruct((t, ROUTE_COLS), F32),
        jax.ShapeDtypeStruct((groups, N_EXPERTS, LANES_V7X), jnp.int32),
        jax.ShapeDtypeStruct((groups * GROUP_ROWS, D_MODEL), F32),
    ]
    out_specs = [
        pl.BlockSpec((TOK_BLOCK, D_MODEL), lambda i, yp: (i, 0)), route_prev, route_prev,
        pl.BlockSpec((1, N_EXPERTS, LANES_V7X), lambda i, yp: (prev(i), 0, 0)),
        pl.BlockSpec((GROUP_ROWS, D_MODEL), lambda i, yp: (prev(i), 0)),
    ]
    operands += [lw["w_in"], lw["p_ret"], lw["p_sg"], lw["w_o"], lw["sg_w"], lw["sg_bt"],
                 lw["sg_ln_g"], lw["sg_ln_b"], lw["ln1_g"], lw["ln1_b"], lw["router_wt"], lw["router_bb"],
                 tables["cq"], tables["sq"], tables["ck"], tables["sk"],
                 tables["dmat"], tables["qdec"], tables["kdec"]]
    kern = functools.partial(_mixer_kernel, chunk_decay=tables["chunk_decay"], steps_per_seq=ns,
                             n_groups=groups, fused=fused)
    grid_spec = pltpu.PrefetchScalarGridSpec(
        num_scalar_prefetch=1, grid=(groups + 1,), in_specs=in_specs, out_specs=out_specs,
        scratch_shapes=scratch)
    return pl.pallas_call(
        kern,
        grid_spec=grid_spec,
        out_shape=out_shape,
        compiler_params=pltpu.CompilerParams(dimension_semantics=("arbitrary",),
                                             vmem_limit_bytes=VMEM_LIMIT_V7X),
        name="mixer",
    )(ypiece, *operands)


def _expert_kernel(bexp_ref, src_ref, nvb_ref, xs_hbm, w1_ref, b1_ref, w2_ref, b2_ref,
                   yb_ref, xbuf0, xbuf1, w1b, w2b, sem):
    b = pl.program_id(0)
    nvb = nvb_ref[0]
    xbufs = (xbuf0, xbuf1)

    def gather(blk, slot, pieces=range(EXP_PIECES)):
        for i in pieces:
            row = pl.multiple_of(src_ref[blk * EXP_PIECES + i] * PIECE, PIECE)
            pltpu.make_async_copy(xs_hbm.at[pl.ds(row, PIECE), :],
                                  xbufs[slot].at[pl.ds(i * PIECE, PIECE), :],
                                  sem.at[slot]).start()

    def wait(slot):
        pltpu.make_async_copy(xs_hbm.at[pl.ds(0, EXP_BLOCK), :], xbufs[slot], sem.at[slot]).wait()

    @pl.when(b == 0)
    def _():
        gather(0, 0)

    new_expert = (b == 0) | (bexp_ref[b] != bexp_ref[jnp.maximum(b - 1, 0)])

    @pl.when(new_expert & (b < nvb))
    def _():
        def cast_rows(i, carry):
            r = pl.multiple_of(i * W_CAST_ROWS, W_CAST_ROWS)
            w1b[pl.ds(r, W_CAST_ROWS), :] = w1_ref[0, 0, pl.ds(r, W_CAST_ROWS), :].astype(BF16)
            w2b[pl.ds(r, W_CAST_ROWS), :] = w2_ref[0, 0, pl.ds(r, W_CAST_ROWS), :].astype(BF16)
            return carry
        lax.fori_loop(0, D_MODEL // W_CAST_ROWS, cast_rows, 0)

    def block(cur):
        nxt = 1 - cur
        gather(jnp.minimum(b + 1, nvb - 1), nxt)
        wait(cur)
        x = xbufs[cur][...].astype(BF16)
        y = jnp.zeros((EXP_BLOCK, D_MODEL), F32)
        for j in range(D_EXPERT // F_CHUNK):
            c0 = j * F_CHUNK
            hg = _dot(x, w1b[:, c0:c0 + F_CHUNK]) + b1_ref[0, 0, :, c0:c0 + F_CHUNK]
            hu = (_dot(x, w1b[:, D_EXPERT + c0:D_EXPERT + c0 + F_CHUNK])
                  + b1_ref[0, 0, :, D_EXPERT + c0:D_EXPERT + c0 + F_CHUNK])
            gate = jnp.minimum(hg, SWIGLU_LIMIT)
            up = jnp.clip(hu, -SWIGLU_LIMIT, SWIGLU_LIMIT)
            act = gate * jax.nn.sigmoid(SWIGLU_ALPHA * gate) * (up + 1.0)
            y = y + _dot(act.astype(BF16), w2b[c0:c0 + F_CHUNK, :])
        yb_ref[...] = y + b2_ref[0, 0]

        @pl.when(b == nvb - 1)
        def _():
            wait(nxt)

    for cur in range(2):
        pl.when((b < nvb) & (b % 2 == cur))(functools.partial(block, cur))

    @pl.when(b >= nvb)
    def _():
        yb_ref[...] = jnp.zeros_like(yb_ref)


def _expert_call(xs, block_exp, src_piece, nvb, layer, e_w1, e_b1, e_w2, e_b2, n_blocks):
    wspec = lambda *shape: pl.BlockSpec((1, 1) + shape, lambda b, be, sp, nv: (layer, be[b], 0, 0))
    grid_spec = pltpu.PrefetchScalarGridSpec(
        num_scalar_prefetch=3,
        grid=(n_blocks,),
        in_specs=[
            pl.BlockSpec(memory_space=pl.ANY),
            wspec(D_MODEL, 2 * D_EXPERT), wspec(1, 2 * D_EXPERT),
            wspec(D_EXPERT, D_MODEL), wspec(1, D_MODEL),
        ],
        out_specs=pl.BlockSpec((EXP_BLOCK, D_MODEL), lambda b, be, sp, nv: (b, 0)),
        scratch_shapes=[pltpu.VMEM((EXP_BLOCK, D_MODEL), F32),
                        pltpu.VMEM((EXP_BLOCK, D_MODEL), F32),
                        pltpu.VMEM((D_MODEL, 2 * D_EXPERT), BF16),
                        pltpu.VMEM((D_EXPERT, D_MODEL), BF16),
                        pltpu.SemaphoreType.DMA((2,))],
    )
    return pl.pallas_call(
        _expert_kernel,
        grid_spec=grid_spec,
        out_shape=jax.ShapeDtypeStruct((n_blocks * EXP_BLOCK, D_MODEL), F32),
        compiler_params=pltpu.CompilerParams(dimension_semantics=("arbitrary",),
                                             vmem_limit_bytes=VMEM_LIMIT_V7X),
        name="experts",
    )(block_exp, src_piece, nvb, xs, e_w1, e_b1, e_w2, e_b2)


def _combine_kernel(yp_ref, x1_ref, yb_hbm, post_ref, gatet_ref, g_ref, b_ref, out_ref,
                    ybuf0, ybuf1, sem):
    g = pl.program_id(0)
    ng = pl.num_programs(0)
    ybufs = (ybuf0, ybuf1)

    def gather(grp, slot, pieces=range(GROUP_PIECES)):
        for i in pieces:
            row = pl.multiple_of(yp_ref[grp * GROUP_PIECES + i] * PIECE, PIECE)
            pltpu.make_async_copy(yb_hbm.at[pl.ds(row, PIECE), :],
                                  ybufs[slot].at[pl.ds(i * PIECE, PIECE), :],
                                  sem.at[slot]).start(priority=i % 2)

    def wait(slot):
        pltpu.make_async_copy(yb_hbm.at[pl.ds(0, GROUP_ROWS), :], ybufs[slot], sem.at[slot]).wait()

    @pl.when(g == 0)
    def _():
        gather(0, 0)

    def group(cur):
        nxt = 1 - cur
        gather(jnp.minimum(g + 1, ng - 1), nxt)
        slabs = _unsort_slabs(post_ref[...], gatet_ref[...])
        wait(cur)
        ffn = _unsort_matmul(slabs, ybufs[cur])
        out_ref[...] = _layer_norm(DN_ALPHA * x1_ref[...] + ffn, g_ref[...], b_ref[...])

        @pl.when(g == ng - 1)
        def _():
            wait(nxt)

    for cur in range(2):
        pl.when(g % 2 == cur)(functools.partial(group, cur))


def _combine_call(x1, yb, ypiece, post, gatet, ln_g, ln_b):
    t = post.shape[0]
    groups = t // TOK_BLOCK
    grid_spec = pltpu.PrefetchScalarGridSpec(
        num_scalar_prefetch=1,
        grid=(groups,),
        in_specs=[
            pl.BlockSpec((TOK_BLOCK, D_MODEL), lambda g, yp: (g, 0)),
            pl.BlockSpec(memory_space=pl.ANY),
            pl.BlockSpec((TOK_BLOCK, ROUTE_COLS), lambda g, yp: (g, 0)),
            pl.BlockSpec((TOK_BLOCK, ROUTE_COLS), lambda g, yp: (g, 0)),
            pl.BlockSpec((1, D_MODEL), lambda g, yp: (0, 0)),
            pl.BlockSpec((1, D_MODEL), lambda g, yp: (0, 0)),
        ],
        out_specs=pl.BlockSpec((TOK_BLOCK, D_MODEL), lambda g, yp: (g, 0)),
        scratch_shapes=[pltpu.VMEM((GROUP_ROWS, D_MODEL), F32), pltpu.VMEM((GROUP_ROWS, D_MODEL), F32),
                        pltpu.SemaphoreType.DMA((2,))],
    )
    return pl.pallas_call(
        _combine_kernel,
        grid_spec=grid_spec,
        out_shape=jax.ShapeDtypeStruct((t, D_MODEL), F32),
        compiler_params=pltpu.CompilerParams(dimension_semantics=("arbitrary",),
                                             vmem_limit_bytes=VMEM_LIMIT_V7X),
        name="combine",
    )(ypiece, x1, yb, post, gatet, ln_g, ln_b)


def _routing_tables(cnt_pad, n_blocks):
    groups = cnt_pad.shape[0]
    npc = cnt_pad // PIECE
    strip_start = jnp.cumsum(npc, axis=1) - npc
    tot = jnp.sum(npc, axis=0)
    tot_pad = (tot + EXP_PIECES - 1) // EXP_PIECES * EXP_PIECES
    exp_end = jnp.cumsum(tot_pad)
    exp_start = exp_end - tot_pad
    dstart = exp_start[None, :] + jnp.cumsum(npc, axis=0) - npc
    nvb = (exp_end[-1] // EXP_PIECES).astype(jnp.int32)

    blocks = jnp.arange(n_blocks, dtype=jnp.int32)
    bexp = jnp.sum(blocks[:, None] * EXP_PIECES >= exp_end[None, :], axis=1).astype(jnp.int32)
    bexp = jnp.minimum(bexp, N_EXPERTS - 1)
    last = jnp.sum((nvb - 1) * EXP_PIECES >= exp_end).astype(jnp.int32)
    bexp = jnp.where(blocks < nvb, bexp, jnp.minimum(last, N_EXPERTS - 1))

    d = jnp.arange(n_blocks * EXP_PIECES, dtype=jnp.int32)
    e_of = jnp.minimum(jnp.sum(d[:, None] >= exp_end[None, :], axis=1), N_EXPERTS - 1)
    oh_e = (e_of[:, None] == jnp.arange(N_EXPERTS, dtype=jnp.int32)[None, :]).astype(F32)
    row_of = lambda tab: jnp.dot(oh_e, tab.T.astype(F32), precision=lax.Precision.HIGHEST).astype(jnp.int32)
    ds_e = row_of(dstart)
    g_of = jnp.maximum(jnp.sum(ds_e <= d[:, None], axis=1) - 1, 0)
    at_g = g_of[:, None] == jnp.arange(groups, dtype=jnp.int32)[None, :]
    pick = lambda rows: jnp.sum(jnp.where(at_g, rows, 0), axis=1)
    i_of = d - pick(ds_e)
    valid = (i_of >= 0) & (i_of < pick(row_of(npc)))
    src = g_of * GROUP_PIECES + pick(row_of(strip_start)) + i_of
    src_piece = jnp.where(valid, src, 0).astype(jnp.int32)

    j = jnp.arange(GROUP_PIECES, dtype=jnp.int32)
    e_loc = jnp.maximum(jnp.sum(strip_start[:, None, :] <= j[None, :, None], axis=2) - 1, 0)
    at_e = e_loc[:, :, None] == jnp.arange(N_EXPERTS, dtype=jnp.int32)[None, None, :]
    pick_e = lambda tab: jnp.sum(jnp.where(at_e, tab[:, None, :], 0), axis=2)
    st_loc = pick_e(strip_start)
    np_loc = pick_e(npc)
    ds_loc = pick_e(dstart)
    i_loc = j[None, :] - st_loc
    ypiece = jnp.where(i_loc < np_loc, ds_loc + i_loc, 0).astype(jnp.int32).reshape(groups * GROUP_PIECES)
    return bexp, src_piece, nvb.reshape(1), ypiece


def _tables(seq):
    half = RET_QK_DIM // 2
    inv = ROPE_BASE ** (-jnp.arange(half, dtype=F32) / half)
    ang = jnp.arange(seq, dtype=jnp.int32).astype(F32)[:, None] * inv[None, :]
    cos, sin = jnp.cos(ang), jnp.sin(ang)
    cosf = jnp.concatenate([cos, cos], axis=1)
    sinf = jnp.concatenate([-sin, sin], axis=1)
    kscale = RET_QK_DIM ** -0.5
    log_g = jnp.log(1.0 - jnp.exp(jnp.linspace(math.log(1.0 / 32), math.log(1.0 / 512), RET_HEADS)))
    idx = jnp.arange(TOK_BLOCK, dtype=F32)
    diff = idx[:, None] - idx[None, :]
    dmat = jnp.where(diff >= 0, jnp.exp(log_g[:, None, None] * jnp.maximum(diff, 0.0)), 0.0)
    qdec = jnp.exp(log_g[:, None] * (idx[None, :] + 1.0))
    kdec = jnp.exp(log_g[:, None] * (TOK_BLOCK - 1.0 - idx[None, :]))
    bc = lambda a: jnp.broadcast_to(a[:, :, None], (RET_HEADS, TOK_BLOCK, RET_QK_DIM))
    log_g_host = np.log(1.0 - np.exp(np.linspace(math.log(1.0 / 32), math.log(1.0 / 512), RET_HEADS)))
    chunk_decay = tuple(float(np.float32(np.exp(np.float32(lg) * np.float32(TOK_BLOCK)))) for lg in log_g_host)
    return {"cq": cosf, "sq": sinf, "ck": cosf * kscale, "sk": sinf * kscale,
            "dmat": dmat.astype(F32), "qdec": bc(qdec), "kdec": bc(kdec), "chunk_decay": chunk_decay}


def kernel(x, w_in, p_ret, sg_ln_g, sg_ln_b, sg_w, sg_b, p_sg, w_o, ln1_g, ln1_b,
           router_w, router_b, e_w1, e_b1, e_w2, e_b2, ln2_g, ln2_b):
    batch, seq, d = x.shape
    depth = w_in.shape[0]
    assert d == D_MODEL and seq % TOK_BLOCK == 0
    t = batch * seq
    groups = t // TOK_BLOCK
    used_pieces = (TOK_BLOCK * TOP_K + N_EXPERTS * (PIECE - 1)) // PIECE
    n_blocks = (groups * used_pieces + N_EXPERTS * (EXP_PIECES - 1)) // EXP_PIECES
    n_blocks = -(-n_blocks // SUBLANES_V7X) * SUBLANES_V7X
    tables = _tables(seq)
    xc = x.reshape(t, D_MODEL)
    w_in_b, p_ret_b, p_sg_b, w_o_b = (w.astype(BF16) for w in (w_in, p_ret, p_sg, w_o))
    e_b1r = e_b1.reshape(depth, N_EXPERTS, 1, 2 * D_EXPERT)
    e_b2r = e_b2.reshape(depth, N_EXPERTS, 1, D_MODEL)
    prev_moe = None
    for l in range(depth):
        lw = {
            "w_in": w_in_b, "p_ret": p_ret_b, "p_sg": p_sg_b,
            "w_o": w_o_b, "sg_w": sg_w[l], "sg_bt": sg_b[l].T,
            "sg_ln_g": sg_ln_g[l][None], "sg_ln_b": sg_ln_b[l][None],
            "ln1_g": ln1_g[l][None], "ln1_b": ln1_b[l][None],
            "router_wt": router_w[l].T,
            "router_bb": jnp.broadcast_to(router_b[l][:, None], (N_EXPERTS, TOK_BLOCK)),
        }
        x1, pos, gates, cnt, xs = _mixer_call(xc, batch, seq, l, lw, tables, prev_moe)
        bexp, src_piece, nvb, ypiece = _routing_tables(cnt[:, :, 0], n_blocks)
        yb = _expert_call(xs, bexp, src_piece, nvb, l, e_w1, e_b1r, e_w2, e_b2r, n_blocks)
        prev_moe = (yb, ypiece, pos, gates, ln2_g[l][None], ln2_b[l][None])
        xc = x1
    out = _combine_call(xc, *prev_moe)
    return out.reshape(batch, seq, D_MODEL)
```

```python
import functools
import math

import jax
import jax.numpy as jnp
import numpy as np
from jax import lax
from jax.experimental import pallas as pl
from jax.experimental.pallas import tpu as pltpu

F32 = jnp.float32
BF16 = jnp.bfloat16

D_MODEL = 1024
DEPTH = 4
RET_HEADS = 4
RET_QK_DIM = 128
RET_V_DIM = 256
RET_QK = RET_HEADS * RET_QK_DIM
RET_V = RET_HEADS * RET_V_DIM
ROPE_BASE = 10000.0
SG_BLOCK = 128
SG_CHUNK = 64
SG_GROUPS = 8
SG_WIDTH = D_MODEL
N_EXPERTS = 32
TOP_K = 4
D_EXPERT = D_MODEL
SWIGLU_LIMIT = 7.0
SWIGLU_ALPHA = 1.702
LN_EPS = 1e-5
DN_ALPHA = (2 * DEPTH) ** 0.25

OFF_Q = 0
OFF_K = OFF_Q + RET_QK
OFF_V = OFF_K + RET_QK
OFF_G = OFF_V + RET_V
OFF_U = OFF_G + RET_V
OFF_VS = OFF_U + SG_WIDTH
OFF_GA = OFF_VS + SG_WIDTH
OFF_GB = OFF_GA + D_MODEL
IN_WIDTH = OFF_GB + D_MODEL

SUBLANES_V7X = 8
LANES_V7X = 128
TOK_BLOCK = 256
PIECE = SUBLANES_V7X
GROUP_ROWS = TOK_BLOCK * TOP_K + N_EXPERTS * PIECE
GROUP_PIECES = GROUP_ROWS // PIECE
SORT_ROWS = 256
ROUTE_COLS = SUBLANES_V7X
EXP_BLOCK = 512
EXP_PIECES = EXP_BLOCK // PIECE
F_CHUNK = 512
W_CAST_ROWS = 128
VMEM_LIMIT_V7X = 60 * 1024 * 1024


def _layer_norm(x, g, b):
    mu = jnp.mean(x, axis=-1, keepdims=True)
    xc = x - mu
    var = jnp.mean(xc * xc, axis=-1, keepdims=True)
    return xc * lax.rsqrt(var + LN_EPS) * g + b


def _gelu_tanh(x):
    c = math.sqrt(2.0 / math.pi)
    return 0.5 * x * (1.0 + jnp.tanh(c * (x + 0.044715 * (x * x * x))))


def _dot(a, b):
    return jnp.dot(a, b, preferred_element_type=F32)


def _dot_nt(a, b):
    return lax.dot_general(a, b, (((1,), (1,)), ((), ())), preferred_element_type=F32)


def _dot_tn(a, b):
    return lax.dot_general(a, b, (((0,), (0,)), ((), ())), preferred_element_type=F32)


def _split_bf16(x):
    hi = x.astype(BF16)
    lo = (x - hi.astype(F32)).astype(BF16)
    return hi, lo


def _unsort_slab(post, gatet, r0):
    c_iota = lax.broadcasted_iota(jnp.int32, (TOK_BLOCK, SORT_ROWS), 1) + r0
    unsort = jnp.zeros((TOK_BLOCK, SORT_ROWS), F32)
    for k in range(TOP_K):
        unsort = jnp.where(c_iota == post[:, k:k + 1], gatet[:, k:k + 1], unsort)
    return unsort.astype(BF16)


def _unsort_slabs(post, gatet):
    return [_unsort_slab(post, gatet, r0) for r0 in range(0, GROUP_ROWS, SORT_ROWS)]


def _unsort_matmul(slabs, ybuf):
    ffn = jnp.zeros((TOK_BLOCK, D_MODEL), F32)
    for i, r0 in enumerate(range(0, GROUP_ROWS, SORT_ROWS)):
        ffn = ffn + _dot(slabs[i], ybuf[r0:r0 + SORT_ROWS, :].astype(BF16))
    return ffn


def _mixer_kernel(yp_ref, *refs, chunk_decay, steps_per_seq, n_groups, fused):
    if fused:
        x_ref, yb_hbm, post_ref, gatet_ref, ln2g_ref, ln2b_ref = refs[:6]
        refs = refs[6:]
        ybuf, sem = refs[-2:]
        refs = refs[:-2]
    else:
        x_ref = refs[0]
        refs = refs[1:]
    (w_in_ref, p_ret_ref, p_sg_ref, w_o_ref, sgw_ref, sgbt_ref,
     sglng_ref, sglnb_ref, ln1g_ref, ln1b_ref, rwt_ref, rbb_ref,
     cq_ref, sq_ref, ck_ref, sk_ref, dmat_ref, qdec_ref, kdec_ref,
     x1_ref, pos_ref, gate_ref, cnt_ref, xs_ref,
     state_ref, ysg_ref, x1_prev_ref) = refs
    ts = TOK_BLOCK
    step = pl.program_id(0)

    @pl.when(step == 0)
    def _():
        x1_prev_ref[...] = jnp.zeros_like(x1_prev_ref)

    @pl.when(step % steps_per_seq == 0)
    def _():
        state_ref[...] = jnp.zeros_like(state_ref)

    if fused:
        slot = step % 2

        def gather(grp, s):
            for i in range(GROUP_PIECES):
                row = pl.multiple_of(yp_ref[grp * GROUP_PIECES + i] * PIECE, PIECE)
                pltpu.make_async_copy(yb_hbm.at[pl.ds(row, PIECE), :],
                                      ybuf.at[s, pl.ds(i * PIECE, PIECE), :], sem.at[s]).start()

        def wait(s):
            pltpu.make_async_copy(yb_hbm.at[pl.ds(0, GROUP_ROWS), :], ybuf.at[s], sem.at[s]).wait()

        @pl.when(step == 0)
        def _():
            gather(0, 0)

    routing = _route_and_sort(x1_prev_ref[...], rwt_ref, rbb_ref, pos_ref, gate_ref, cnt_ref, xs_ref)
    routing_stage = lambda: next(routing, None)

    if fused:
        slabs = _unsort_slabs(post_ref[...], gatet_ref[...])
        wait(slot)
        ffn = _unsort_matmul(slabs, ybuf.at[slot])
        xb = _layer_norm(DN_ALPHA * x_ref[...] + ffn, ln2g_ref[...], ln2b_ref[...])
        gather(jnp.minimum(step + 1, n_groups - 1), 1 - slot)
    else:
        xb = x_ref[...]
    xbf = xb.astype(BF16)

    def proj(off, width):
        return _dot(xbf, w_in_ref[0, :, off:off + width])

    cq, sq, ck, sk = cq_ref[...], sq_ref[...], ck_ref[...], sk_ref[...]
    acc_ret = jnp.zeros((ts, D_MODEL), F32)
    pair, head_in = {}, {}

    def project_head(h):
        if h % 2 == 0:
            pair["q"] = proj(OFF_Q + h * RET_QK_DIM, 2 * RET_QK_DIM)
            yield
            pair["k"] = proj(OFF_K + h * RET_QK_DIM, 2 * RET_QK_DIM)
            yield
        lanes = slice((h % 2) * RET_QK_DIM, (h % 2 + 1) * RET_QK_DIM)
        v = proj(OFF_V + h * RET_V_DIM, RET_V_DIM).astype(BF16)
        yield
        g = proj(OFF_G + h * RET_V_DIM, RET_V_DIM)
        head_in[h] = (pair["q"][:, lanes], pair["k"][:, lanes], v, g)
        yield

    for _ in project_head(0):
        pass
    for h in range(RET_HEADS):
        upcoming = project_head(h + 1) if h + 1 < RET_HEADS else iter(())
        q, k, v, g = head_in.pop(h)
        routing_stage()
        next(upcoming, None)
        qr = q * cq + pltpu.roll(q, RET_QK_DIM // 2, 1) * sq
        kr = k * ck + pltpu.roll(k, RET_QK_DIM // 2, 1) * sk
        scores = _dot_nt(qr.astype(BF16), kr.astype(BF16)) * dmat_ref[h]
        next(upcoming, None)
        inner = _dot(scores.astype(BF16), v)
        st = state_ref[h]
        cross = _dot((qr * qdec_ref[h]).astype(BF16), st.astype(BF16))
        kd = (kr * kdec_ref[h]).astype(BF16)
        state_ref[h] = st * chunk_decay[h] + _dot_tn(kd, v)
        routing_stage()
        next(upcoming, None)
        y = inner + cross
        mu = jnp.mean(y, axis=-1, keepdims=True)
        yc = y - mu
        var = jnp.mean(yc * yc, axis=-1, keepdims=True)
        yn = yc * lax.rsqrt(var + LN_EPS)
        next(upcoming, None)
        yr = yn * (g * jax.nn.sigmoid(g))
        acc_ret = acc_ret + _dot(yr.astype(BF16), p_ret_ref[0, h * RET_V_DIM:(h + 1) * RET_V_DIM, :])
        for _ in upcoming:
            pass
        if fused and h == 1:
            @pl.when(step == n_groups)
            def _():
                wait(1 - slot)

    pre_u = proj(OFF_U, SG_WIDTH)
    pre_vs = proj(OFF_VS, SG_WIDTH)
    ga = proj(OFF_GA, D_MODEL)
    u = _gelu_tanh(pre_u)
    routing_stage()
    gb = proj(OFF_GB, D_MODEL)
    vn = _layer_norm(_gelu_tanh(pre_vs), sglng_ref[...], sglnb_ref[...])
    qi = lax.broadcasted_iota(jnp.int32, (SG_BLOCK, SG_BLOCK), 0) // SG_CHUNK
    pi = lax.broadcasted_iota(jnp.int32, (SG_BLOCK, SG_BLOCK), 1) // SG_CHUNK
    sg_mask = qi >= pi
    for gi in range(SG_GROUPS):
        wm = jnp.where(sg_mask, sgw_ref[gi], 0.0).astype(BF16)
        bias = sgbt_ref[:, gi:gi + 1]
        cols = slice(gi * SG_BLOCK, (gi + 1) * SG_BLOCK)
        blocks = [slice(blk * SG_BLOCK, (blk + 1) * SG_BLOCK) for blk in range(ts // SG_BLOCK)]
        vcat = jnp.concatenate([vn[rows, cols] for rows in blocks], axis=1).astype(BF16)
        mixed = _dot(wm, vcat) + bias
        for blk, rows in enumerate(blocks):
            ysg_ref[rows, cols] = (u[rows, cols] * mixed[:, blk * SG_BLOCK:(blk + 1) * SG_BLOCK]).astype(BF16)
    routing_stage()
    acc_sg = _dot(ysg_ref[...], p_sg_ref[0])

    hmix = jax.nn.sigmoid(ga) * acc_ret + jax.nn.sigmoid(gb) * acc_sg
    mix = _dot(hmix.astype(BF16), w_o_ref[0])
    for _ in routing:
        pass
    x1 = _layer_norm(DN_ALPHA * xb + mix, ln1g_ref[...], ln1b_ref[...])
    x1_ref[...] = x1
    x1_prev_ref[...] = x1


def _route_and_sort(x1, rwt_ref, rbb_ref, pos_ref, gate_ref, cnt_ref, xs_ref):
    ts = TOK_BLOCK
    x_hi, x_lo = _split_bf16(x1)
    w_hi, w_lo = _split_bf16(rwt_ref[...])
    logits = (_dot_nt(w_hi, x_hi) + _dot_nt(w_hi, x_lo) + _dot_nt(w_lo, x_hi)) + rbb_ref[...]
    yield
    e_iota = lax.broadcasted_iota(jnp.int32, (N_EXPERTS, ts), 0).astype(F32)
    sels, vals = [], []
    work = logits
    for _ in range(TOP_K):
        m = jnp.max(work, axis=0, keepdims=True)
        idx = jnp.min(jnp.where(work == m, e_iota, float(N_EXPERTS)), axis=0, keepdims=True)
        sel = e_iota == idx
        work = jnp.where(sel, -jnp.inf, work)
        sels.append(sel)
        vals.append(m)
        yield
    exps = [jnp.exp(v - vals[0]) for v in vals]
    denom = exps[0] + exps[1] + exps[2] + exps[3]
    pad_rows = jnp.zeros((ROUTE_COLS - TOP_K, ts), F32)
    gate_ref[...] = jnp.concatenate([e / denom for e in exps] + [pad_rows], axis=0).T

    onehots = [jnp.where(s, 1.0, 0.0) for s in sels]
    oh_sum = onehots[0] + onehots[1] + onehots[2] + onehots[3]
    t_row = lax.broadcasted_iota(jnp.int32, (ts, ts), 0)
    t_col = lax.broadcasted_iota(jnp.int32, (ts, ts), 1)
    upper = jnp.where(t_row < t_col, 1.0, 0.0).astype(BF16)
    before = _dot(oh_sum.astype(BF16), upper)
    cnt = jnp.sum(oh_sum, axis=1, keepdims=True)
    cnt_pad = jnp.floor((cnt + (PIECE - 1)) * (1.0 / PIECE)) * PIECE
    cnt_b = jnp.broadcast_to(cnt_pad, (N_EXPERTS, ts))
    e_row = lax.broadcasted_iota(jnp.int32, (N_EXPERTS, N_EXPERTS), 0)
    e_col = lax.broadcasted_iota(jnp.int32, (N_EXPERTS, N_EXPERTS), 1)
    lower = jnp.where(e_col < e_row, 1.0, 0.0).astype(BF16)
    strip_off = _dot(lower, cnt_b.astype(BF16))
    slot = strip_off + before
    pos_f = [jnp.sum(oh * slot, axis=0, keepdims=True) for oh in onehots]
    pos_ref[...] = jnp.concatenate(pos_f + [pad_rows], axis=0).T.astype(jnp.int32)
    pos = [p.astype(jnp.int32) for p in pos_f]
    cnt_ref[0] = cnt_b[:, :LANES_V7X].astype(jnp.int32)
    yield
    for r0 in range(0, GROUP_ROWS, SORT_ROWS):
        r_iota = lax.broadcasted_iota(jnp.int32, (SORT_ROWS, ts), 0) + r0
        perm = jnp.zeros((SORT_ROWS, ts), F32)
        for k in range(TOP_K):
            perm = jnp.where(r_iota == pos[k], 1.0, perm)
        perm = perm.astype(BF16)
        xs_ref[r0:r0 + SORT_ROWS, :] = _dot(perm, x_hi)
        yield


def _mixer_call(x2d, batch, seq, layer, lw, tables, prev_moe=None):
    t = batch * seq
    ns = seq // TOK_BLOCK
    groups = t // TOK_BLOCK
    fused = prev_moe is not None
    const = lambda *shape: pl.BlockSpec(shape, lambda i, yp: (0,) * len(shape),
                                        pipeline_mode=pl.Buffered(1))
    stacked = lambda *shape: pl.BlockSpec((1,) + shape, lambda i, yp: (layer, 0, 0),
                                          pipeline_mode=pl.Buffered(1))
    cur = lambda i: jnp.minimum(i, groups - 1)
    prev = lambda i: jnp.maximum(i - 1, 0)
    seqtab = pl.BlockSpec((TOK_BLOCK, RET_QK_DIM), lambda i, yp: (cur(i) % ns, 0))
    route_prev = pl.BlockSpec((TOK_BLOCK, ROUTE_COLS), lambda i, yp: (prev(i), 0))
    tok_cur = lambda width: pl.BlockSpec((TOK_BLOCK, width), lambda i, yp: (cur(i), 0))
    in_specs = [tok_cur(D_MODEL)]
    operands = [x2d]
    scratch = [pltpu.VMEM((RET_HEADS, RET_QK_DIM, RET_V_DIM), F32),
               pltpu.VMEM((TOK_BLOCK, SG_WIDTH), BF16),
               pltpu.VMEM((TOK_BLOCK, D_MODEL), F32)]
    if fused:
        yb, ypiece, post, gatet, ln2_g, ln2_b = prev_moe
        in_specs += [pl.BlockSpec(memory_space=pl.ANY), tok_cur(ROUTE_COLS), tok_cur(ROUTE_COLS),
                     const(1, D_MODEL), const(1, D_MODEL)]
        operands += [yb, post, gatet, ln2_g, ln2_b]
        scratch += [pltpu.VMEM((2, GROUP_ROWS, D_MODEL), F32), pltpu.SemaphoreType.DMA((2,))]
    else:
        ypiece = jnp.zeros((1,), jnp.int32)
    in_specs += [
        stacked(D_MODEL, IN_WIDTH), stacked(RET_V, D_MODEL), stacked(SG_WIDTH, D_MODEL),
        stacked(D_MODEL, D_MODEL),
        const(SG_GROUPS, SG_BLOCK, SG_BLOCK), const(SG_BLOCK, SG_GROUPS),
        const(1, SG_WIDTH), const(1, SG_WIDTH), const(1, D_MODEL), const(1, D_MODEL),
        const(N_EXPERTS, D_MODEL), const(N_EXPERTS, TOK_BLOCK),
        seqtab, seqtab, seqtab, seqtab,
        const(RET_HEADS, TOK_BLOCK, TOK_BLOCK), const(RET_HEADS, TOK_BLOCK, RET_QK_DIM),
        const(RET_HEADS, TOK_BLOCK, RET_QK_DIM),
    ]
    out_shape = [
        jax.ShapeDtypeStruct((t + TOK_BLOCK, D_MODEL), F32),
        jax.ShapeDtypeStruct((t, ROUTE_COLS), jnp.int32),
        jax.ShapeDtypeStruct((t, ROUTE_COLS), F32),
        jax.ShapeDtypeStruct((groups, N_EXPERTS, LANES_V7X), jnp.int32),
        jax.ShapeDtypeStruct((groups * GROUP_ROWS, D_MODEL), F32),
    ]
    out_specs = [
        pl.BlockSpec((TOK_BLOCK, D_MODEL), lambda i, yp: (i, 0)), route_prev, route_prev,
        pl.BlockSpec((1, N_EXPERTS, LANES_V7X), lambda i, yp: (prev(i), 0, 0)),
        pl.BlockSpec((GROUP_ROWS, D_MODEL), lambda i, yp: (prev(i), 0)),
    ]
    operands += [lw["w_in"], lw["p_ret"], lw["p_sg"], lw["w_o"], lw["sg_w"], lw["sg_bt"],
                 lw["sg_ln_g"], lw["sg_ln_b"], lw["ln1_g"], lw["ln1_b"], lw["router_wt"], lw["router_bb"],
                 tables["cq"], tables["sq"], tables["ck"], tables["sk"],
                 tables["dmat"], tables["qdec"], tables["kdec"]]
    kern = functools.partial(_mixer_kernel, chunk_decay=tables["chunk_decay"], steps_per_seq=ns,
                             n_groups=groups, fused=fused)
    grid_spec = pltpu.PrefetchScalarGridSpec(
        num_scalar_prefetch=1, grid=(groups + 1,), in_specs=in_specs, out_specs=out_specs,
        scratch_shapes=scratch)
    return pl.pallas_call(
        kern,
        grid_spec=grid_spec,
        out_shape=out_shape,
        compiler_params=pltpu.CompilerParams(dimension_semantics=("arbitrary",),
                                             vmem_limit_bytes=VMEM_LIMIT_V7X),
        name="mixer",
    )(ypiece, *operands)


def _expert_kernel(bexp_ref, src_ref, nvb_ref, xs_hbm, w1_ref, b1_ref, w2_ref, b2_ref,
                   yb_ref, xbuf, w1b, w2b, sem):
    b = pl.program_id(0)
    nvb = nvb_ref[0]

    def gather(blk, slot, pieces=range(EXP_PIECES)):
        for i in pieces:
            row = pl.multiple_of(src_ref[blk * EXP_PIECES + i] * PIECE, PIECE)
            pltpu.make_async_copy(xs_hbm.at[pl.ds(row, PIECE), :],
                                  xbuf.at[slot, pl.ds(i * PIECE, PIECE), :],
                                  sem.at[slot]).start()

    def wait(slot):
        pltpu.make_async_copy(xs_hbm.at[pl.ds(0, EXP_BLOCK), :], xbuf.at[slot], sem.at[slot]).wait()

    @pl.when(b == 0)
    def _():
        gather(0, 0)

    new_expert = (b == 0) | (bexp_ref[b] != bexp_ref[jnp.maximum(b - 1, 0)])

    @pl.when(new_expert & (b < nvb))
    def _():
        def cast_rows(i, carry):
            r = pl.multiple_of(i * W_CAST_ROWS, W_CAST_ROWS)
            w1b[pl.ds(r, W_CAST_ROWS), :] = w1_ref[0, 0, pl.ds(r, W_CAST_ROWS), :].astype(BF16)
            w2b[pl.ds(r, W_CAST_ROWS), :] = w2_ref[0, 0, pl.ds(r, W_CAST_ROWS), :].astype(BF16)
            return carry
        lax.fori_loop(0, D_MODEL // W_CAST_ROWS, cast_rows, 0)

    @pl.when(b < nvb)
    def _():
        slot = b % 2
        nxt = 1 - slot
        nblk = jnp.minimum(b + 1, nvb - 1)
        wait(slot)
        n_chunks = D_EXPERT // F_CHUNK
        first = EXP_PIECES // 8
        batches = [range(0, first), range(first, EXP_PIECES)] + [range(0)] * (n_chunks - 2)
        y = jnp.zeros((EXP_BLOCK, D_MODEL), F32)
        for j in range(n_chunks):
            c0 = j * F_CHUNK
            gather(nblk, nxt, batches[j])
            x = xbuf[slot].astype(BF16)
            hg = _dot(x, w1b[:, c0:c0 + F_CHUNK]) + b1_ref[0, 0, :, c0:c0 + F_CHUNK]
            hu = (_dot(x, w1b[:, D_EXPERT + c0:D_EXPERT + c0 + F_CHUNK])
                  + b1_ref[0, 0, :, D_EXPERT + c0:D_EXPERT + c0 + F_CHUNK])
            gate = jnp.minimum(hg, SWIGLU_LIMIT)
            up = jnp.clip(hu, -SWIGLU_LIMIT, SWIGLU_LIMIT)
            act = gate * jax.nn.sigmoid(SWIGLU_ALPHA * gate) * (up + 1.0)
            y = y + _dot(act.astype(BF16), w2b[c0:c0 + F_CHUNK, :])
        yb_ref[...] = y + b2_ref[0, 0]

        @pl.when(b == nvb - 1)
        def _():
            wait(nxt)

    @pl.when(b >= nvb)
    def _():
        yb_ref[...] = jnp.zeros_like(yb_ref)


def _expert_call(xs, block_exp, src_piece, nvb, layer, e_w1, e_b1, e_w2, e_b2, n_blocks):
    wspec = lambda *shape: pl.BlockSpec((1, 1) + shape, lambda b, be, sp, nv: (layer, be[b], 0, 0))
    grid_spec = pltpu.PrefetchScalarGridSpec(
        num_scalar_prefetch=3,
        grid=(n_blocks,),
        in_specs=[
            pl.BlockSpec(memory_space=pl.ANY),
            wspec(D_MODEL, 2 * D_EXPERT), wspec(1, 2 * D_EXPERT),
            wspec(D_EXPERT, D_MODEL), wspec(1, D_MODEL),
        ],
        out_specs=pl.BlockSpec((EXP_BLOCK, D_MODEL), lambda b, be, sp, nv: (b, 0)),
        scratch_shapes=[pltpu.VMEM((2, EXP_BLOCK, D_MODEL), F32),
                        pltpu.VMEM((D_MODEL, 2 * D_EXPERT), BF16),
                        pltpu.VMEM((D_EXPERT, D_MODEL), BF16),
                        pltpu.SemaphoreType.DMA((2,))],
    )
    return pl.pallas_call(
        _expert_kernel,
        grid_spec=grid_spec,
        out_shape=jax.ShapeDtypeStruct((n_blocks * EXP_BLOCK, D_MODEL), F32),
        compiler_params=pltpu.CompilerParams(dimension_semantics=("arbitrary",),
                                             vmem_limit_bytes=VMEM_LIMIT_V7X),
        name="experts",
    )(block_exp, src_piece, nvb, xs, e_w1, e_b1, e_w2, e_b2)


def _combine_kernel(yp_ref, x1_ref, yb_hbm, post_ref, gatet_ref, g_ref, b_ref, out_ref,
                    ybuf0, ybuf1, sem):
    g = pl.program_id(0)
    ng = pl.num_programs(0)
    ybufs = (ybuf0, ybuf1)

    def gather(grp, slot, pieces=range(GROUP_PIECES)):
        for i in pieces:
            row = pl.multiple_of(yp_ref[grp * GROUP_PIECES + i] * PIECE, PIECE)
            pltpu.make_async_copy(yb_hbm.at[pl.ds(row, PIECE), :],
                                  ybufs[slot].at[pl.ds(i * PIECE, PIECE), :],
                                  sem.at[slot]).start(priority=i % 2)

    def wait(slot):
        pltpu.make_async_copy(yb_hbm.at[pl.ds(0, GROUP_ROWS), :], ybufs[slot], sem.at[slot]).wait()

    @pl.when(g == 0)
    def _():
        gather(0, 0)

    def group(cur):
        nxt = 1 - cur
        gather(jnp.minimum(g + 1, ng - 1), nxt)
        slabs = _unsort_slabs(post_ref[...], gatet_ref[...])
        wait(cur)
        ffn = _unsort_matmul(slabs, ybufs[cur])
        out_ref[...] = _layer_norm(DN_ALPHA * x1_ref[...] + ffn, g_ref[...], b_ref[...])

        @pl.when(g == ng - 1)
        def _():
            wait(nxt)

    for cur in range(2):
        pl.when(g % 2 == cur)(functools.partial(group, cur))


def _combine_call(x1, yb, ypiece, post, gatet, ln_g, ln_b):
    t = post.shape[0]
    groups = t // TOK_BLOCK
    grid_spec = pltpu.PrefetchScalarGridSpec(
        num_scalar_prefetch=1,
        grid=(groups,),
        in_specs=[
            pl.BlockSpec((TOK_BLOCK, D_MODEL), lambda g, yp: (g, 0)),
            pl.BlockSpec(memory_space=pl.ANY),
            pl.BlockSpec((TOK_BLOCK, ROUTE_COLS), lambda g, yp: (g, 0)),
            pl.BlockSpec((TOK_BLOCK, ROUTE_COLS), lambda g, yp: (g, 0)),
            pl.BlockSpec((1, D_MODEL), lambda g, yp: (0, 0)),
            pl.BlockSpec((1, D_MODEL), lambda g, yp: (0, 0)),
        ],
        out_specs=pl.BlockSpec((TOK_BLOCK, D_MODEL), lambda g, yp: (g, 0)),
        scratch_shapes=[pltpu.VMEM((GROUP_ROWS, D_MODEL), F32), pltpu.VMEM((GROUP_ROWS, D_MODEL), F32),
                        pltpu.SemaphoreType.DMA((2,))],
    )
    return pl.pallas_call(
        _combine_kernel,
        grid_spec=grid_spec,
        out_shape=jax.ShapeDtypeStruct((t, D_MODEL), F32),
        compiler_params=pltpu.CompilerParams(dimension_semantics=("arbitrary",),
                                             vmem_limit_bytes=VMEM_LIMIT_V7X),
        name="combine",
    )(ypiece, x1, yb, post, gatet, ln_g, ln_b)


def _routing_tables(cnt_pad, n_blocks):
    groups = cnt_pad.shape[0]
    npc = cnt_pad // PIECE
    strip_start = jnp.cumsum(npc, axis=1) - npc
    tot = jnp.sum(npc, axis=0)
    tot_pad = (tot + EXP_PIECES - 1) // EXP_PIECES * EXP_PIECES
    exp_end = jnp.cumsum(tot_pad)
    exp_start = exp_end - tot_pad
    dstart = exp_start[None, :] + jnp.cumsum(npc, axis=0) - npc
    nvb = (exp_end[-1] // EXP_PIECES).astype(jnp.int32)

    blocks = jnp.arange(n_blocks, dtype=jnp.int32)
    bexp = jnp.sum(blocks[:, None] * EXP_PIECES >= exp_end[None, :], axis=1).astype(jnp.int32)
    bexp = jnp.minimum(bexp, N_EXPERTS - 1)
    last = jnp.sum((nvb - 1) * EXP_PIECES >= exp_end).astype(jnp.int32)
    bexp = jnp.where(blocks < nvb, bexp, jnp.minimum(last, N_EXPERTS - 1))

    d = jnp.arange(n_blocks * EXP_PIECES, dtype=jnp.int32)
    e_of = jnp.minimum(jnp.sum(d[:, None] >= exp_end[None, :], axis=1), N_EXPERTS - 1)
    oh_e = (e_of[:, None] == jnp.arange(N_EXPERTS, dtype=jnp.int32)[None, :]).astype(F32)
    row_of = lambda tab: jnp.dot(oh_e, tab.T.astype(F32), precision=lax.Precision.HIGHEST).astype(jnp.int32)
    ds_e = row_of(dstart)
    g_of = jnp.maximum(jnp.sum(ds_e <= d[:, None], axis=1) - 1, 0)
    at_g = g_of[:, None] == jnp.arange(groups, dtype=jnp.int32)[None, :]
    pick = lambda rows: jnp.sum(jnp.where(at_g, rows, 0), axis=1)
    i_of = d - pick(ds_e)
    valid = (i_of >= 0) & (i_of < pick(row_of(npc)))
    src = g_of * GROUP_PIECES + pick(row_of(strip_start)) + i_of
    src_piece = jnp.where(valid, src, 0).astype(jnp.int32)

    j = jnp.arange(GROUP_PIECES, dtype=jnp.int32)
    e_loc = jnp.maximum(jnp.sum(strip_start[:, None, :] <= j[None, :, None], axis=2) - 1, 0)
    at_e = e_loc[:, :, None] == jnp.arange(N_EXPERTS, dtype=jnp.int32)[None, None, :]
    pick_e = lambda tab: jnp.sum(jnp.where(at_e, tab[:, None, :], 0), axis=2)
    st_loc = pick_e(strip_start)
    np_loc = pick_e(npc)
    ds_loc = pick_e(dstart)
    i_loc = j[None, :] - st_loc
    ypiece = jnp.where(i_loc < np_loc, ds_loc + i_loc, 0).astype(jnp.int32).reshape(groups * GROUP_PIECES)
    return bexp, src_piece, nvb.reshape(1), ypiece


def _tables(seq):
    half = RET_QK_DIM // 2
    inv = ROPE_BASE ** (-jnp.arange(half, dtype=F32) / half)
    ang = jnp.arange(seq, dtype=jnp.int32).astype(F32)[:, None] * inv[None, :]
    cos, sin = jnp.cos(ang), jnp.sin(ang)
    cosf = jnp.concatenate([cos, cos], axis=1)
    sinf = jnp.concatenate([-sin, sin], axis=1)
    kscale = RET_QK_DIM ** -0.5
    log_g = jnp.log(1.0 - jnp.exp(jnp.linspace(math.log(1.0 / 32), math.log(1.0 / 512), RET_HEADS)))
    idx = jnp.arange(TOK_BLOCK, dtype=F32)
    diff = idx[:, None] - idx[None, :]
    dmat = jnp.where(diff >= 0, jnp.exp(log_g[:, None, None] * jnp.maximum(diff, 0.0)), 0.0)
    qdec = jnp.exp(log_g[:, None] * (idx[None, :] + 1.0))
    kdec = jnp.exp(log_g[:, None] * (TOK_BLOCK - 1.0 - idx[None, :]))
    bc = lambda a: jnp.broadcast_to(a[:, :, None], (RET_HEADS, TOK_BLOCK, RET_QK_DIM))
    log_g_host = np.log(1.0 - np.exp(np.linspace(math.log(1.0 / 32), math.log(1.0 / 512), RET_HEADS)))
    chunk_decay = tuple(float(np.float32(np.exp(np.float32(lg) * np.float32(TOK_BLOCK)))) for lg in log_g_host)
    return {"cq": cosf, "sq": sinf, "ck": cosf * kscale, "sk": sinf * kscale,
            "dmat": dmat.astype(F32), "qdec": bc(qdec), "kdec": bc(kdec), "chunk_decay": chunk_decay}


def kernel(x, w_in, p_ret, sg_ln_g, sg_ln_b, sg_w, sg_b, p_sg, w_o, ln1_g, ln1_b,
           router_w, router_b, e_w1, e_b1, e_w2, e_b2, ln2_g, ln2_b):
    batch, seq, d = x.shape
    depth = w_in.shape[0]
    assert d == D_MODEL and seq % TOK_BLOCK == 0
    t = batch * seq
    groups = t // TOK_BLOCK
    used_pieces = (TOK_BLOCK * TOP_K + N_EXPERTS * (PIECE - 1)) // PIECE
    n_blocks = (groups * used_pieces + N_EXPERTS * (EXP_PIECES - 1)) // EXP_PIECES
    n_blocks = -(-n_blocks // SUBLANES_V7X) * SUBLANES_V7X
    tables = _tables(seq)
    xc = x.reshape(t, D_MODEL)
    w_in_b, p_ret_b, p_sg_b, w_o_b = (w.astype(BF16) for w in (w_in, p_ret, p_sg, w_o))
    e_b1r = e_b1.reshape(depth, N_EXPERTS, 1, 2 * D_EXPERT)
    e_b2r = e_b2.reshape(depth, N_EXPERTS, 1, D_MODEL)
    prev_moe = None
    for l in range(depth):
        lw = {
            "w_in": w_in_b, "p_ret": p_ret_b, "p_sg": p_sg_b,
            "w_o": w_o_b, "sg_w": sg_w[l], "sg_bt": sg_b[l].T,
            "sg_ln_g": sg_ln_g[l][None], "sg_ln_b": sg_ln_b[l][None],
            "ln1_g": ln1_g[l][None], "ln1_b": ln1_b[l][None],
            "router_wt": router_w[l].T,
            "router_bb": jnp.broadcast_to(router_b[l][:, None], (N_EXPERTS, TOK_BLOCK)),
        }
        x1, pos, gates, cnt, xs = _mixer_call(xc, batch, seq, l, lw, tables, prev_moe)
        bexp, src_piece, nvb, ypiece = _routing_tables(cnt[:, :, 0], n_blocks)
        yb = _expert_call(xs, bexp, src_piece, nvb, l, e_w1, e_b1r, e_w2, e_b2r, n_blocks)
        prev_moe = (yb, ypiece, pos, gates, ln2_g[l][None], ln2_b[l][None])
        xc = x1
    out = _combine_call(xc, *prev_moe)
    return out.reshape(batch, seq, D_MODEL)
```

```python
import functools
import math

import jax
import jax.numpy as jnp
import numpy as np
from jax import lax
from jax.experimental import pallas as pl
from jax.experimental.pallas import tpu as pltpu

F32 = jnp.float32
BF16 = jnp.bfloat16

D_MODEL = 1024
DEPTH = 4
RET_HEADS = 4
RET_QK_DIM = 128
RET_V_DIM = 256
RET_QK = RET_HEADS * RET_QK_DIM
RET_V = RET_HEADS * RET_V_DIM
ROPE_BASE = 10000.0
SG_BLOCK = 128
SG_CHUNK = 64
SG_GROUPS = 8
SG_WIDTH = D_MODEL
N_EXPERTS = 32
TOP_K = 4
D_EXPERT = D_MODEL
SWIGLU_LIMIT = 7.0
SWIGLU_ALPHA = 1.702
LN_EPS = 1e-5
DN_ALPHA = (2 * DEPTH) ** 0.25

OFF_Q = 0
OFF_K = OFF_Q + RET_QK
OFF_V = OFF_K + RET_QK
OFF_G = OFF_V + RET_V
OFF_U = OFF_G + RET_V
OFF_VS = OFF_U + SG_WIDTH
OFF_GA = OFF_VS + SG_WIDTH
OFF_GB = OFF_GA + D_MODEL
IN_WIDTH = OFF_GB + D_MODEL

SUBLANES_V7X = 8
LANES_V7X = 128
TOK_BLOCK = 256
PIECE = SUBLANES_V7X
GROUP_ROWS = TOK_BLOCK * TOP_K + N_EXPERTS * PIECE
GROUP_PIECES = GROUP_ROWS // PIECE
SORT_ROWS = 256
ROUTE_COLS = SUBLANES_V7X
EXP_BLOCK = 512
EXP_PIECES = EXP_BLOCK // PIECE
F_CHUNK = 512
W_CAST_ROWS = 128
VMEM_LIMIT_V7X = 60 * 1024 * 1024


def _layer_norm(x, g, b):
    mu = jnp.mean(x, axis=-1, keepdims=True)
    xc = x - mu
    var = jnp.mean(xc * xc, axis=-1, keepdims=True)
    return xc * lax.rsqrt(var + LN_EPS) * g + b


def _gelu_tanh(x):
    c = math.sqrt(2.0 / math.pi)
    return 0.5 * x * (1.0 + jnp.tanh(c * (x + 0.044715 * (x * x * x))))


def _dot(a, b):
    return jnp.dot(a, b, preferred_element_type=F32)


def _dot_nt(a, b):
    return lax.dot_general(a, b, (((1,), (1,)), ((), ())), preferred_element_type=F32)


def _dot_tn(a, b):
    return lax.dot_general(a, b, (((0,), (0,)), ((), ())), preferred_element_type=F32)


def _split_bf16(x):
    hi = x.astype(BF16)
    lo = (x - hi.astype(F32)).astype(BF16)
    return hi, lo


def _unsort_slab(post, gatet, r0):
    c_iota = lax.broadcasted_iota(jnp.int32, (TOK_BLOCK, SORT_ROWS), 1) + r0
    unsort = jnp.zeros((TOK_BLOCK, SORT_ROWS), F32)
    for k in range(TOP_K):
        unsort = jnp.where(c_iota == post[:, k:k + 1], gatet[:, k:k + 1], unsort)
    return unsort.astype(BF16)


def _unsort_slabs(post, gatet):
    return [_unsort_slab(post, gatet, r0) for r0 in range(0, GROUP_ROWS, SORT_ROWS)]


def _unsort_matmul(slabs, ybuf):
    ffn = jnp.zeros((TOK_BLOCK, D_MODEL), F32)
    for i, r0 in enumerate(range(0, GROUP_ROWS, SORT_ROWS)):
        ffn = ffn + _dot(slabs[i], ybuf[r0:r0 + SORT_ROWS, :].astype(BF16))
    return ffn


def _mixer_kernel(yp_ref, *refs, chunk_decay, steps_per_seq, n_groups, fused):
    if fused:
        x_ref, yb_hbm, post_ref, gatet_ref, ln2g_ref, ln2b_ref = refs[:6]
        refs = refs[6:]
        ybuf, sem = refs[-2:]
        refs = refs[:-2]
    else:
        x_ref = refs[0]
        refs = refs[1:]
    (w_in_ref, p_ret_ref, p_sg_ref, w_o_ref, sgw_ref, sgbt_ref,
     sglng_ref, sglnb_ref, ln1g_ref, ln1b_ref, rwt_ref, rbb_ref,
     cq_ref, sq_ref, ck_ref, sk_ref, dmat_ref, qdec_ref, kdec_ref,
     x1_ref, pos_ref, gate_ref, cnt_ref, xs_ref,
     state_ref, ysg_ref, x1_prev_ref) = refs
    ts = TOK_BLOCK
    step = pl.program_id(0)

    @pl.when(step == 0)
    def _():
        x1_prev_ref[...] = jnp.zeros_like(x1_prev_ref)

    @pl.when(step % steps_per_seq == 0)
    def _():
        state_ref[...] = jnp.zeros_like(state_ref)

    if fused:
        slot = step % 2

        def gather(grp, s):
            for i in range(GROUP_PIECES):
                row = pl.multiple_of(yp_ref[grp * GROUP_PIECES + i] * PIECE, PIECE)
                pltpu.make_async_copy(yb_hbm.at[pl.ds(row, PIECE), :],
                                      ybuf.at[s, pl.ds(i * PIECE, PIECE), :], sem.at[s]).start()

        def wait(s):
            pltpu.make_async_copy(yb_hbm.at[pl.ds(0, GROUP_ROWS), :], ybuf.at[s], sem.at[s]).wait()

        @pl.when(step == 0)
        def _():
            gather(0, 0)

    routing = _route_and_sort(x1_prev_ref[...], rwt_ref, rbb_ref, pos_ref, gate_ref, cnt_ref, xs_ref)
    routing_stage = lambda: next(routing, None)

    if fused:
        slabs = _unsort_slabs(post_ref[...], gatet_ref[...])
        wait(slot)
        ffn = _unsort_matmul(slabs, ybuf.at[slot])
        xb = _layer_norm(DN_ALPHA * x_ref[...] + ffn, ln2g_ref[...], ln2b_ref[...])
        gather(jnp.minimum(step + 1, n_groups - 1), 1 - slot)
    else:
        xb = x_ref[...]
    xbf = xb.astype(BF16)

    def proj(off, width):
        return _dot(xbf, w_in_ref[0, :, off:off + width])

    cq, sq, ck, sk = cq_ref[...], sq_ref[...], ck_ref[...], sk_ref[...]
    acc_ret = jnp.zeros((ts, D_MODEL), F32)
    pair, head_in = {}, {}

    def project_head(h):
        if h % 2 == 0:
            pair["q"] = proj(OFF_Q + h * RET_QK_DIM, 2 * RET_QK_DIM)
            yield
            pair["k"] = proj(OFF_K + h * RET_QK_DIM, 2 * RET_QK_DIM)
            yield
        lanes = slice((h % 2) * RET_QK_DIM, (h % 2 + 1) * RET_QK_DIM)
        v = proj(OFF_V + h * RET_V_DIM, RET_V_DIM).astype(BF16)
        yield
        g = proj(OFF_G + h * RET_V_DIM, RET_V_DIM)
        head_in[h] = (pair["q"][:, lanes], pair["k"][:, lanes], v, g)
        yield

    for _ in project_head(0):
        pass
    for h in range(RET_HEADS):
        upcoming = project_head(h + 1) if h + 1 < RET_HEADS else iter(())
        q, k, v, g = head_in.pop(h)
        routing_stage()
        next(upcoming, None)
        qr = q * cq + pltpu.roll(q, RET_QK_DIM // 2, 1) * sq
        kr = k * ck + pltpu.roll(k, RET_QK_DIM // 2, 1) * sk
        scores = _dot_nt(qr.astype(BF16), kr.astype(BF16)) * dmat_ref[h]
        next(upcoming, None)
        inner = _dot(scores.astype(BF16), v)
        st = state_ref[h]
        cross = _dot((qr * qdec_ref[h]).astype(BF16), st.astype(BF16))
        kd = (kr * kdec_ref[h]).astype(BF16)
        state_ref[h] = st * chunk_decay[h] + _dot_tn(kd, v)
        routing_stage()
        next(upcoming, None)
        y = inner + cross
        mu = jnp.mean(y, axis=-1, keepdims=True)
        yc = y - mu
        var = jnp.mean(yc * yc, axis=-1, keepdims=True)
        yn = yc * lax.rsqrt(var + LN_EPS)
        next(upcoming, None)
        yr = yn * (g * jax.nn.sigmoid(g))
        acc_ret = acc_ret + _dot(yr.astype(BF16), p_ret_ref[0, h * RET_V_DIM:(h + 1) * RET_V_DIM, :])
        for _ in upcoming:
            pass
        if fused and h == 1:
            @pl.when(step == n_groups)
            def _():
                wait(1 - slot)

    pre_u = proj(OFF_U, SG_WIDTH)
    pre_vs = proj(OFF_VS, SG_WIDTH)
    ga = proj(OFF_GA, D_MODEL)
    u = _gelu_tanh(pre_u)
    routing_stage()
    gb = proj(OFF_GB, D_MODEL)
    vn = _layer_norm(_gelu_tanh(pre_vs), sglng_ref[...], sglnb_ref[...])
    qi = lax.broadcasted_iota(jnp.int32, (SG_BLOCK, SG_BLOCK), 0) // SG_CHUNK
    pi = lax.broadcasted_iota(jnp.int32, (SG_BLOCK, SG_BLOCK), 1) // SG_CHUNK
    sg_mask = qi >= pi
    for gi in range(SG_GROUPS):
        wm = jnp.where(sg_mask, sgw_ref[gi], 0.0).astype(BF16)
        bias = sgbt_ref[:, gi:gi + 1]
        cols = slice(gi * SG_BLOCK, (gi + 1) * SG_BLOCK)
        blocks = [slice(blk * SG_BLOCK, (blk + 1) * SG_BLOCK) for blk in range(ts // SG_BLOCK)]
        vcat = jnp.concatenate([vn[rows, cols] for rows in blocks], axis=1).astype(BF16)
        mixed = _dot(wm, vcat) + bias
        for blk, rows in enumerate(blocks):
            ysg_ref[rows, cols] = (u[rows, cols] * mixed[:, blk * SG_BLOCK:(blk + 1) * SG_BLOCK]).astype(BF16)
    routing_stage()
    acc_sg = _dot(ysg_ref[...], p_sg_ref[0])

    hmix = jax.nn.sigmoid(ga) * acc_ret + jax.nn.sigmoid(gb) * acc_sg
    mix = _dot(hmix.astype(BF16), w_o_ref[0])
    for _ in routing:
        pass
    x1 = _layer_norm(DN_ALPHA * xb + mix, ln1g_ref[...], ln1b_ref[...])
    x1_ref[...] = x1
    x1_prev_ref[...] = x1


def _route_and_sort(x1, rwt_ref, rbb_ref, pos_ref, gate_ref, cnt_ref, xs_ref):
    ts = TOK_BLOCK
    x_hi, x_lo = _split_bf16(x1)
    w_hi, w_lo = _split_bf16(rwt_ref[...])
    logits = (_dot_nt(w_hi, x_hi) + _dot_nt(w_hi, x_lo) + _dot_nt(w_lo, x_hi)) + rbb_ref[...]
    yield
    e_iota = lax.broadcasted_iota(jnp.int32, (N_EXPERTS, ts), 0).astype(F32)
    sels, vals = [], []
    work = logits
    for _ in range(TOP_K):
        m = jnp.max(work, axis=0, keepdims=True)
        idx = jnp.min(jnp.where(work == m, e_iota, float(N_EXPERTS)), axis=0, keepdims=True)
        sel = e_iota == idx
        work = jnp.where(sel, -jnp.inf, work)
        sels.append(sel)
        vals.append(m)
        yield
    exps = [jnp.exp(v - vals[0]) for v in vals]
    denom = exps[0] + exps[1] + exps[2] + exps[3]
    pad_rows = jnp.zeros((ROUTE_COLS - TOP_K, ts), F32)
    gate_ref[...] = jnp.concatenate([e / denom for e in exps] + [pad_rows], axis=0).T

    onehots = [jnp.where(s, 1.0, 0.0) for s in sels]
    oh_sum = onehots[0] + onehots[1] + onehots[2] + onehots[3]
    t_row = lax.broadcasted_iota(jnp.int32, (ts, ts), 0)
    t_col = lax.broadcasted_iota(jnp.int32, (ts, ts), 1)
    upper = jnp.where(t_row < t_col, 1.0, 0.0).astype(BF16)
    before = _dot(oh_sum.astype(BF16), upper)
    cnt = jnp.sum(oh_sum, axis=1, keepdims=True)
    cnt_pad = jnp.floor((cnt + (PIECE - 1)) * (1.0 / PIECE)) * PIECE
    cnt_b = jnp.broadcast_to(cnt_pad, (N_EXPERTS, ts))
    e_row = lax.broadcasted_iota(jnp.int32, (N_EXPERTS, N_EXPERTS), 0)
    e_col = lax.broadcasted_iota(jnp.int32, (N_EXPERTS, N_EXPERTS), 1)
    lower = jnp.where(e_col < e_row, 1.0, 0.0).astype(BF16)
    strip_off = _dot(lower, cnt_b.astype(BF16))
    slot = strip_off + before
    pos_f = [jnp.sum(oh * slot, axis=0, keepdims=True) for oh in onehots]
    pos_ref[...] = jnp.concatenate(pos_f + [pad_rows], axis=0).T.astype(jnp.int32)
    pos = [p.astype(jnp.int32) for p in pos_f]
    cnt_ref[0] = cnt_b[:, :LANES_V7X].astype(jnp.int32)
    yield
    for r0 in range(0, GROUP_ROWS, SORT_ROWS):
        r_iota = lax.broadcasted_iota(jnp.int32, (SORT_ROWS, ts), 0) + r0
        perm = jnp.zeros((SORT_ROWS, ts), F32)
        for k in range(TOP_K):
            perm = jnp.where(r_iota == pos[k], 1.0, perm)
        perm = perm.astype(BF16)
        xs_ref[r0:r0 + SORT_ROWS, :] = _dot(perm, x_hi)
        yield


def _mixer_call(x2d, batch, seq, layer, lw, tables, prev_moe=None):
    t = batch * seq
    ns = seq // TOK_BLOCK
    groups = t // TOK_BLOCK
    fused = prev_moe is not None
    const = lambda *shape: pl.BlockSpec(shape, lambda i, yp: (0,) * len(shape),
                                        pipeline_mode=pl.Buffered(1))
    stacked = lambda *shape: pl.BlockSpec((1,) + shape, lambda i, yp: (layer, 0, 0),
                                          pipeline_mode=pl.Buffered(1))
    cur = lambda i: jnp.minimum(i, groups - 1)
    prev = lambda i: jnp.maximum(i - 1, 0)
    seqtab = pl.BlockSpec((TOK_BLOCK, RET_QK_DIM), lambda i, yp: (cur(i) % ns, 0))
    route_prev = pl.BlockSpec((TOK_BLOCK, ROUTE_COLS), lambda i, yp: (prev(i), 0))
    tok_cur = lambda width: pl.BlockSpec((TOK_BLOCK, width), lambda i, yp: (cur(i), 0))
    in_specs = [tok_cur(D_MODEL)]
    operands = [x2d]
    scratch = [pltpu.VMEM((RET_HEADS, RET_QK_DIM, RET_V_DIM), F32),
               pltpu.VMEM((TOK_BLOCK, SG_WIDTH), BF16),
               pltpu.VMEM((TOK_BLOCK, D_MODEL), F32)]
    if fused:
        yb, ypiece, post, gatet, ln2_g, ln2_b = prev_moe
        in_specs += [pl.BlockSpec(memory_space=pl.ANY), tok_cur(ROUTE_COLS), tok_cur(ROUTE_COLS),
                     const(1, D_MODEL), const(1, D_MODEL)]
        operands += [yb, post, gatet, ln2_g, ln2_b]
        scratch += [pltpu.VMEM((2, GROUP_ROWS, D_MODEL), F32), pltpu.SemaphoreType.DMA((2,))]
    else:
        ypiece = jnp.zeros((1,), jnp.int32)
    in_specs += [
        stacked(D_MODEL, IN_WIDTH), stacked(RET_V, D_MODEL), stacked(SG_WIDTH, D_MODEL),
        stacked(D_MODEL, D_MODEL),
        const(SG_GROUPS, SG_BLOCK, SG_BLOCK), const(SG_BLOCK, SG_GROUPS),
        const(1, SG_WIDTH), const(1, SG_WIDTH), const(1, D_MODEL), const(1, D_MODEL),
        const(N_EXPERTS, D_MODEL), const(N_EXPERTS, TOK_BLOCK),
        seqtab, seqtab, seqtab, seqtab,
        const(RET_HEADS, TOK_BLOCK, TOK_BLOCK), const(RET_HEADS, TOK_BLOCK, RET_QK_DIM),
        const(RET_HEADS, TOK_BLOCK, RET_QK_DIM),
    ]
    out_shape = [
        jax.ShapeDtypeStruct((t + TOK_BLOCK, D_MODEL), F32),
        jax.ShapeDtypeStruct((t, ROUTE_COLS), jnp.int32),
        jax.ShapeDtypeStruct((t, ROUTE_COLS), F32),
        jax.ShapeDtypeStruct((groups, N_EXPERTS, LANES_V7X), jnp.int32),
        jax.ShapeDtypeStruct((groups * GROUP_ROWS, D_MODEL), F32),
    ]
    out_specs = [
        pl.BlockSpec((TOK_BLOCK, D_MODEL), lambda i, yp: (i, 0)), route_prev, route_prev,
        pl.BlockSpec((1, N_EXPERTS, LANES_V7X), lambda i, yp: (prev(i), 0, 0)),
        pl.BlockSpec((GROUP_ROWS, D_MODEL), lambda i, yp: (prev(i), 0)),
    ]
    operands += [lw["w_in"], lw["p_ret"], lw["p_sg"], lw["w_o"], lw["sg_w"], lw["sg_bt"],
                 lw["sg_ln_g"], lw["sg_ln_b"], lw["ln1_g"], lw["ln1_b"], lw["router_wt"], lw["router_bb"],
                 tables["cq"], tables["sq"], tables["ck"], tables["sk"],
                 tables["dmat"], tables["qdec"], tables["kdec"]]
    kern = functools.partial(_mixer_kernel, chunk_decay=tables["chunk_decay"], steps_per_seq=ns,
                             n_groups=groups, fused=fused)
    grid_spec = pltpu.PrefetchScalarGridSpec(
        num_scalar_prefetch=1, grid=(groups + 1,), in_specs=in_specs, out_specs=out_specs,
        scratch_shapes=scratch)
    return pl.pallas_call(
        kern,
        grid_spec=grid_spec,
        out_shape=out_shape,
        compiler_params=pltpu.CompilerParams(dimension_semantics=("arbitrary",),
                                             vmem_limit_bytes=VMEM_LIMIT_V7X),
        name="mixer",
    )(ypiece, *operands)


def _expert_kernel(bexp_ref, wslot_ref, next_ref, src_ref, nvb_ref, xs_hbm, w1_hbm, b1_ref, w2_hbm, b2_ref,
                   yb_ref, xbuf, w1f, w2f, w1b, w2b, sem, wsem, *, layer):
    b = pl.program_id(0)
    nvb = nvb_ref[0]

    def fetch_weights(e, p):
        pltpu.make_async_copy(w1_hbm.at[layer, e], w1f.at[p], wsem.at[p]).start()
        pltpu.make_async_copy(w2_hbm.at[layer, e], w2f.at[p], wsem.at[p]).start()

    def wait_weights(p):
        pltpu.make_async_copy(w1_hbm.at[layer, 0], w1f.at[p], wsem.at[p]).wait()
        pltpu.make_async_copy(w2_hbm.at[layer, 0], w2f.at[p], wsem.at[p]).wait()

    def gather(blk, slot, pieces=range(EXP_PIECES)):
        for i in pieces:
            row = pl.multiple_of(src_ref[blk * EXP_PIECES + i] * PIECE, PIECE)
            pltpu.make_async_copy(xs_hbm.at[pl.ds(row, PIECE), :],
                                  xbuf.at[slot, pl.ds(i * PIECE, PIECE), :],
                                  sem.at[slot]).start()

    def wait(slot):
        pltpu.make_async_copy(xs_hbm.at[pl.ds(0, EXP_BLOCK), :], xbuf.at[slot], sem.at[slot]).wait()

    @pl.when(b == 0)
    def _():
        gather(0, 0)
        fetch_weights(bexp_ref[0], 0)

    new_expert = (b == 0) | (bexp_ref[b] != bexp_ref[jnp.maximum(b - 1, 0)])

    @pl.when(new_expert & (b < nvb))
    def _():
        p = wslot_ref[b]
        wait_weights(p)

        @pl.when(next_ref[b] >= 0)
        def _():
            fetch_weights(next_ref[b], 1 - p)

        def cast_rows(i, carry):
            r = pl.multiple_of(i * W_CAST_ROWS, W_CAST_ROWS)
            w1b[pl.ds(r, W_CAST_ROWS), :] = w1f[p, pl.ds(r, W_CAST_ROWS), :].astype(BF16)
            w2b[pl.ds(r, W_CAST_ROWS), :] = w2f[p, pl.ds(r, W_CAST_ROWS), :].astype(BF16)
            return carry
        lax.fori_loop(0, D_MODEL // W_CAST_ROWS, cast_rows, 0)

    @pl.when(b < nvb)
    def _():
        slot = b % 2
        nxt = 1 - slot
        nblk = jnp.minimum(b + 1, nvb - 1)
        wait(slot)
        n_chunks = D_EXPERT // F_CHUNK
        first = EXP_PIECES // 8
        batches = [range(0, first), range(first, EXP_PIECES)] + [range(0)] * (n_chunks - 2)
        y = jnp.zeros((EXP_BLOCK, D_MODEL), F32)
        for j in range(n_chunks):
            c0 = j * F_CHUNK
            gather(nblk, nxt, batches[j])
            x = xbuf[slot].astype(BF16)
            hg = _dot(x, w1b[:, c0:c0 + F_CHUNK]) + b1_ref[0, 0, :, c0:c0 + F_CHUNK]
            hu = (_dot(x, w1b[:, D_EXPERT + c0:D_EXPERT + c0 + F_CHUNK])
                  + b1_ref[0, 0, :, D_EXPERT + c0:D_EXPERT + c0 + F_CHUNK])
            gate = jnp.minimum(hg, SWIGLU_LIMIT)
            up = jnp.clip(hu, -SWIGLU_LIMIT, SWIGLU_LIMIT)
            act = gate * jax.nn.sigmoid(SWIGLU_ALPHA * gate) * (up + 1.0)
            y = y + _dot(act.astype(BF16), w2b[c0:c0 + F_CHUNK, :])
        yb_ref[...] = y + b2_ref[0, 0]

        @pl.when(b == nvb - 1)
        def _():
            wait(nxt)

    @pl.when(b >= nvb)
    def _():
        yb_ref[...] = jnp.zeros_like(yb_ref)


def _expert_call(xs, block_tables, layer, e_w1, e_b1, e_w2, e_b2, n_blocks):
    bias = lambda width: pl.BlockSpec((1, 1, 1, width), lambda b, be, *_: (layer, be[b], 0, 0))
    grid_spec = pltpu.PrefetchScalarGridSpec(
        num_scalar_prefetch=len(block_tables),
        grid=(n_blocks,),
        in_specs=[
            pl.BlockSpec(memory_space=pl.ANY),
            pl.BlockSpec(memory_space=pl.ANY), bias(2 * D_EXPERT),
            pl.BlockSpec(memory_space=pl.ANY), bias(D_MODEL),
        ],
        out_specs=pl.BlockSpec((EXP_BLOCK, D_MODEL), lambda b, *_: (b, 0)),
        scratch_shapes=[pltpu.VMEM((2, EXP_BLOCK, D_MODEL), F32),
                        pltpu.VMEM((2, D_MODEL, 2 * D_EXPERT), F32),
                        pltpu.VMEM((2, D_EXPERT, D_MODEL), F32),
                        pltpu.VMEM((D_MODEL, 2 * D_EXPERT), BF16),
                        pltpu.VMEM((D_EXPERT, D_MODEL), BF16),
                        pltpu.SemaphoreType.DMA((2,)),
                        pltpu.SemaphoreType.DMA((2,))],
    )
    return pl.pallas_call(
        functools.partial(_expert_kernel, layer=layer),
        grid_spec=grid_spec,
        out_shape=jax.ShapeDtypeStruct((n_blocks * EXP_BLOCK, D_MODEL), F32),
        compiler_params=pltpu.CompilerParams(dimension_semantics=("arbitrary",),
                                             vmem_limit_bytes=VMEM_LIMIT_V7X),
        name="experts",
    )(*block_tables, xs, e_w1, e_b1, e_w2, e_b2)


def _combine_kernel(yp_ref, x1_ref, yb_hbm, post_ref, gatet_ref, g_ref, b_ref, out_ref,
                    ybuf0, ybuf1, sem):
    g = pl.program_id(0)
    ng = pl.num_programs(0)
    ybufs = (ybuf0, ybuf1)

    def gather(grp, slot, pieces=range(GROUP_PIECES)):
        for i in pieces:
            row = pl.multiple_of(yp_ref[grp * GROUP_PIECES + i] * PIECE, PIECE)
            pltpu.make_async_copy(yb_hbm.at[pl.ds(row, PIECE), :],
                                  ybufs[slot].at[pl.ds(i * PIECE, PIECE), :],
                                  sem.at[slot]).start(priority=i % 2)

    def wait(slot):
        pltpu.make_async_copy(yb_hbm.at[pl.ds(0, GROUP_ROWS), :], ybufs[slot], sem.at[slot]).wait()

    @pl.when(g == 0)
    def _():
        gather(0, 0)

    def group(cur):
        nxt = 1 - cur
        gather(jnp.minimum(g + 1, ng - 1), nxt)
        slabs = _unsort_slabs(post_ref[...], gatet_ref[...])
        wait(cur)
        ffn = _unsort_matmul(slabs, ybufs[cur])
        out_ref[...] = _layer_norm(DN_ALPHA * x1_ref[...] + ffn, g_ref[...], b_ref[...])

        @pl.when(g == ng - 1)
        def _():
            wait(nxt)

    for cur in range(2):
        pl.when(g % 2 == cur)(functools.partial(group, cur))


def _combine_call(x1, yb, ypiece, post, gatet, ln_g, ln_b):
    t = post.shape[0]
    groups = t // TOK_BLOCK
    grid_spec = pltpu.PrefetchScalarGridSpec(
        num_scalar_prefetch=1,
        grid=(groups,),
        in_specs=[
            pl.BlockSpec((TOK_BLOCK, D_MODEL), lambda g, yp: (g, 0)),
            pl.BlockSpec(memory_space=pl.ANY),
            pl.BlockSpec((TOK_BLOCK, ROUTE_COLS), lambda g, yp: (g, 0)),
            pl.BlockSpec((TOK_BLOCK, ROUTE_COLS), lambda g, yp: (g, 0)),
            pl.BlockSpec((1, D_MODEL), lambda g, yp: (0, 0)),
            pl.BlockSpec((1, D_MODEL), lambda g, yp: (0, 0)),
        ],
        out_specs=pl.BlockSpec((TOK_BLOCK, D_MODEL), lambda g, yp: (g, 0)),
        scratch_shapes=[pltpu.VMEM((GROUP_ROWS, D_MODEL), F32), pltpu.VMEM((GROUP_ROWS, D_MODEL), F32),
                        pltpu.SemaphoreType.DMA((2,))],
    )
    return pl.pallas_call(
        _combine_kernel,
        grid_spec=grid_spec,
        out_shape=jax.ShapeDtypeStruct((t, D_MODEL), F32),
        compiler_params=pltpu.CompilerParams(dimension_semantics=("arbitrary",),
                                             vmem_limit_bytes=VMEM_LIMIT_V7X),
        name="combine",
    )(ypiece, x1, yb, post, gatet, ln_g, ln_b)


def _routing_tables(cnt_pad, n_blocks):
    groups = cnt_pad.shape[0]
    npc = cnt_pad // PIECE
    strip_start = jnp.cumsum(npc, axis=1) - npc
    tot = jnp.sum(npc, axis=0)
    tot_pad = (tot + EXP_PIECES - 1) // EXP_PIECES * EXP_PIECES
    exp_end = jnp.cumsum(tot_pad)
    exp_start = exp_end - tot_pad
    dstart = exp_start[None, :] + jnp.cumsum(npc, axis=0) - npc
    nvb = (exp_end[-1] // EXP_PIECES).astype(jnp.int32)

    blocks = jnp.arange(n_blocks, dtype=jnp.int32)
    bexp = jnp.sum(blocks[:, None] * EXP_PIECES >= exp_end[None, :], axis=1).astype(jnp.int32)
    bexp = jnp.minimum(bexp, N_EXPERTS - 1)
    last = jnp.sum((nvb - 1) * EXP_PIECES >= exp_end).astype(jnp.int32)
    bexp = jnp.where(blocks < nvb, bexp, jnp.minimum(last, N_EXPERTS - 1))
    prev_exp = jnp.concatenate([jnp.full((1,), -1, jnp.int32), bexp[:-1]])
    run_idx = jnp.cumsum((bexp != prev_exp).astype(jnp.int32)) - 1
    at_e = bexp[:, None] == jnp.arange(N_EXPERTS, dtype=jnp.int32)[None, :]
    run_end = jnp.sum(jnp.where(at_e, (exp_end // EXP_PIECES)[None, :], 0), axis=1)
    exp_at = jnp.sum(jnp.where(run_end[:, None] == blocks[None, :], bexp[None, :], 0), axis=1)
    next_exp = jnp.where(run_end < nvb, exp_at, -1).astype(jnp.int32)
    wslot = (run_idx % 2).astype(jnp.int32)

    d = jnp.arange(n_blocks * EXP_PIECES, dtype=jnp.int32)
    e_of = jnp.minimum(jnp.sum(d[:, None] >= exp_end[None, :], axis=1), N_EXPERTS - 1)
    oh_e = (e_of[:, None] == jnp.arange(N_EXPERTS, dtype=jnp.int32)[None, :]).astype(F32)
    row_of = lambda tab: jnp.dot(oh_e, tab.T.astype(F32), precision=lax.Precision.HIGHEST).astype(jnp.int32)
    ds_e = row_of(dstart)
    g_of = jnp.maximum(jnp.sum(ds_e <= d[:, None], axis=1) - 1, 0)
    at_g = g_of[:, None] == jnp.arange(groups, dtype=jnp.int32)[None, :]
    pick = lambda rows: jnp.sum(jnp.where(at_g, rows, 0), axis=1)
    i_of = d - pick(ds_e)
    valid = (i_of >= 0) & (i_of < pick(row_of(npc)))
    src = g_of * GROUP_PIECES + pick(row_of(strip_start)) + i_of
    src_piece = jnp.where(valid, src, 0).astype(jnp.int32)

    j = jnp.arange(GROUP_PIECES, dtype=jnp.int32)
    e_loc = jnp.maximum(jnp.sum(strip_start[:, None, :] <= j[None, :, None], axis=2) - 1, 0)
    at_e = e_loc[:, :, None] == jnp.arange(N_EXPERTS, dtype=jnp.int32)[None, None, :]
    pick_e = lambda tab: jnp.sum(jnp.where(at_e, tab[:, None, :], 0), axis=2)
    st_loc = pick_e(strip_start)
    np_loc = pick_e(npc)
    ds_loc = pick_e(dstart)
    i_loc = j[None, :] - st_loc
    ypiece = jnp.where(i_loc < np_loc, ds_loc + i_loc, 0).astype(jnp.int32).reshape(groups * GROUP_PIECES)
    return (bexp, wslot, next_exp, src_piece, nvb.reshape(1)), ypiece


def _tables(seq):
    half = RET_QK_DIM // 2
    inv = ROPE_BASE ** (-jnp.arange(half, dtype=F32) / half)
    ang = jnp.arange(seq, dtype=jnp.int32).astype(F32)[:, None] * inv[None, :]
    cos, sin = jnp.cos(ang), jnp.sin(ang)
    cosf = jnp.concatenate([cos, cos], axis=1)
    sinf = jnp.concatenate([-sin, sin], axis=1)
    kscale = RET_QK_DIM ** -0.5
    log_g = jnp.log(1.0 - jnp.exp(jnp.linspace(math.log(1.0 / 32), math.log(1.0 / 512), RET_HEADS)))
    idx = jnp.arange(TOK_BLOCK, dtype=F32)
    diff = idx[:, None] - idx[None, :]
    dmat = jnp.where(diff >= 0, jnp.exp(log_g[:, None, None] * jnp.maximum(diff, 0.0)), 0.0)
    qdec = jnp.exp(log_g[:, None] * (idx[None, :] + 1.0))
    kdec = jnp.exp(log_g[:, None] * (TOK_BLOCK - 1.0 - idx[None, :]))
    bc = lambda a: jnp.broadcast_to(a[:, :, None], (RET_HEADS, TOK_BLOCK, RET_QK_DIM))
    log_g_host = np.log(1.0 - np.exp(np.linspace(math.log(1.0 / 32), math.log(1.0 / 512), RET_HEADS)))
    chunk_decay = tuple(float(np.float32(np.exp(np.float32(lg) * np.float32(TOK_BLOCK)))) for lg in log_g_host)
    return {"cq": cosf, "sq": sinf, "ck": cosf * kscale, "sk": sinf * kscale,
            "dmat": dmat.astype(F32), "qdec": bc(qdec), "kdec": bc(kdec), "chunk_decay": chunk_decay}


def kernel(x, w_in, p_ret, sg_ln_g, sg_ln_b, sg_w, sg_b, p_sg, w_o, ln1_g, ln1_b,
           router_w, router_b, e_w1, e_b1, e_w2, e_b2, ln2_g, ln2_b):
    batch, seq, d = x.shape
    depth = w_in.shape[0]
    assert d == D_MODEL and seq % TOK_BLOCK == 0
    t = batch * seq
    groups = t // TOK_BLOCK
    used_pieces = (TOK_BLOCK * TOP_K + N_EXPERTS * (PIECE - 1)) // PIECE
    n_blocks = (groups * used_pieces + N_EXPERTS * (EXP_PIECES - 1)) // EXP_PIECES
    n_blocks = -(-n_blocks // SUBLANES_V7X) * SUBLANES_V7X
    tables = _tables(seq)
    xc = x.reshape(t, D_MODEL)
    w_in_b, p_ret_b, p_sg_b, w_o_b = (w.astype(BF16) for w in (w_in, p_ret, p_sg, w_o))
    e_b1r = e_b1.reshape(depth, N_EXPERTS, 1, 2 * D_EXPERT)
    e_b2r = e_b2.reshape(depth, N_EXPERTS, 1, D_MODEL)
    prev_moe = None
    for l in range(depth):
        lw = {
            "w_in": w_in_b, "p_ret": p_ret_b, "p_sg": p_sg_b,
            "w_o": w_o_b, "sg_w": sg_w[l], "sg_bt": sg_b[l].T,
            "sg_ln_g": sg_ln_g[l][None], "sg_ln_b": sg_ln_b[l][None],
            "ln1_g": ln1_g[l][None], "ln1_b": ln1_b[l][None],
            "router_wt": router_w[l].T,
            "router_bb": jnp.broadcast_to(router_b[l][:, None], (N_EXPERTS, TOK_BLOCK)),
        }
        x1, pos, gates, cnt, xs = _mixer_call(xc, batch, seq, l, lw, tables, prev_moe)
        block_tables, ypiece = _routing_tables(cnt[:, :, 0], n_blocks)
        yb = _expert_call(xs, block_tables, l, e_w1, e_b1r, e_w2, e_b2r, n_blocks)
        prev_moe = (yb, ypiece, pos, gates, ln2_g[l][None], ln2_b[l][None])
        xc = x1
    out = _combine_call(xc, *prev_moe)
    return out.reshape(batch, seq, D_MODEL)
```

```python
import functools
import math

import jax
import jax.numpy as jnp
import numpy as np
from jax import lax
from jax.experimental import pallas as pl
from jax.experimental.pallas import tpu as pltpu

F32 = jnp.float32
BF16 = jnp.bfloat16

D_MODEL = 1024
DEPTH = 4
RET_HEADS = 4
RET_QK_DIM = 128
RET_V_DIM = 256
RET_QK = RET_HEADS * RET_QK_DIM
RET_V = RET_HEADS * RET_V_DIM
ROPE_BASE = 10000.0
SG_BLOCK = 128
SG_CHUNK = 64
SG_GROUPS = 8
SG_WIDTH = D_MODEL
N_EXPERTS = 32
TOP_K = 4
D_EXPERT = D_MODEL
SWIGLU_LIMIT = 7.0
SWIGLU_ALPHA = 1.702
LN_EPS = 1e-5
DN_ALPHA = (2 * DEPTH) ** 0.25

OFF_Q = 0
OFF_K = OFF_Q + RET_QK
OFF_V = OFF_K + RET_QK
OFF_G = OFF_V + RET_V
OFF_U = OFF_G + RET_V
OFF_VS = OFF_U + SG_WIDTH
OFF_GA = OFF_VS + SG_WIDTH
OFF_GB = OFF_GA + D_MODEL
IN_WIDTH = OFF_GB + D_MODEL

SUBLANES_V7X = 8
LANES_V7X = 128
TOK_BLOCK = 256
PIECE = SUBLANES_V7X
GROUP_ROWS = TOK_BLOCK * TOP_K + N_EXPERTS * PIECE
GROUP_PIECES = GROUP_ROWS // PIECE
SORT_ROWS = 256
ROUTE_COLS = SUBLANES_V7X
EXP_BLOCK = 512
EXP_PIECES = EXP_BLOCK // PIECE
F_CHUNK = 512
W_CAST_ROWS = 128
VMEM_LIMIT_V7X = 60 * 1024 * 1024


def _layer_norm(x, g, b):
    mu = jnp.mean(x, axis=-1, keepdims=True)
    xc = x - mu
    var = jnp.mean(xc * xc, axis=-1, keepdims=True)
    return xc * lax.rsqrt(var + LN_EPS) * g + b


def _gelu_tanh(x):
    c = math.sqrt(2.0 / math.pi)
    return 0.5 * x * (1.0 + jnp.tanh(c * (x + 0.044715 * (x * x * x))))


def _dot(a, b):
    return jnp.dot(a, b, preferred_element_type=F32)


def _dot_nt(a, b):
    return lax.dot_general(a, b, (((1,), (1,)), ((), ())), preferred_element_type=F32)


def _dot_tn(a, b):
    return lax.dot_general(a, b, (((0,), (0,)), ((), ())), preferred_element_type=F32)


def _split_bf16(x):
    hi = x.astype(BF16)
    lo = (x - hi.astype(F32)).astype(BF16)
    return hi, lo


def _unsort_slab(post, gatet, r0):
    c_iota = lax.broadcasted_iota(jnp.int32, (TOK_BLOCK, SORT_ROWS), 1) + r0
    unsort = jnp.zeros((TOK_BLOCK, SORT_ROWS), F32)
    for k in range(TOP_K):
        unsort = jnp.where(c_iota == post[:, k:k + 1], gatet[:, k:k + 1], unsort)
    return unsort.astype(BF16)


def _unsort_slabs(post, gatet):
    return [_unsort_slab(post, gatet, r0) for r0 in range(0, GROUP_ROWS, SORT_ROWS)]


def _unsort_matmul(slabs, ybuf):
    ffn = jnp.zeros((TOK_BLOCK, D_MODEL), F32)
    for i, r0 in enumerate(range(0, GROUP_ROWS, SORT_ROWS)):
        ffn = ffn + _dot(slabs[i], ybuf[r0:r0 + SORT_ROWS, :].astype(BF16))
    return ffn


def _mixer_kernel(yp_ref, *refs, chunk_decay, steps_per_seq, n_groups, fused):
    if fused:
        x_ref, yb_hbm, post_ref, gatet_ref, ln2g_ref, ln2b_ref = refs[:6]
        refs = refs[6:]
        ybuf, sem = refs[-2:]
        refs = refs[:-2]
    else:
        x_ref = refs[0]
        refs = refs[1:]
    (w_in_ref, p_ret_ref, p_sg_ref, w_o_ref, sgw_ref, sgbt_ref,
     sglng_ref, sglnb_ref, ln1g_ref, ln1b_ref, rwt_ref, rbb_ref,
     cq_ref, sq_ref, ck_ref, sk_ref, dmat_ref, qdec_ref, kdec_ref,
     x1_ref, pos_ref, gate_ref, cnt_ref, xs_ref,
     state_ref, ysg_ref, x1_prev_ref) = refs
    ts = TOK_BLOCK
    step = pl.program_id(0)

    @pl.when(step == 0)
    def _():
        x1_prev_ref[...] = jnp.zeros_like(x1_prev_ref)

    @pl.when(step % steps_per_seq == 0)
    def _():
        state_ref[...] = jnp.zeros_like(state_ref)

    if fused:
        slot = step % 2

        def gather(grp, s):
            for i in range(GROUP_PIECES):
                row = pl.multiple_of(yp_ref[grp * GROUP_PIECES + i] * PIECE, PIECE)
                pltpu.make_async_copy(yb_hbm.at[pl.ds(row, PIECE), :],
                                      ybuf.at[s, pl.ds(i * PIECE, PIECE), :], sem.at[s]).start()

        def wait(s):
            pltpu.make_async_copy(yb_hbm.at[pl.ds(0, GROUP_ROWS), :], ybuf.at[s], sem.at[s]).wait()

        @pl.when(step == 0)
        def _():
            gather(0, 0)

    routing = _route_and_sort(x1_prev_ref[...], rwt_ref, rbb_ref, pos_ref, gate_ref, cnt_ref, xs_ref)
    routing_stage = lambda: next(routing, None)

    if fused:
        slabs = _unsort_slabs(post_ref[...], gatet_ref[...])
        wait(slot)
        ffn = _unsort_matmul(slabs, ybuf.at[slot])
        xb = _layer_norm(DN_ALPHA * x_ref[...] + ffn, ln2g_ref[...], ln2b_ref[...])
        gather(jnp.minimum(step + 1, n_groups - 1), 1 - slot)
    else:
        xb = x_ref[...]
    xbf = xb.astype(BF16)

    def proj(off, width):
        return _dot(xbf, w_in_ref[0, :, off:off + width])

    cq, sq, ck, sk = cq_ref[...], sq_ref[...], ck_ref[...], sk_ref[...]
    acc_ret = jnp.zeros((ts, D_MODEL), F32)
    pair, head_in = {}, {}

    def project_head(h):
        if h % 2 == 0:
            pair["q"] = proj(OFF_Q + h * RET_QK_DIM, 2 * RET_QK_DIM)
            yield
            pair["k"] = proj(OFF_K + h * RET_QK_DIM, 2 * RET_QK_DIM)
            yield
        lanes = slice((h % 2) * RET_QK_DIM, (h % 2 + 1) * RET_QK_DIM)
        v = proj(OFF_V + h * RET_V_DIM, RET_V_DIM).astype(BF16)
        yield
        g = proj(OFF_G + h * RET_V_DIM, RET_V_DIM)
        head_in[h] = (pair["q"][:, lanes], pair["k"][:, lanes], v, g)
        yield

    for _ in project_head(0):
        pass
    for h in range(RET_HEADS):
        upcoming = project_head(h + 1) if h + 1 < RET_HEADS else iter(())
        q, k, v, g = head_in.pop(h)
        routing_stage()
        next(upcoming, None)
        qr = q * cq + pltpu.roll(q, RET_QK_DIM // 2, 1) * sq
        kr = k * ck + pltpu.roll(k, RET_QK_DIM // 2, 1) * sk
        scores = _dot_nt(qr.astype(BF16), kr.astype(BF16)) * dmat_ref[h]
        next(upcoming, None)
        inner = _dot(scores.astype(BF16), v)
        st = state_ref[h]
        cross = _dot((qr * qdec_ref[h]).astype(BF16), st.astype(BF16))
        kd = (kr * kdec_ref[h]).astype(BF16)
        state_ref[h] = st * chunk_decay[h] + _dot_tn(kd, v)
        routing_stage()
        next(upcoming, None)
        y = inner + cross
        mu = jnp.mean(y, axis=-1, keepdims=True)
        yc = y - mu
        var = jnp.mean(yc * yc, axis=-1, keepdims=True)
        yn = yc * lax.rsqrt(var + LN_EPS)
        next(upcoming, None)
        yr = yn * (g * jax.nn.sigmoid(g))
        acc_ret = acc_ret + _dot(yr.astype(BF16), p_ret_ref[0, h * RET_V_DIM:(h + 1) * RET_V_DIM, :])
        for _ in upcoming:
            pass
        if fused and h == 1:
            @pl.when(step == n_groups)
            def _():
                wait(1 - slot)

    pre_u = proj(OFF_U, SG_WIDTH)
    pre_vs = proj(OFF_VS, SG_WIDTH)
    ga = proj(OFF_GA, D_MODEL)
    u = _gelu_tanh(pre_u)
    routing_stage()
    gb = proj(OFF_GB, D_MODEL)
    vn = _layer_norm(_gelu_tanh(pre_vs), sglng_ref[...], sglnb_ref[...])
    qi = lax.broadcasted_iota(jnp.int32, (SG_BLOCK, SG_BLOCK), 0) // SG_CHUNK
    pi = lax.broadcasted_iota(jnp.int32, (SG_BLOCK, SG_BLOCK), 1) // SG_CHUNK
    sg_mask = qi >= pi
    for gi in range(SG_GROUPS):
        wm = jnp.where(sg_mask, sgw_ref[gi], 0.0).astype(BF16)
        bias = sgbt_ref[:, gi:gi + 1]
        cols = slice(gi * SG_BLOCK, (gi + 1) * SG_BLOCK)
        blocks = [slice(blk * SG_BLOCK, (blk + 1) * SG_BLOCK) for blk in range(ts // SG_BLOCK)]
        vcat = jnp.concatenate([vn[rows, cols] for rows in blocks], axis=1).astype(BF16)
        mixed = _dot(wm, vcat) + bias
        for blk, rows in enumerate(blocks):
            ysg_ref[rows, cols] = (u[rows, cols] * mixed[:, blk * SG_BLOCK:(blk + 1) * SG_BLOCK]).astype(BF16)
    routing_stage()
    acc_sg = _dot(ysg_ref[...], p_sg_ref[0])

    hmix = jax.nn.sigmoid(ga) * acc_ret + jax.nn.sigmoid(gb) * acc_sg
    mix = _dot(hmix.astype(BF16), w_o_ref[0])
    for _ in routing:
        pass
    x1 = _layer_norm(DN_ALPHA * xb + mix, ln1g_ref[...], ln1b_ref[...])
    x1_ref[...] = x1
    x1_prev_ref[...] = x1


def _route_and_sort(x1, rwt_ref, rbb_ref, pos_ref, gate_ref, cnt_ref, xs_ref):
    ts = TOK_BLOCK
    x_hi, x_lo = _split_bf16(x1)
    w_hi, w_lo = _split_bf16(rwt_ref[...])
    logits = (_dot_nt(w_hi, x_hi) + _dot_nt(w_hi, x_lo) + _dot_nt(w_lo, x_hi)) + rbb_ref[...]
    yield
    e_iota = lax.broadcasted_iota(jnp.int32, (N_EXPERTS, ts), 0).astype(F32)
    sels, vals = [], []
    work = logits
    for _ in range(TOP_K):
        m = jnp.max(work, axis=0, keepdims=True)
        idx = jnp.min(jnp.where(work == m, e_iota, float(N_EXPERTS)), axis=0, keepdims=True)
        sel = e_iota == idx
        work = jnp.where(sel, -jnp.inf, work)
        sels.append(sel)
        vals.append(m)
        yield
    exps = [jnp.exp(v - vals[0]) for v in vals]
    denom = exps[0] + exps[1] + exps[2] + exps[3]
    pad_rows = jnp.zeros((ROUTE_COLS - TOP_K, ts), F32)
    gate_ref[...] = jnp.concatenate([e / denom for e in exps] + [pad_rows], axis=0).T

    onehots = [jnp.where(s, 1.0, 0.0) for s in sels]
    oh_sum = onehots[0] + onehots[1] + onehots[2] + onehots[3]
    t_row = lax.broadcasted_iota(jnp.int32, (ts, ts), 0)
    t_col = lax.broadcasted_iota(jnp.int32, (ts, ts), 1)
    upper = jnp.where(t_row < t_col, 1.0, 0.0).astype(BF16)
    before = _dot(oh_sum.astype(BF16), upper)
    cnt = jnp.sum(oh_sum, axis=1, keepdims=True)
    cnt_pad = jnp.floor((cnt + (PIECE - 1)) * (1.0 / PIECE)) * PIECE
    cnt_b = jnp.broadcast_to(cnt_pad, (N_EXPERTS, ts))
    e_row = lax.broadcasted_iota(jnp.int32, (N_EXPERTS, N_EXPERTS), 0)
    e_col = lax.broadcasted_iota(jnp.int32, (N_EXPERTS, N_EXPERTS), 1)
    lower = jnp.where(e_col < e_row, 1.0, 0.0).astype(BF16)
    strip_off = _dot(lower, cnt_b.astype(BF16))
    slot = strip_off + before
    pos_f = [jnp.sum(oh * slot, axis=0, keepdims=True) for oh in onehots]
    pos_ref[...] = jnp.concatenate(pos_f + [pad_rows], axis=0).T.astype(jnp.int32)
    pos = [p.astype(jnp.int32) for p in pos_f]
    cnt_ref[0] = cnt_b[:, :LANES_V7X].astype(jnp.int32)
    yield
    for r0 in range(0, GROUP_ROWS, SORT_ROWS):
        r_iota = lax.broadcasted_iota(jnp.int32, (SORT_ROWS, ts), 0) + r0
        perm = jnp.zeros((SORT_ROWS, ts), F32)
        for k in range(TOP_K):
            perm = jnp.where(r_iota == pos[k], 1.0, perm)
        perm = perm.astype(BF16)
        xs_ref[r0:r0 + SORT_ROWS, :] = _dot(perm, x_hi)
        yield


def _mixer_call(x2d, batch, seq, layer, lw, tables, prev_moe=None):
    t = batch * seq
    ns = seq // TOK_BLOCK
    groups = t // TOK_BLOCK
    fused = prev_moe is not None
    const = lambda *shape: pl.BlockSpec(shape, lambda i, yp: (0,) * len(shape),
                                        pipeline_mode=pl.Buffered(1))
    stacked = lambda *shape: pl.BlockSpec((1,) + shape, lambda i, yp: (layer, 0, 0),
                                          pipeline_mode=pl.Buffered(1))
    cur = lambda i: jnp.minimum(i, groups - 1)
    prev = lambda i: jnp.maximum(i - 1, 0)
    seqtab = pl.BlockSpec((TOK_BLOCK, RET_QK_DIM), lambda i, yp: (cur(i) % ns, 0))
    route_prev = pl.BlockSpec((TOK_BLOCK, ROUTE_COLS), lambda i, yp: (prev(i), 0))
    tok_cur = lambda width: pl.BlockSpec((TOK_BLOCK, width), lambda i, yp: (cur(i), 0))
    in_specs = [tok_cur(D_MODEL)]
    operands = [x2d]
    scratch = [pltpu.VMEM((RET_HEADS, RET_QK_DIM, RET_V_DIM), F32),
               pltpu.VMEM((TOK_BLOCK, SG_WIDTH), BF16),
               pltpu.VMEM((TOK_BLOCK, D_MODEL), F32)]
    if fused:
        yb, ypiece, post, gatet, ln2_g, ln2_b = prev_moe
        in_specs += [pl.BlockSpec(memory_space=pl.ANY), tok_cur(ROUTE_COLS), tok_cur(ROUTE_COLS),
                     const(1, D_MODEL), const(1, D_MODEL)]
        operands += [yb, post, gatet, ln2_g, ln2_b]
        scratch += [pltpu.VMEM((2, GROUP_ROWS, D_MODEL), F32), pltpu.SemaphoreType.DMA((2,))]
    else:
        ypiece = jnp.zeros((1,), jnp.int32)
    in_specs += [
        stacked(D_MODEL, IN_WIDTH), stacked(RET_V, D_MODEL), stacked(SG_WIDTH, D_MODEL),
        stacked(D_MODEL, D_MODEL),
        const(SG_GROUPS, SG_BLOCK, SG_BLOCK), const(SG_BLOCK, SG_GROUPS),
        const(1, SG_WIDTH), const(1, SG_WIDTH), const(1, D_MODEL), const(1, D_MODEL),
        const(N_EXPERTS, D_MODEL), const(N_EXPERTS, TOK_BLOCK),
        seqtab, seqtab, seqtab, seqtab,
        const(RET_HEADS, TOK_BLOCK, TOK_BLOCK), const(RET_HEADS, TOK_BLOCK, RET_QK_DIM),
        const(RET_HEADS, TOK_BLOCK, RET_QK_DIM),
    ]
    out_shape = [
        jax.ShapeDtypeStruct((t + TOK_BLOCK, D_MODEL), F32),
        jax.ShapeDtypeStruct((t, ROUTE_COLS), jnp.int32),
        jax.ShapeDtypeStruct((t, ROUTE_COLS), F32),
        jax.ShapeDtypeStruct((groups, N_EXPERTS, LANES_V7X), jnp.int32),
        jax.ShapeDtypeStruct((groups * GROUP_ROWS, D_MODEL), F32),
    ]
    out_specs = [
        pl.BlockSpec((TOK_BLOCK, D_MODEL), lambda i, yp: (i, 0)), route_prev, route_prev,
        pl.BlockSpec((1, N_EXPERTS, LANES_V7X), lambda i, yp: (prev(i), 0, 0)),
        pl.BlockSpec((GROUP_ROWS, D_MODEL), lambda i, yp: (prev(i), 0)),
    ]
    operands += [lw["w_in"], lw["p_ret"], lw["p_sg"], lw["w_o"], lw["sg_w"], lw["sg_bt"],
                 lw["sg_ln_g"], lw["sg_ln_b"], lw["ln1_g"], lw["ln1_b"], lw["router_wt"], lw["router_bb"],
                 tables["cq"], tables["sq"], tables["ck"], tables["sk"],
                 tables["dmat"], tables["qdec"], tables["kdec"]]
    kern = functools.partial(_mixer_kernel, chunk_decay=tables["chunk_decay"], steps_per_seq=ns,
                             n_groups=groups, fused=fused)
    grid_spec = pltpu.PrefetchScalarGridSpec(
        num_scalar_prefetch=1, grid=(groups + 1,), in_specs=in_specs, out_specs=out_specs,
        scratch_shapes=scratch)
    return pl.pallas_call(
        kern,
        grid_spec=grid_spec,
        out_shape=out_shape,
        compiler_params=pltpu.CompilerParams(dimension_semantics=("arbitrary",),
                                             vmem_limit_bytes=VMEM_LIMIT_V7X),
        name="mixer",
    )(ypiece, *operands)


def _expert_kernel(bexp_ref, wslot_ref, next_ref, src_ref, nvb_ref, xs_hbm, w1_hbm, b1_ref, w2_hbm, b2_ref,
                   yb_ref, xbuf, w1f, w2f, w1b, w2b, sem, wsem, *, layer):
    b = pl.program_id(0)
    nvb = nvb_ref[0]

    def fetch_weights(e, p):
        pltpu.make_async_copy(w1_hbm.at[layer, e], w1f.at[p], wsem.at[p]).start(priority=1)
        pltpu.make_async_copy(w2_hbm.at[layer, e], w2f.at[p], wsem.at[p]).start(priority=1)

    def wait_weights(p):
        pltpu.make_async_copy(w1_hbm.at[layer, 0], w1f.at[p], wsem.at[p]).wait()
        pltpu.make_async_copy(w2_hbm.at[layer, 0], w2f.at[p], wsem.at[p]).wait()

    def gather(blk, slot, pieces=range(EXP_PIECES)):
        for i in pieces:
            row = pl.multiple_of(src_ref[blk * EXP_PIECES + i] * PIECE, PIECE)
            pltpu.make_async_copy(xs_hbm.at[pl.ds(row, PIECE), :],
                                  xbuf.at[slot, pl.ds(i * PIECE, PIECE), :],
                                  sem.at[slot]).start()

    def wait(slot):
        pltpu.make_async_copy(xs_hbm.at[pl.ds(0, EXP_BLOCK), :], xbuf.at[slot], sem.at[slot]).wait()

    @pl.when(b == 0)
    def _():
        gather(0, 0)
        fetch_weights(bexp_ref[0], 0)

    new_expert = (b == 0) | (bexp_ref[b] != bexp_ref[jnp.maximum(b - 1, 0)])

    @pl.when(new_expert & (b < nvb))
    def _():
        p = wslot_ref[b]
        wait_weights(p)

        @pl.when(next_ref[b] >= 0)
        def _():
            fetch_weights(next_ref[b], 1 - p)

        def cast_rows(i, carry):
            r = pl.multiple_of(i * W_CAST_ROWS, W_CAST_ROWS)
            w1b[pl.ds(r, W_CAST_ROWS), :] = w1f[p, pl.ds(r, W_CAST_ROWS), :].astype(BF16)
            w2b[pl.ds(r, W_CAST_ROWS), :] = w2f[p, pl.ds(r, W_CAST_ROWS), :].astype(BF16)
            return carry
        lax.fori_loop(0, D_MODEL // W_CAST_ROWS, cast_rows, 0)

    @pl.when(b < nvb)
    def _():
        slot = b % 2
        nxt = 1 - slot
        nblk = jnp.minimum(b + 1, nvb - 1)
        wait(slot)
        n_chunks = D_EXPERT // F_CHUNK
        first = EXP_PIECES // 8
        batches = [range(0, first), range(first, EXP_PIECES)] + [range(0)] * (n_chunks - 2)
        y = jnp.zeros((EXP_BLOCK, D_MODEL), F32)
        for j in range(n_chunks):
            c0 = j * F_CHUNK
            gather(nblk, nxt, batches[j])
            x = xbuf[slot].astype(BF16)
            hg = _dot(x, w1b[:, c0:c0 + F_CHUNK]) + b1_ref[0, 0, :, c0:c0 + F_CHUNK]
            hu = (_dot(x, w1b[:, D_EXPERT + c0:D_EXPERT + c0 + F_CHUNK])
                  + b1_ref[0, 0, :, D_EXPERT + c0:D_EXPERT + c0 + F_CHUNK])
            gate = jnp.minimum(hg, SWIGLU_LIMIT)
            up = jnp.clip(hu, -SWIGLU_LIMIT, SWIGLU_LIMIT)
            act = gate * jax.nn.sigmoid(SWIGLU_ALPHA * gate) * (up + 1.0)
            y = y + _dot(act.astype(BF16), w2b[c0:c0 + F_CHUNK, :])
        yb_ref[...] = y + b2_ref[0, 0]

        @pl.when(b == nvb - 1)
        def _():
            wait(nxt)

    @pl.when(b >= nvb)
    def _():
        yb_ref[...] = jnp.zeros_like(yb_ref)


def _expert_call(xs, block_tables, layer, e_w1, e_b1, e_w2, e_b2, n_blocks):
    bias = lambda width: pl.BlockSpec((1, 1, 1, width), lambda b, be, *_: (layer, be[b], 0, 0))
    grid_spec = pltpu.PrefetchScalarGridSpec(
        num_scalar_prefetch=len(block_tables),
        grid=(n_blocks,),
        in_specs=[
            pl.BlockSpec(memory_space=pl.ANY),
            pl.BlockSpec(memory_space=pl.ANY), bias(2 * D_EXPERT),
            pl.BlockSpec(memory_space=pl.ANY), bias(D_MODEL),
        ],
        out_specs=pl.BlockSpec((EXP_BLOCK, D_MODEL), lambda b, *_: (b, 0)),
        scratch_shapes=[pltpu.VMEM((2, EXP_BLOCK, D_MODEL), F32),
                        pltpu.VMEM((2, D_MODEL, 2 * D_EXPERT), F32),
                        pltpu.VMEM((2, D_EXPERT, D_MODEL), F32),
                        pltpu.VMEM((D_MODEL, 2 * D_EXPERT), BF16),
                        pltpu.VMEM((D_EXPERT, D_MODEL), BF16),
                        pltpu.SemaphoreType.DMA((2,)),
                        pltpu.SemaphoreType.DMA((2,))],
    )
    return pl.pallas_call(
        functools.partial(_expert_kernel, layer=layer),
        grid_spec=grid_spec,
        out_shape=jax.ShapeDtypeStruct((n_blocks * EXP_BLOCK, D_MODEL), F32),
        compiler_params=pltpu.CompilerParams(dimension_semantics=("arbitrary",),
                                             vmem_limit_bytes=VMEM_LIMIT_V7X),
        name="experts",
    )(*block_tables, xs, e_w1, e_b1, e_w2, e_b2)


def _combine_kernel(yp_ref, x1_ref, yb_hbm, post_ref, gatet_ref, g_ref, b_ref, out_ref,
                    ybuf0, ybuf1, sem):
    g = pl.program_id(0)
    ng = pl.num_programs(0)
    ybufs = (ybuf0, ybuf1)

    def gather(grp, slot, pieces=range(GROUP_PIECES)):
        for i in pieces:
            row = pl.multiple_of(yp_ref[grp * GROUP_PIECES + i] * PIECE, PIECE)
            pltpu.make_async_copy(yb_hbm.at[pl.ds(row, PIECE), :],
                                  ybufs[slot].at[pl.ds(i * PIECE, PIECE), :],
                                  sem.at[slot]).start(priority=i % 2)

    def wait(slot):
        pltpu.make_async_copy(yb_hbm.at[pl.ds(0, GROUP_ROWS), :], ybufs[slot], sem.at[slot]).wait()

    @pl.when(g == 0)
    def _():
        gather(0, 0)

    def group(cur):
        nxt = 1 - cur
        gather(jnp.minimum(g + 1, ng - 1), nxt)
        slabs = _unsort_slabs(post_ref[...], gatet_ref[...])
        wait(cur)
        ffn = _unsort_matmul(slabs, ybufs[cur])
        out_ref[...] = _layer_norm(DN_ALPHA * x1_ref[...] + ffn, g_ref[...], b_ref[...])

        @pl.when(g == ng - 1)
        def _():
            wait(nxt)

    for cur in range(2):
        pl.when(g % 2 == cur)(functools.partial(group, cur))


def _combine_call(x1, yb, ypiece, post, gatet, ln_g, ln_b):
    t = post.shape[0]
    groups = t // TOK_BLOCK
    grid_spec = pltpu.PrefetchScalarGridSpec(
        num_scalar_prefetch=1,
        grid=(groups,),
        in_specs=[
            pl.BlockSpec((TOK_BLOCK, D_MODEL), lambda g, yp: (g, 0)),
            pl.BlockSpec(memory_space=pl.ANY),
            pl.BlockSpec((TOK_BLOCK, ROUTE_COLS), lambda g, yp: (g, 0)),
            pl.BlockSpec((TOK_BLOCK, ROUTE_COLS), lambda g, yp: (g, 0)),
            pl.BlockSpec((1, D_MODEL), lambda g, yp: (0, 0)),
            pl.BlockSpec((1, D_MODEL), lambda g, yp: (0, 0)),
        ],
        out_specs=pl.BlockSpec((TOK_BLOCK, D_MODEL), lambda g, yp: (g, 0)),
        scratch_shapes=[pltpu.VMEM((GROUP_ROWS, D_MODEL), F32), pltpu.VMEM((GROUP_ROWS, D_MODEL), F32),
                        pltpu.SemaphoreType.DMA((2,))],
    )
    return pl.pallas_call(
        _combine_kernel,
        grid_spec=grid_spec,
        out_shape=jax.ShapeDtypeStruct((t, D_MODEL), F32),
        compiler_params=pltpu.CompilerParams(dimension_semantics=("arbitrary",),
                                             vmem_limit_bytes=VMEM_LIMIT_V7X),
        name="combine",
    )(ypiece, x1, yb, post, gatet, ln_g, ln_b)


def _routing_tables(cnt_pad, n_blocks):
    groups = cnt_pad.shape[0]
    npc = cnt_pad // PIECE
    strip_start = jnp.cumsum(npc, axis=1) - npc
    tot = jnp.sum(npc, axis=0)
    tot_pad = (tot + EXP_PIECES - 1) // EXP_PIECES * EXP_PIECES
    exp_end = jnp.cumsum(tot_pad)
    exp_start = exp_end - tot_pad
    dstart = exp_start[None, :] + jnp.cumsum(npc, axis=0) - npc
    nvb = (exp_end[-1] // EXP_PIECES).astype(jnp.int32)

    blocks = jnp.arange(n_blocks, dtype=jnp.int32)
    bexp = jnp.sum(blocks[:, None] * EXP_PIECES >= exp_end[None, :], axis=1).astype(jnp.int32)
    bexp = jnp.minimum(bexp, N_EXPERTS - 1)
    last = jnp.sum((nvb - 1) * EXP_PIECES >= exp_end).astype(jnp.int32)
    bexp = jnp.where(blocks < nvb, bexp, jnp.minimum(last, N_EXPERTS - 1))
    prev_exp = jnp.concatenate([jnp.full((1,), -1, jnp.int32), bexp[:-1]])
    run_idx = jnp.cumsum((bexp != prev_exp).astype(jnp.int32)) - 1
    at_e = bexp[:, None] == jnp.arange(N_EXPERTS, dtype=jnp.int32)[None, :]
    run_end = jnp.sum(jnp.where(at_e, (exp_end // EXP_PIECES)[None, :], 0), axis=1)
    exp_at = jnp.sum(jnp.where(run_end[:, None] == blocks[None, :], bexp[None, :], 0), axis=1)
    next_exp = jnp.where(run_end < nvb, exp_at, -1).astype(jnp.int32)
    wslot = (run_idx % 2).astype(jnp.int32)

    d = jnp.arange(n_blocks * EXP_PIECES, dtype=jnp.int32)
    e_of = jnp.minimum(jnp.sum(d[:, None] >= exp_end[None, :], axis=1), N_EXPERTS - 1)
    oh_e = (e_of[:, None] == jnp.arange(N_EXPERTS, dtype=jnp.int32)[None, :]).astype(F32)
    row_of = lambda tab: jnp.dot(oh_e, tab.T.astype(F32), precision=lax.Precision.HIGHEST).astype(jnp.int32)
    ds_e = row_of(dstart)
    g_of = jnp.maximum(jnp.sum(ds_e <= d[:, None], axis=1) - 1, 0)
    at_g = g_of[:, None] == jnp.arange(groups, dtype=jnp.int32)[None, :]
    pick = lambda rows: jnp.sum(jnp.where(at_g, rows, 0), axis=1)
    i_of = d - pick(ds_e)
    valid = (i_of >= 0) & (i_of < pick(row_of(npc)))
    src = g_of * GROUP_PIECES + pick(row_of(strip_start)) + i_of
    src_piece = jnp.where(valid, src, 0).astype(jnp.int32)

    j = jnp.arange(GROUP_PIECES, dtype=jnp.int32)
    e_loc = jnp.maximum(jnp.sum(strip_start[:, None, :] <= j[None, :, None], axis=2) - 1, 0)
    at_e = e_loc[:, :, None] == jnp.arange(N_EXPERTS, dtype=jnp.int32)[None, None, :]
    pick_e = lambda tab: jnp.sum(jnp.where(at_e, tab[:, None, :], 0), axis=2)
    st_loc = pick_e(strip_start)
    np_loc = pick_e(npc)
    ds_loc = pick_e(dstart)
    i_loc = j[None, :] - st_loc
    ypiece = jnp.where(i_loc < np_loc, ds_loc + i_loc, 0).astype(jnp.int32).reshape(groups * GROUP_PIECES)
    return (bexp, wslot, next_exp, src_piece, nvb.reshape(1)), ypiece


def _tables(seq):
    half = RET_QK_DIM // 2
    inv = ROPE_BASE ** (-jnp.arange(half, dtype=F32) / half)
    ang = jnp.arange(seq, dtype=jnp.int32).astype(F32)[:, None] * inv[None, :]
    cos, sin = jnp.cos(ang), jnp.sin(ang)
    cosf = jnp.concatenate([cos, cos], axis=1)
    sinf = jnp.concatenate([-sin, sin], axis=1)
    kscale = RET_QK_DIM ** -0.5
    log_g = jnp.log(1.0 - jnp.exp(jnp.linspace(math.log(1.0 / 32), math.log(1.0 / 512), RET_HEADS)))
    idx = jnp.arange(TOK_BLOCK, dtype=F32)
    diff = idx[:, None] - idx[None, :]
    dmat = jnp.where(diff >= 0, jnp.exp(log_g[:, None, None] * jnp.maximum(diff, 0.0)), 0.0)
    qdec = jnp.exp(log_g[:, None] * (idx[None, :] + 1.0))
    kdec = jnp.exp(log_g[:, None] * (TOK_BLOCK - 1.0 - idx[None, :]))
    bc = lambda a: jnp.broadcast_to(a[:, :, None], (RET_HEADS, TOK_BLOCK, RET_QK_DIM))
    log_g_host = np.log(1.0 - np.exp(np.linspace(math.log(1.0 / 32), math.log(1.0 / 512), RET_HEADS)))
    chunk_decay = tuple(float(np.float32(np.exp(np.float32(lg) * np.float32(TOK_BLOCK)))) for lg in log_g_host)
    return {"cq": cosf, "sq": sinf, "ck": cosf * kscale, "sk": sinf * kscale,
            "dmat": dmat.astype(F32), "qdec": bc(qdec), "kdec": bc(kdec), "chunk_decay": chunk_decay}


def kernel(x, w_in, p_ret, sg_ln_g, sg_ln_b, sg_w, sg_b, p_sg, w_o, ln1_g, ln1_b,
           router_w, router_b, e_w1, e_b1, e_w2, e_b2, ln2_g, ln2_b):
    batch, seq, d = x.shape
    depth = w_in.shape[0]
    assert d == D_MODEL and seq % TOK_BLOCK == 0
    t = batch * seq
    groups = t // TOK_BLOCK
    used_pieces = (TOK_BLOCK * TOP_K + N_EXPERTS * (PIECE - 1)) // PIECE
    n_blocks = (groups * used_pieces + N_EXPERTS * (EXP_PIECES - 1)) // EXP_PIECES
    n_blocks = -(-n_blocks // SUBLANES_V7X) * SUBLANES_V7X
    tables = _tables(seq)
    xc = x.reshape(t, D_MODEL)
    w_in_b, p_ret_b, p_sg_b, w_o_b = (w.astype(BF16) for w in (w_in, p_ret, p_sg, w_o))
    e_b1r = e_b1.reshape(depth, N_EXPERTS, 1, 2 * D_EXPERT)
    e_b2r = e_b2.reshape(depth, N_EXPERTS, 1, D_MODEL)
    prev_moe = None
    for l in range(depth):
        lw = {
            "w_in": w_in_b, "p_ret": p_ret_b, "p_sg": p_sg_b,
            "w_o": w_o_b, "sg_w": sg_w[l], "sg_bt": sg_b[l].T,
            "sg_ln_g": sg_ln_g[l][None], "sg_ln_b": sg_ln_b[l][None],
            "ln1_g": ln1_g[l][None], "ln1_b": ln1_b[l][None],
            "router_wt": router_w[l].T,
            "router_bb": jnp.broadcast_to(router_b[l][:, None], (N_EXPERTS, TOK_BLOCK)),
        }
        x1, pos, gates, cnt, xs = _mixer_call(xc, batch, seq, l, lw, tables, prev_moe)
        block_tables, ypiece = _routing_tables(cnt[:, :, 0], n_blocks)
        yb = _expert_call(xs, block_tables, l, e_w1, e_b1r, e_w2, e_b2r, n_blocks)
        prev_moe = (yb, ypiece, pos, gates, ln2_g[l][None], ln2_b[l][None])
        xc = x1
    out = _combine_call(xc, *prev_moe)
    return out.reshape(batch, seq, D_MODEL)
```
